```python
import math
import jax
import jax.numpy as jnp
from jax import lax
import numpy as np

D_MODEL = 2048
BATCH = 1
SEQ = 8192
DEPTH = 2

GRID_W = 64
CTX_LEN = 256
MIX_W = D_MODEL
GROUP_W = MIX_W // 4
POOL_WINDOWS = (2, 4, 8, 16)
POOL_GW = GROUP_W // len(POOL_WINDOWS)
DIFF_HEADS = 8
DIFF_QK = 32
DIFF_V = GROUP_W // DIFF_HEADS
ROPE_AXIS = DIFF_QK // 2
ROPE_BASE = 10000.0
ATTN_BLOCK = 128
CONV_WIDTH = 31
S5_P = 16
S5_GROUPS = GROUP_W // S5_P
S5_N = 64
D_FF = 5632
N_EXPERTS = 8
TOP_K = 2
ALPHA = (2.0 * DEPTH) ** 0.25
BETA = (8.0 * DEPTH) ** -0.25
LN_EPS = 1e-5
POOL_OFF = 0
Q_OFF = POOL_OFF + GROUP_W
K_OFF = Q_OFF + DIFF_HEADS * 2 * DIFF_QK
V_OFF = K_OFF + DIFF_HEADS * 2 * DIFF_QK
CONV_OFF = V_OFF + DIFF_HEADS * DIFF_V
S5_OFF = CONV_OFF + 2 * GROUP_W
IN_W = S5_OFF + GROUP_W

kernel_name = 'hybrid_parallel_mixer_diffusion_block'


def layer_norm(x, g, b):
    xf = x.astype(jnp.float32)
    mu = jnp.mean(xf, -1, keepdims=True)
    xc = xf - mu
    var = jnp.mean(xc * xc, -1, keepdims=True)
    return (xc * lax.rsqrt(var + LN_EPS) * g.astype(jnp.float32) + b.astype(jnp.float32)).astype(x.dtype)


def axial_rope_tables(T):
    rows = T // GRID_W
    row = jnp.repeat(jnp.arange(rows, dtype=jnp.float32), GRID_W)
    col = jnp.tile(jnp.arange(GRID_W, dtype=jnp.float32), rows)
    inv = ROPE_BASE ** (-jnp.arange(0, ROPE_AXIS, 2, dtype=jnp.float32) / ROPE_AXIS)
    ang = jnp.stack([row[:, None] * inv, col[:, None] * inv], axis=1)
    return jnp.cos(ang), jnp.sin(ang)


def rope_2d(x, cos, sin):
    xr = x.reshape(x.shape[:-1] + (2, 2, ROPE_AXIS // 2))
    x1, x2 = xr[..., 0, :], xr[..., 1, :]
    c = cos[None, :, None, None]
    s = sin[None, :, None, None]
    return jnp.stack([x1 * c - x2 * s, x1 * s + x2 * c], axis=-2).reshape(x.shape)


def pool_mixer(u, w_pool, scale):
    B, T, _ = u.shape
    uf = u.astype(jnp.float32)
    csum = jnp.concatenate([jnp.zeros((B, 1, GROUP_W), jnp.float32), jnp.cumsum(uf, axis=1)], axis=1)
    t = jnp.arange(T)
    outs = []
    for g, w in enumerate(POOL_WINDOWS):
        lo = jnp.clip(t - w // 2, 0, T)
        hi = jnp.clip(t - w // 2 + w, 0, T)
        sl = slice(g * POOL_GW, (g + 1) * POOL_GW)
        cg = csum[..., sl]
        mean = (cg[:, hi] - cg[:, lo]) / (hi - lo).astype(jnp.float32)[None, :, None]
        outs.append((mean - uf[..., sl]) @ w_pool[g].astype(jnp.float32))
    return (jnp.concatenate(outs, axis=-1) * scale.astype(jnp.float32)).astype(u.dtype)


def diff_attention(q, k, v, lam):
    B, T = q.shape[:2]
    nb = T // ATTN_BLOCK
    qb = jnp.moveaxis(q.reshape((B, nb, ATTN_BLOCK) + q.shape[2:]), 1, 0)
    scale = DIFF_QK ** -0.5

    def one_block(qblk):
        s = jnp.einsum('bqhcd,bkhcd->bchqk', qblk, k) * scale
        p = jax.nn.softmax(s, axis=-1)
        return jnp.einsum('bhqk,bkhe->bqhe', p[:, 0] - lam * p[:, 1], v)

    o = lax.map(one_block, qb)
    return jnp.moveaxis(o, 0, 1).reshape(B, T, DIFF_HEADS, DIFF_V)


def diff_mixer(z, zc, cos, sin, lam_vec, subln_g, lam_init, need_ctx):
    def qkv(u):
        uf = u.astype(jnp.float32)
        lead = u.shape[:2]
        q = uf[..., Q_OFF:K_OFF].reshape(lead + (DIFF_HEADS, 2, DIFF_QK))
        k = uf[..., K_OFF:V_OFF].reshape(lead + (DIFF_HEADS, 2, DIFF_QK))
        v = uf[..., V_OFF:CONV_OFF].reshape(lead + (DIFF_HEADS, DIFF_V))
        return q, k, v
    q, k, v = qkv(z)
    qc, kc, vc = qkv(zc)
    q = rope_2d(q, cos, sin)
    k = rope_2d(k, cos, sin)
    lv = lam_vec.astype(jnp.float32)
    lam = jnp.exp(jnp.sum(lv[0] * lv[1])) - jnp.exp(jnp.sum(lv[2] * lv[3])) + lam_init
    g = subln_g.astype(jnp.float32).reshape(DIFF_HEADS, DIFF_V) * (1.0 - lam_init)

    def subln(o):
        o = o * lax.rsqrt(jnp.mean(o * o, -1, keepdims=True) + LN_EPS) * g
        return o.reshape(o.shape[:2] + (GROUP_W,)).astype(z.dtype)

    o = subln(diff_attention(q, jnp.concatenate([k, kc], axis=1), jnp.concatenate([v, vc], axis=1), lam))
    oc = subln(diff_attention(qc, kc, vc, lam)) if need_ctx else None
    return o, oc


def conv_module(u, dw, db, ln_g, ln_b, pw):
    a = u[..., :GROUP_W] * jax.nn.sigmoid(u[..., GROUP_W:])
    y = lax.conv_general_dilated(a, dw[:, None, :].astype(a.dtype), (1,),
                                 [(CONV_WIDTH // 2, CONV_WIDTH // 2)],
                                 dimension_numbers=('NWC', 'WIO', 'NWC'),
                                 feature_group_count=GROUP_W) + db
    y = layer_norm(y, ln_g, ln_b)
    return (jax.nn.silu(y) @ pw).astype(u.dtype)


def _lin_combine(e1, e2):
    a1, b1 = e1
    a2, b2 = e2
    return a2 * a1, a2 * b1 + b2


def s5_scan(abar, bu, h0, reverse):
    if h0 is not None:
        edge = -1 if reverse else 0
        bu = bu.at[:, edge].add(abar * h0)
    _, h = lax.associative_scan(_lin_combine, (jnp.broadcast_to(abar, bu.shape), bu), reverse=reverse, axis=1)
    return h


def s5_mixer(u, uc, a_re, a_im, log_dt, b_re, b_im, c_re, c_im, d_skip, glu_w, glu_b, need_ctx):
    B, T, _ = u.shape
    Tc = uc.shape[1]
    uf = u.astype(jnp.float32)
    ucf = uc.astype(jnp.float32)
    ug = uf.reshape(B, T, S5_GROUPS, S5_P).astype(jnp.complex64)
    ucg = ucf.reshape(B, Tc, S5_GROUPS, S5_P).astype(jnp.complex64)
    ys, ycs = [], []
    for d, reverse in enumerate((False, True)):
        A = lax.complex(a_re[d].astype(jnp.float32), a_im[d].astype(jnp.float32))
        dt = jnp.exp(log_dt[d].astype(jnp.float32))[:, None]
        abar = jnp.exp(dt * A)
        bbar = ((abar - 1.0) / A)[..., None] * lax.complex(b_re[d].astype(jnp.float32), b_im[d].astype(jnp.float32))
        cmat = lax.complex(c_re[d].astype(jnp.float32), c_im[d].astype(jnp.float32))
        hc = s5_scan(abar, jnp.einsum('btgp,gnp->btgn', ucg, bbar), None, reverse)
        h_ctx = hc[:, 0] if reverse else hc[:, -1]
        h = s5_scan(abar, jnp.einsum('btgp,gnp->btgn', ug, bbar), h_ctx, reverse)
        ys.append(jnp.einsum('btgn,gpn->btgp', h, cmat).real)
        if need_ctx:
            ycs.append(jnp.einsum('btgn,gpn->btgp', hc, cmat).real)

    def glu_out(y, uu):
        y = y.reshape(uu.shape) + d_skip.astype(jnp.float32) * uu
        zz = jax.nn.gelu(y)
        return (zz * jax.nn.sigmoid(zz @ glu_w.astype(jnp.float32) + glu_b.astype(jnp.float32))).astype(u.dtype)

    out = glu_out(ys[0] + ys[1], uf)
    out_c = glu_out(ycs[0] + ycs[1], ucf) if need_ctx else None
    return out, out_c


def swiglu(h, w1, w3, w2):
    return (jax.nn.silu(h @ w1) * (h @ w3)) @ w2


def moe_ffn(h, router_w, w1, w3, w2):
    logits = (h @ router_w).astype(jnp.float32)
    top_v, top_i = lax.top_k(logits, TOP_K)
    gates = jax.nn.softmax(top_v, axis=-1)
    combine = jnp.sum(jax.nn.one_hot(top_i, N_EXPERTS, dtype=jnp.float32) * gates[..., None], axis=-2).astype(h.dtype)
    out = jnp.zeros_like(h)
    for e in range(N_EXPERTS):
        out = out + combine[..., e:e + 1] * swiglu(h, w1[e], w3[e], w2[e])
    return out


def setup_inputs(seed: int = 0) -> dict:
    key = jax.random.key(seed)
    keys = iter(jax.random.split(key, 48))
    f32 = jnp.float32

    def nrm(shape, std):
        return std * jax.random.normal(next(keys), shape, f32)

    L, ND, NM = DEPTH, (DEPTH + 1) // 2, DEPTH // 2
    D, G, N, P = D_MODEL, S5_GROUPS, S5_N, S5_P
    log_dt = math.log(1e-3) + (math.log(1e-1) - math.log(1e-3)) * jax.random.uniform(next(keys), (L, 2, G), f32)
    a_im = math.pi * jnp.arange(N, dtype=f32) + nrm((L, 2, G, N), 0.01)
    return {
        'x': nrm((BATCH, SEQ, D), 1.0),
        'c': nrm((BATCH, D), 1.0),
        'ctx': nrm((BATCH, CTX_LEN, D), 1.0),
        'c_ctx': nrm((D,), 1.0),
        'w_mod': nrm((L, D, 6 * D), 0.5 * D ** -0.5),
        'b_mod': nrm((L, 6 * D), 0.02),
        'w_in': nrm((L, D, IN_W), D ** -0.5),
        'w_out': nrm((L, MIX_W, D), BETA * MIX_W ** -0.5),
        'ln1_g': 1.0 + nrm((L, D), 0.02),
        'ln1_b': nrm((L, D), 0.02),
        'ln2_g': 1.0 + nrm((L, D), 0.02),
        'ln2_b': nrm((L, D), 0.02),
        'pool_w': nrm((L, len(POOL_WINDOWS), POOL_GW, POOL_GW), POOL_GW ** -0.5),
        'pool_scale': 1.0 + nrm((L, GROUP_W), 0.02),
        'diff_lambda': nrm((L, 4, DIFF_QK), 0.1),
        'diff_subln_g': 1.0 + nrm((L, GROUP_W), 0.02),
        'conv_dw': nrm((L, CONV_WIDTH, GROUP_W), CONV_WIDTH ** -0.5),
        'conv_db': nrm((L, GROUP_W), 0.02),
        'conv_ln_g': 1.0 + nrm((L, GROUP_W), 0.02),
        'conv_ln_b': nrm((L, GROUP_W), 0.02),
        'conv_pw': nrm((L, GROUP_W, GROUP_W), GROUP_W ** -0.5),
        's5_a_re': -0.5 + nrm((L, 2, G, N), 0.01),
        's5_a_im': a_im,
        's5_log_dt': log_dt,
        's5_b_re': nrm((L, 2, G, N, P), (2.0 * P) ** -0.5),
        's5_b_im': nrm((L, 2, G, N, P), (2.0 * P) ** -0.5),
        's5_c_re': nrm((L, 2, G, P, N), (2.0 * N) ** -0.5),
        's5_c_im': nrm((L, 2, G, P, N), (2.0 * N) ** -0.5),
        's5_d': nrm((L, GROUP_W), 1.0),
        's5_glu_w': nrm((L, GROUP_W, GROUP_W), GROUP_W ** -0.5),
        's5_glu_b': nrm((L, GROUP_W), 0.02),
        'ffn_w1': nrm((ND, D, D_FF), D ** -0.5),
        'ffn_w3': nrm((ND, D, D_FF), D ** -0.5),
        'ffn_w2': nrm((ND, D_FF, D), BETA * D_FF ** -0.5),
        'router_w': nrm((NM, D, N_EXPERTS), D ** -0.5),
        'moe_w1': nrm((NM, N_EXPERTS, D, D_FF), D ** -0.5),
        'moe_w3': nrm((NM, N_EXPERTS, D, D_FF), D ** -0.5),
        'moe_w2': nrm((NM, N_EXPERTS, D_FF, D), BETA * D_FF ** -0.5),
    }


def reference(x, c, ctx, c_ctx, w_mod, b_mod, w_in, w_out, ln1_g, ln1_b, ln2_g, ln2_b,
              pool_w, pool_scale, diff_lambda, diff_subln_g, conv_dw, conv_db, conv_ln_g, conv_ln_b, conv_pw,
              s5_a_re, s5_a_im, s5_log_dt, s5_b_re, s5_b_im, s5_c_re, s5_c_im, s5_d, s5_glu_w, s5_glu_b,
              ffn_w1, ffn_w3, ffn_w2, router_w, moe_w1, moe_w3, moe_w2):
    T = x.shape[1]
    cos, sin = axial_rope_tables(T)
    xc = ctx
    for l in range(DEPTH):
        need_ctx = l < DEPTH - 1
        lam_init = 0.8 - 0.6 * math.exp(-0.3 * l)
        mod = jax.nn.silu(c) @ w_mod[l] + b_mod[l]
        sh1, sc1, g1, sh2, sc2, g2 = [m[:, None, :] for m in jnp.split(mod, 6, axis=-1)]
        sh1c, sc1c, g1c, sh2c, sc2c, g2c = jnp.split(jax.nn.silu(c_ctx) @ w_mod[l] + b_mod[l], 6, axis=-1)

        z = (x * (1.0 + sc1) + sh1) @ w_in[l]
        zc = (xc * (1.0 + sc1c) + sh1c) @ w_in[l]
        pa = pool_mixer(z[..., POOL_OFF:Q_OFF], pool_w[l], pool_scale[l])
        pb, pbc = diff_mixer(z, zc, cos, sin, diff_lambda[l], diff_subln_g[l], lam_init, need_ctx)
        pcv = conv_module(z[..., CONV_OFF:S5_OFF], conv_dw[l], conv_db[l], conv_ln_g[l], conv_ln_b[l], conv_pw[l])
        pd, pdc = s5_mixer(z[..., S5_OFF:IN_W], zc[..., S5_OFF:IN_W], s5_a_re[l], s5_a_im[l], s5_log_dt[l],
                           s5_b_re[l], s5_b_im[l], s5_c_re[l], s5_c_im[l], s5_d[l], s5_glu_w[l], s5_glu_b[l], need_ctx)
        mix = jnp.concatenate([pa, pb, pcv, pd], axis=-1) @ w_out[l]
        x = layer_norm(ALPHA * x + g1 * mix, ln1_g[l], ln1_b[l])
        if need_ctx:
            pac = pool_mixer(zc[..., POOL_OFF:Q_OFF], pool_w[l], pool_scale[l])
            pcvc = conv_module(zc[..., CONV_OFF:S5_OFF], conv_dw[l], conv_db[l], conv_ln_g[l], conv_ln_b[l], conv_pw[l])
            mixc = jnp.concatenate([pac, pbc, pcvc, pdc], axis=-1) @ w_out[l]
            xc = layer_norm(ALPHA * xc + g1c * mixc, ln1_g[l], ln1_b[l])

        h = x * (1.0 + sc2) + sh2
        if need_ctx:
            h = jnp.concatenate([h, xc * (1.0 + sc2c) + sh2c], axis=1)
        if l % 2 == 0:
            f = swiglu(h, ffn_w1[l // 2], ffn_w3[l // 2], ffn_w2[l // 2])
        else:
            f = moe_ffn(h, router_w[l // 2], moe_w1[l // 2], moe_w3[l // 2], moe_w2[l // 2])
        if need_ctx:
            xc = layer_norm(ALPHA * xc + g2c * f[:, T:], ln2_g[l], ln2_b[l])
        x = layer_norm(ALPHA * x + g2 * f[:, :T], ln2_g[l], ln2_b[l])
    return x
```

```python
import functools
import math

import numpy as np
import jax
import jax.numpy as jnp
from jax import lax
from jax.experimental import pallas as pl
from jax.experimental.pallas import tpu as pltpu

F32 = jnp.float32
BF16 = jnp.bfloat16

GRID_W = 64
N_GROUPS = 4
POOL_WINDOWS = (2, 4, 8, 16)
DIFF_HEADS = 8
DIFF_QK = 32
CONV_WIDTH = 31
S5_P = 16
S5_N = 64
TOP_K = 2
ROPE_BASE = 10000.0
LN_EPS = 1e-5

LANES = 128
SUBLANES = 8
ROW_BLOCK = 256
VMEM_LIMIT = 56 * 1024 * 1024


def _cparams(n_axes, vmem=VMEM_LIMIT):
    return pltpu.CompilerParams(dimension_semantics=("arbitrary",) * n_axes, vmem_limit_bytes=vmem)


def _layer_norm(y, g, b):
    mu = jnp.mean(y, -1, keepdims=True)
    yc = y - mu
    var = jnp.mean(yc * yc, -1, keepdims=True)
    return yc * lax.rsqrt(var + LN_EPS) * g + b


def _silu(x):
    return x * jax.nn.sigmoid(x)


def _mod_kernel(cc_ref, w_ref, b_ref, o_ref):
    a = _silu(cc_ref[...])
    o_ref[...] = jnp.dot(a.astype(BF16), w_ref[...].astype(BF16), preferred_element_type=F32) + b_ref[...]


def _modulation(cc, w_mod, b_mod):
    L, D, N = w_mod.shape
    tn = 1536
    assert N % tn == 0
    return pl.pallas_call(
        _mod_kernel,
        grid=(L, N // tn),
        in_specs=[pl.BlockSpec((SUBLANES, D), lambda l, j: (0, 0)),
                  pl.BlockSpec((None, D, tn), lambda l, j: (l, 0, j)),
                  pl.BlockSpec((None, 1, tn), lambda l, j: (l, 0, j))],
        out_specs=pl.BlockSpec((None, SUBLANES, tn), lambda l, j: (l, 0, j)),
        out_shape=jax.ShapeDtypeStruct((L, SUBLANES, N), F32),
        compiler_params=_cparams(2),
    )(cc, w_mod, b_mod.reshape(L, 1, N))


def _mod_spec(layer, chunk, D, n_grid_axes):
    if n_grid_axes == 1:
        return pl.BlockSpec((None, SUBLANES, D), lambda i: (layer, 0, chunk))
    return pl.BlockSpec((None, SUBLANES, D), lambda j, i: (layer, 0, chunk))


def _pick(m, is_ctx):
    return jnp.where(is_ctx, m[1:2, :], m[0:1, :])


def _inproj_kernel(x_ref, sh_ref, sc_ref, w_ref, o_ref, wb_ref, *, n_ctx_blocks):
    i = pl.program_id(1)

    @pl.when(i == 0)
    def _():
        wb_ref[...] = w_ref[...].astype(BF16)

    is_ctx = i < n_ctx_blocks
    h = x_ref[...] * (1.0 + _pick(sc_ref[...], is_ctx)) + _pick(sh_ref[...], is_ctx)
    o_ref[...] = jnp.dot(h.astype(BF16), wb_ref[...], preferred_element_type=F32)


def _in_projection(xs, mod, w_in, layer, n_ctx_blocks):
    S, D = xs.shape
    N = w_in.shape[-1]
    tn = 896
    assert N % tn == 0 and S % ROW_BLOCK == 0
    return pl.pallas_call(
        functools.partial(_inproj_kernel, n_ctx_blocks=n_ctx_blocks),
        grid=(N // tn, S // ROW_BLOCK),
        in_specs=[pl.BlockSpec((ROW_BLOCK, D), lambda j, i: (i, 0)),
                  _mod_spec(layer, 0, D, 2),
                  _mod_spec(layer, 1, D, 2),
                  pl.BlockSpec((None, D, tn), lambda j, i: (layer, 0, j))],
        out_specs=pl.BlockSpec((ROW_BLOCK, tn), lambda j, i: (i, j)),
        out_shape=jax.ShapeDtypeStruct((S, N), F32),
        scratch_shapes=[pltpu.VMEM((D, tn), BF16)],
        compiler_params=_cparams(2),
    )(xs, mod, mod, w_in)


def _seq_edges(i, n_blocks, n_ctx_blocks):
    prev_ok = jnp.logical_and(i != 0, i != n_ctx_blocks)
    next_ok = jnp.logical_and(i != n_ctx_blocks - 1, i != n_blocks - 1)
    return prev_ok, next_ok


def _pool_kernel(p_ref, c_ref, n_ref, w_ref, scale_ref, o_ref, ext_ref, *, n_blocks, n_ctx_blocks):
    i = pl.program_id(0)
    R = ROW_BLOCK
    halo = SUBLANES
    prev_ok, next_ok = _seq_edges(i, n_blocks, n_ctx_blocks)
    cur = c_ref[...]
    ext_ref[0:halo, :] = jnp.where(prev_ok, p_ref[R - halo:R, :], 0.0)
    ext_ref[halo:halo + R, :] = cur
    ext_ref[halo + R:halo + R + halo, :] = jnp.where(next_ok, n_ref[0:halo, :], 0.0)
    rloc = lax.broadcasted_iota(jnp.int32, (R, 1), 0)
    gw = cur.shape[1] // len(POOL_WINDOWS)
    outs = []
    for g, w in enumerate(POOL_WINDOWS):
        acc = jnp.zeros((R, gw), F32)
        cnt = jnp.zeros((R, 1), F32)
        for d in range(-(w // 2), w - w // 2):
            acc = acc + ext_ref[halo + d:halo + d + R, g * gw:(g + 1) * gw]
            valid = jnp.logical_and(jnp.logical_or(rloc + d >= 0, prev_ok),
                                    jnp.logical_or(rloc + d < R, next_ok))
            cnt = cnt + valid.astype(F32)
        diff = acc / cnt - cur[:, g * gw:(g + 1) * gw]
        outs.append(jnp.dot(diff.astype(BF16), w_ref[g].astype(BF16), preferred_element_type=F32))
    o_ref[...] = jnp.concatenate(outs, axis=-1) * scale_ref[...]


def _pool_mixer(z, pool_w, pool_scale, layer, n_ctx_blocks):
    S = z.shape[0]
    nb = S // ROW_BLOCK
    GW = pool_scale.shape[-1]
    G, gw = pool_w.shape[1], pool_w.shape[2]
    return pl.pallas_call(
        functools.partial(_pool_kernel, n_blocks=nb, n_ctx_blocks=n_ctx_blocks),
        grid=(nb,),
        in_specs=[pl.BlockSpec((ROW_BLOCK, GW), lambda i: (jnp.maximum(i - 1, 0), 0)),
                  pl.BlockSpec((ROW_BLOCK, GW), lambda i: (i, 0)),
                  pl.BlockSpec((ROW_BLOCK, GW), lambda i: (jnp.minimum(i + 1, nb - 1), 0)),
                  pl.BlockSpec((None, G, gw, gw), lambda i: (layer, 0, 0, 0)),
                  pl.BlockSpec((None, 1, GW), lambda i: (layer, 0, 0))],
        out_specs=pl.BlockSpec((ROW_BLOCK, GW), lambda i: (i, 0)),
        out_shape=jax.ShapeDtypeStruct((S, GW), F32),
        scratch_shapes=[pltpu.VMEM((ROW_BLOCK + 2 * SUBLANES, GW), F32)],
        compiler_params=_cparams(1),
    )(z, z, z, pool_w, pool_scale.reshape(pool_scale.shape[0], 1, GW))


CONV_HALO = 16


def _conv_kernel(p_ref, c_ref, n_ref, dw_ref, db_ref, g_ref, b_ref, pw_ref, o_ref, ext_ref,
                 *, n_blocks, n_ctx_blocks):
    i = pl.program_id(0)
    R = ROW_BLOCK
    H = CONV_HALO
    GW = o_ref.shape[1]
    prev_ok, next_ok = _seq_edges(i, n_blocks, n_ctx_blocks)

    def glu(u):
        return u[:, :GW] * jax.nn.sigmoid(u[:, GW:])

    ext_ref[0:H, :] = jnp.where(prev_ok, glu(p_ref[R - H:R, :]), 0.0)
    ext_ref[H:H + R, :] = glu(c_ref[...])
    ext_ref[H + R:H + R + H, :] = jnp.where(next_ok, glu(n_ref[0:H, :]), 0.0)
    off = H - CONV_WIDTH // 2
    acc = jnp.zeros((R, GW), F32)
    for j in range(CONV_WIDTH):
        acc = acc + ext_ref[off + j:off + j + R, :] * dw_ref[j:j + 1, :]
    y = _layer_norm(acc + db_ref[...], g_ref[...], b_ref[...])
    o_ref[...] = jnp.dot(_silu(y).astype(BF16), pw_ref[...].astype(BF16), preferred_element_type=F32)


def _conv_mixer(z, col_block, conv_dw, conv_db, conv_ln_g, conv_ln_b, conv_pw, layer, n_ctx_blocks):
    S = z.shape[0]
    nb = S // ROW_BLOCK
    GW = conv_db.shape[-1]
    L = conv_db.shape[0]
    vec = lambda a: a.reshape(L, 1, GW)
    vspec = pl.BlockSpec((None, 1, GW), lambda i: (layer, 0, 0))
    return pl.pallas_call(
        functools.partial(_conv_kernel, n_blocks=nb, n_ctx_blocks=n_ctx_blocks),
        grid=(nb,),
        in_specs=[pl.BlockSpec((ROW_BLOCK, 2 * GW), lambda i: (jnp.maximum(i - 1, 0), col_block)),
                  pl.BlockSpec((ROW_BLOCK, 2 * GW), lambda i: (i, col_block)),
                  pl.BlockSpec((ROW_BLOCK, 2 * GW), lambda i: (jnp.minimum(i + 1, nb - 1), col_block)),
                  pl.BlockSpec((None, CONV_WIDTH, GW), lambda i: (layer, 0, 0)),
                  vspec, vspec, vspec,
                  pl.BlockSpec((None, GW, GW), lambda i: (layer, 0, 0))],
        out_specs=pl.BlockSpec((ROW_BLOCK, GW), lambda i: (i, 0)),
        out_shape=jax.ShapeDtypeStruct((S, GW), F32),
        scratch_shapes=[pltpu.VMEM((ROW_BLOCK + 2 * CONV_HALO, GW), F32)],
        compiler_params=_cparams(1),
    )(z, z, z, conv_dw, vec(conv_db), vec(conv_ln_g), vec(conv_ln_b), conv_pw)


def _rope_tables(T, Tc, width):
    ax = DIFF_QK // 2
    inv = ROPE_BASE ** (-jnp.arange(0, ax, 2, dtype=F32) / ax)
    t = jnp.arange(T)
    row = (t // GRID_W).astype(F32)
    col = (t % GRID_W).astype(F32)
    ang = jnp.stack([row[:, None] * inv, col[:, None] * inv], axis=1)
    cos = jnp.cos(ang)[:, :, None, :]
    sin = jnp.sin(ang)[:, :, None, :]
    cos = jnp.broadcast_to(cos, (T, 2, 2, ax // 2)).reshape(T, DIFF_QK)
    sin = jnp.concatenate([-sin, sin], axis=2).reshape(T, DIFF_QK)
    reps = width // DIFF_QK
    cos = jnp.concatenate([jnp.ones((Tc, DIFF_QK), F32), cos], axis=0)
    sin = jnp.concatenate([jnp.zeros((Tc, DIFF_QK), F32), sin], axis=0)
    return jnp.tile(cos, (1, reps)), jnp.tile(sin, (1, reps))


def _qkv_prep_kernel(q_ref, k_ref, v_ref, cos_ref, sin_ref, qo_ref, kto_ref, vo_ref):
    W = q_ref.shape[1]
    half = DIFF_QK // 4
    lane = lax.broadcasted_iota(jnp.int32, (1, W), 1)
    first = (lane % (2 * half)) < half
    cos = cos_ref[...]
    sin = sin_ref[...]

    def rope(x):
        partner = jnp.where(first, pltpu.roll(x, W - half, 1), pltpu.roll(x, half, 1))
        return x * cos + partner * sin

    qo_ref[...] = (rope(q_ref[...]) * (DIFF_QK ** -0.5)).astype(BF16)
    kto_ref[...] = rope(k_ref[...]).T.astype(BF16)
    vo_ref[...] = v_ref[...].astype(BF16)


def _qkv_prep(z, cos, sin, q_blk, k_blk, v_blk):
    S = z.shape[0]
    W = cos.shape[1]
    nb = S // ROW_BLOCK
    row = lambda c: pl.BlockSpec((ROW_BLOCK, W), lambda i: (i, c))
    return pl.pallas_call(
        _qkv_prep_kernel,
        grid=(nb,),
        in_specs=[row(q_blk), row(k_blk), row(v_blk), row(0), row(0)],
        out_specs=[row(0), pl.BlockSpec((W, ROW_BLOCK), lambda i: (0, i)), row(0)],
        out_shape=[jax.ShapeDtypeStruct((S, W), BF16), jax.ShapeDtypeStruct((W, S), BF16),
                   jax.ShapeDtypeStruct((S, W), BF16)],
        compiler_params=_cparams(1),
    )(z, z, z, cos, sin)


ATTN_TQ = 256
ATTN_TK = 1024


def _attn_kernel(q_ref, kt_ref, v_ref, lam_ref, g_ref, o_ref, m_ref, l_ref, acc_ref,
                 *, n_ctx, n_ctx_blocks, n_lat_chunks):
    i = pl.program_id(1)
    tq = ATTN_TQ
    dv = LANES // 2
    q = q_ref[...]
    lane = lax.broadcasted_iota(jnp.int32, (1, LANES), 1)
    lam = lam_ref[...]
    zero = jnp.zeros_like(q)

    def attend(start, size, first):
        s = jnp.dot(qq, kt_ref[:, pl.ds(start, size)], preferred_element_type=F32)
        mx = jnp.max(s, axis=-1, keepdims=True)
        vv = v_ref[pl.ds(start, size), :]
        if first:
            p = jnp.exp(s - mx)
            m_ref[...] = mx
            l_ref[...] = jnp.sum(p, axis=-1, keepdims=True)
            acc_ref[...] = jnp.dot(p.astype(BF16), vv, preferred_element_type=F32)
        else:
            m_old = m_ref[...]
            m_new = jnp.maximum(m_old, mx)
            alpha = jnp.exp(m_old - m_new)
            p = jnp.exp(s - m_new)
            m_ref[...] = m_new
            l_ref[...] = alpha * l_ref[...] + jnp.sum(p, axis=-1, keepdims=True)
            acc_ref[...] = alpha * acc_ref[...] + jnp.dot(p.astype(BF16), vv, preferred_element_type=F32)

    res = jnp.zeros((tq, LANES), F32)
    for hh in range(2):
        base = hh * dv
        in0 = jnp.logical_and(lane >= base, lane < base + DIFF_QK)
        in1 = jnp.logical_and(lane >= base + DIFF_QK, lane < base + 2 * DIFF_QK)
        qq = jnp.concatenate([jnp.where(in0, q, zero), jnp.where(in1, q, zero)], axis=0)
        attend(0, n_ctx, True)
        n_steps = jnp.where(i < n_ctx_blocks, 0, n_lat_chunks)

        def body(c, carry):
            attend(pl.multiple_of(n_ctx + c * ATTN_TK, LANES), ATTN_TK, False)
            return carry

        lax.fori_loop(0, n_steps, body, 0)
        acc = acc_ref[...]
        l = l_ref[...]
        o = acc[:tq] / l[:tq] - lam * (acc[tq:] / l[tq:])
        res = jnp.where(jnp.logical_and(lane >= base, lane < base + dv), o, res)

    sq = res * res
    lo = lane < dv
    ss0 = jnp.sum(jnp.where(lo, sq, 0.0), axis=-1, keepdims=True)
    ss1 = jnp.sum(jnp.where(lo, 0.0, sq), axis=-1, keepdims=True)
    r = jnp.where(lo, lax.rsqrt(ss0 / dv + LN_EPS), lax.rsqrt(ss1 / dv + LN_EPS))
    o_ref[...] = res * r * g_ref[...]


def _diff_attention(q, kt, v, lam, g, n_ctx):
    S, W = q.shape
    assert S % ATTN_TQ == 0 and n_ctx % ATTN_TQ == 0 and (S - n_ctx) % ATTN_TK == 0
    nq = S // ATTN_TQ
    return pl.pallas_call(
        functools.partial(_attn_kernel, n_ctx=n_ctx, n_ctx_blocks=n_ctx // ATTN_TQ,
                          n_lat_chunks=(S - n_ctx) // ATTN_TK),
        grid=(W // LANES, nq),
        in_specs=[pl.BlockSpec((ATTN_TQ, LANES), lambda p, i: (i, p)),
                  pl.BlockSpec((LANES, S), lambda p, i: (p, 0)),
                  pl.BlockSpec((S, LANES), lambda p, i: (0, p)),
                  pl.BlockSpec((1, LANES), lambda p, i: (0, 0)),
                  pl.BlockSpec((1, LANES), lambda p, i: (0, p))],
        out_specs=pl.BlockSpec((ATTN_TQ, LANES), lambda p, i: (i, p)),
        out_shape=jax.ShapeDtypeStruct((S, W), F32),
        scratch_shapes=[pltpu.VMEM((2 * ATTN_TQ, 1), F32), pltpu.VMEM((2 * ATTN_TQ, 1), F32),
                        pltpu.VMEM((2 * ATTN_TQ, LANES), F32)],
        compiler_params=_cparams(2),
    )(q, kt, v, lam, g)


def _s5_params(a_re, a_im, log_dt, b_re, b_im, c_re, c_im):
    G, N = a_re.shape
    P = b_re.shape[-1]
    A = lax.complex(a_re.astype(F32), a_im.astype(F32))
    dt = jnp.exp(log_dt.astype(F32))[:, None]
    abar = jnp.exp(dt * A)
    bbar = ((abar - 1.0) / A)[..., None] * lax.complex(b_re.astype(F32), b_im.astype(F32))
    eye = jnp.eye(G, dtype=F32)
    wb = lambda m: jnp.einsum('gh,gnp->gphn', eye, m).reshape(G * P, G * N)
    w_in = jnp.concatenate([wb(jnp.real(bbar)), wb(jnp.imag(bbar))], axis=1)
    cm = lambda m: jnp.einsum('gh,gpn->hngp', eye, m.astype(F32)).reshape(G * N, G * P)
    w_out = jnp.concatenate([cm(c_re), -cm(c_im)], axis=0)
    coef = jnp.concatenate([jnp.real(abar).reshape(1, G * N), jnp.imag(abar).reshape(1, G * N)], axis=1)
    return w_in.astype(BF16), coef, w_out.astype(BF16)


def _s5_kernel(uf_ref, ur_ref, wbf_ref, wbr_ref, af_ref, ar_ref, cf_ref, cr_ref, yf_ref, yr_ref,
               bf_ref, br_ref, hf_ref, hr_ref):
    i = pl.program_id(0)
    R = ROW_BLOCK
    NS = af_ref.shape[1] // 2

    @pl.when(i == 0)
    def _():
        hf_ref[...] = jnp.zeros_like(hf_ref)
        hr_ref[...] = jnp.zeros_like(hr_ref)

    bf_ref[...] = jnp.dot(uf_ref[...].astype(BF16), wbf_ref[...], preferred_element_type=F32)
    br_ref[...] = jnp.dot(ur_ref[...].astype(BF16), wbr_ref[...], preferred_element_type=F32)

    def scan(buf_ref, a_ref, h_ref, reverse):
        are = a_ref[:, :NS]
        aim = a_ref[:, NS:]

        def step(t, carry):
            hre, him = carry
            row = (R - 1 - t) if reverse else t
            bu = buf_ref[pl.ds(row, 1), :]
            nre = are * hre - aim * him + bu[:, :NS]
            nim = are * him + aim * hre + bu[:, NS:]
            buf_ref[pl.ds(row, 1), :] = jnp.concatenate([nre, nim], axis=1)
            return nre, nim

        hre, him = lax.fori_loop(0, R, step, (h_ref[:, :NS], h_ref[:, NS:]))
        h_ref[...] = jnp.concatenate([hre, him], axis=1)

    scan(bf_ref, af_ref, hf_ref, False)
    scan(br_ref, ar_ref, hr_ref, True)
    yf_ref[...] = jnp.dot(bf_ref[...].astype(BF16), cf_ref[...], preferred_element_type=F32)
    yr_ref[...] = jnp.dot(br_ref[...].astype(BF16), cr_ref[...], preferred_element_type=F32)


def _s5_scan(z, col_block, pf, pr, n_ctx_blocks):
    S = z.shape[0]
    nb = S // ROW_BLOCK
    GW = pf[0].shape[0]
    NS2 = pf[0].shape[1]

    def rev_block(i):
        return jnp.where(i < n_ctx_blocks, n_ctx_blocks - 1 - i, nb - 1 - i + n_ctx_blocks)

    full = lambda a: pl.BlockSpec(a.shape, lambda i: (0, 0))
    return pl.pallas_call(
        _s5_kernel,
        grid=(nb,),
        in_specs=[pl.BlockSpec((ROW_BLOCK, GW), lambda i: (i, col_block)),
                  pl.BlockSpec((ROW_BLOCK, GW), lambda i: (rev_block(i), col_block)),
                  full(pf[0]), full(pr[0]), full(pf[1]), full(pr[1]), full(pf[2]), full(pr[2])],
        out_specs=[pl.BlockSpec((ROW_BLOCK, GW), lambda i: (i, 0)),
                   pl.BlockSpec((ROW_BLOCK, GW), lambda i: (rev_block(i), 0))],
        out_shape=[jax.ShapeDtypeStruct((S, GW), F32), jax.ShapeDtypeStruct((S, GW), F32)],
        scratch_shapes=[pltpu.VMEM((ROW_BLOCK, NS2), F32), pltpu.VMEM((ROW_BLOCK, NS2), F32),
                        pltpu.VMEM((1, NS2), F32), pltpu.VMEM((1, NS2), F32)],
        compiler_params=_cparams(1),
    )(z, z, pf[0], pr[0], pf[1], pr[1], pf[2], pr[2])


def _s5_glu_kernel(yf_ref, yr_ref, u_ref, d_ref, w_ref, b_ref, o_ref):
    y = yf_ref[...] + yr_ref[...] + d_ref[...] * u_ref[...]
    zz = jax.nn.gelu(y)
    gate = jnp.dot(zz.astype(BF16), w_ref[...].astype(BF16), preferred_element_type=F32) + b_ref[...]
    o_ref[...] = zz * jax.nn.sigmoid(gate)


def _s5_glu(yf, yr, z, col_block, s5_d, glu_w, glu_b, layer):
    S, GW = yf.shape
    L = s5_d.shape[0]
    row = lambda c: pl.BlockSpec((ROW_BLOCK, GW), lambda i: (i, c))
    vspec = pl.BlockSpec((None, 1, GW), lambda i: (layer, 0, 0))
    return pl.pallas_call(
        _s5_glu_kernel,
        grid=(S // ROW_BLOCK,),
        in_specs=[row(0), row(0), row(col_block), vspec,
                  pl.BlockSpec((None, GW, GW), lambda i: (layer, 0, 0)), vspec],
        out_specs=row(0),
        out_shape=jax.ShapeDtypeStruct((S, GW), F32),
        compiler_params=_cparams(1),
    )(yf, yr, z, s5_d.reshape(L, 1, GW), glu_w, glu_b.reshape(L, 1, GW))


def _cast_kernel(x_ref, o_ref):
    o_ref[...] = x_ref[...].astype(o_ref.dtype)


def _cast_bf16(w, layer):
    _, K, N = w.shape
    tk = 512
    return pl.pallas_call(
        _cast_kernel,
        grid=(K // tk,),
        in_specs=[pl.BlockSpec((None, tk, N), lambda i: (layer, i, 0))],
        out_specs=pl.BlockSpec((tk, N), lambda i: (i, 0)),
        out_shape=jax.ShapeDtypeStruct((K, N), BF16),
        compiler_params=_cparams(1),
    )(w)


def _mixout_kernel(*refs, alpha, n_ctx_blocks, route, h_dtype):
    (pa_ref, pb_ref, pc_ref, pd_ref, w_ref, x_ref, g1_ref, lg_ref, lb_ref, sh_ref, sc_ref) = refs[:11]
    if route:
        rw_ref, x1_ref, h_ref, idx_ref, gate_ref = refs[11:]
    else:
        x1_ref, h_ref = refs[11:]
    i = pl.program_id(0)
    is_ctx = i < n_ctx_blocks
    GW = pa_ref.shape[1]
    mix = jnp.zeros(x_ref.shape, F32)
    for k, p_ref in enumerate((pa_ref, pb_ref, pc_ref, pd_ref)):
        mix = mix + jnp.dot(p_ref[...].astype(BF16), w_ref[k * GW:(k + 1) * GW, :],
                            preferred_element_type=F32)
    y = alpha * x_ref[...] + _pick(g1_ref[...], is_ctx) * mix
    x1 = _layer_norm(y, lg_ref[...], lb_ref[...])
    x1_ref[...] = x1
    h = x1 * (1.0 + _pick(sc_ref[...], is_ctx)) + _pick(sh_ref[...], is_ctx)
    h_ref[...] = h.astype(h_dtype)
    if route:
        logits = jnp.dot(h, rw_ref[...], preferred_element_type=F32, precision=lax.Precision.HIGHEST)
        n_exp = rw_ref.shape[1]
        lane = lax.broadcasted_iota(jnp.int32, logits.shape, 1)
        m1 = jnp.max(logits, axis=-1, keepdims=True)
        i1 = jnp.min(jnp.where(logits == m1, lane, n_exp), axis=-1, keepdims=True)
        rest = jnp.where(lane == i1, -jnp.inf, logits)
        m2 = jnp.max(rest, axis=-1, keepdims=True)
        i2 = jnp.min(jnp.where(rest == m2, lane, n_exp), axis=-1, keepdims=True)
        e2 = jnp.exp(m2 - m1)
        idx_ref[...] = jnp.concatenate([i1, i2], axis=1)
        gate_ref[...] = jnp.concatenate([1.0 / (1.0 + e2), e2 / (1.0 + e2)], axis=1)


def _mix_out(parts, w_out_bf, xs, mod, ln_g, ln_b, layer, alpha, n_ctx_blocks, router_w):
    S, D = xs.shape
    GW = parts[0].shape[1]
    L = ln_g.shape[0]
    route = router_w is not None
    h_dtype = F32 if route else BF16
    part = pl.BlockSpec((ROW_BLOCK, GW), lambda i: (i, 0))
    rows = pl.BlockSpec((ROW_BLOCK, D), lambda i: (i, 0))
    vspec = pl.BlockSpec((None, 1, D), lambda i: (layer, 0, 0))
    in_specs = [part, part, part, part,
                pl.BlockSpec((D, D), lambda i: (0, 0)), rows,
                _mod_spec(layer, 2, D, 1), vspec, vspec, _mod_spec(layer, 3, D, 1), _mod_spec(layer, 4, D, 1)]
    args = list(parts) + [w_out_bf, xs, mod, ln_g.reshape(L, 1, D), ln_b.reshape(L, 1, D), mod, mod]
    out_specs = [rows, rows]
    out_shape = [jax.ShapeDtypeStruct((S, D), F32), jax.ShapeDtypeStruct((S, D), h_dtype)]
    if route:
        E = router_w.shape[-1]
        in_specs.append(pl.BlockSpec((D, E), lambda i: (0, 0)))
        args.append(router_w)
        out_specs += [pl.BlockSpec((ROW_BLOCK, TOP_K), lambda i: (i, 0))] * 2
        out_shape += [jax.ShapeDtypeStruct((S, TOP_K), jnp.int32), jax.ShapeDtypeStruct((S, TOP_K), F32)]
    return pl.pallas_call(
        functools.partial(_mixout_kernel, alpha=alpha, n_ctx_blocks=n_ctx_blocks, route=route, h_dtype=h_dtype),
        grid=(S // ROW_BLOCK,),
        in_specs=in_specs,
        out_specs=out_specs,
        out_shape=out_shape,
        compiler_params=_cparams(1),
    )(*args)


FFN_TM = 512
FFN_TF = 512
FFN_TN = 512


def _expert_changed(te_ref, i):
    prev = te_ref[jnp.maximum(i - 1, 0)]
    return jnp.logical_or(i == 0, te_ref[i] != prev)


def _ffn_up_kernel(te_ref, nu_ref, h_ref, w1_ref, w3_ref, o_ref, w1b_ref, w3b_ref):
    i = pl.program_id(1)

    @pl.when(_expert_changed(te_ref, i))
    def _():
        w1b_ref[...] = w1_ref[...].astype(BF16)
        w3b_ref[...] = w3_ref[...].astype(BF16)

    @pl.when(i < nu_ref[0])
    def _():
        h = h_ref[...]
        a = jnp.dot(h, w1b_ref[...], preferred_element_type=F32)
        b = jnp.dot(h, w3b_ref[...], preferred_element_type=F32)
        o_ref[...] = (_silu(a) * b).astype(o_ref.dtype)

    @pl.when(i >= nu_ref[0])
    def _():
        o_ref[...] = jnp.zeros_like(o_ref)


def _ffn_down_kernel(te_ref, nu_ref, g_ref, w2_ref, o_ref, w2b_ref):
    i = pl.program_id(1)

    @pl.when(_expert_changed(te_ref, i))
    def _():
        w2b_ref[...] = w2_ref[...].astype(BF16)

    @pl.when(i < nu_ref[0])
    def _():
        o_ref[...] = jnp.dot(g_ref[...], w2b_ref[...], preferred_element_type=F32)

    @pl.when(i >= nu_ref[0])
    def _():
        o_ref[...] = jnp.zeros_like(o_ref)


def _swiglu_tiles(hs, w1, w3, w2, tile_expert, n_used, tm):
    R, D = hs.shape
    _, _, F = w1.shape
    n_tiles = R // tm
    tf = FFN_TF if F % FFN_TF == 0 else F
    tn = FFN_TN
    assert R % tm == 0 and F % tf == 0 and D % tn == 0
    g = pl.pallas_call(
        _ffn_up_kernel,
        grid_spec=pltpu.PrefetchScalarGridSpec(
            num_scalar_prefetch=2,
            grid=(F // tf, n_tiles),
            in_specs=[pl.BlockSpec((tm, D), lambda j, i, te, nu: (i, 0)),
                      pl.BlockSpec((None, D, tf), lambda j, i, te, nu: (te[i], 0, j)),
                      pl.BlockSpec((None, D, tf), lambda j, i, te, nu: (te[i], 0, j))],
            out_specs=pl.BlockSpec((tm, tf), lambda j, i, te, nu: (i, j)),
            scratch_shapes=[pltpu.VMEM((D, tf), BF16), pltpu.VMEM((D, tf), BF16)]),
        out_shape=jax.ShapeDtypeStruct((R, F), BF16),
        compiler_params=_cparams(2),
    )(tile_expert, n_used, hs, w1, w3)
    return pl.pallas_call(
        _ffn_down_kernel,
        grid_spec=pltpu.PrefetchScalarGridSpec(
            num_scalar_prefetch=2,
            grid=(D // tn, n_tiles),
            in_specs=[pl.BlockSpec((tm, F), lambda j, i, te, nu: (i, 0)),
                      pl.BlockSpec((None, F, tn), lambda j, i, te, nu: (te[i], 0, j))],
            out_specs=pl.BlockSpec((tm, tn), lambda j, i, te, nu: (i, j)),
            scratch_shapes=[pltpu.VMEM((F, tn), BF16)]),
        out_shape=jax.ShapeDtypeStruct((R, D), F32),
        compiler_params=_cparams(2),
    )(tile_expert, n_used, g, w2)


def _row_copy(src_ref, dst_ref, src_row, dst_row, sem):
    return pltpu.make_async_copy(src_ref.at[pl.ds(src_row, 1)], dst_ref.at[pl.ds(dst_row, 1)], sem)


def _gather_kernel(tok_ref, src_ref, o_ref, buf_ref, sem):
    i = pl.program_id(0)
    tm = buf_ref.shape[0]

    def issue(r, c):
        _row_copy(src_ref, buf_ref, tok_ref[i * tm + r], r, sem).start()
        return c

    def drain(r, c):
        _row_copy(src_ref, buf_ref, 0, r, sem).wait()
        return c

    lax.fori_loop(0, tm, issue, 0)
    lax.fori_loop(0, tm, drain, 0)
    o_ref[...] = buf_ref[...].astype(o_ref.dtype)


def _gather_rows(src, tok_of_slot, tm):
    R = tok_of_slot.shape[0]
    D = src.shape[1]
    return pl.pallas_call(
        _gather_kernel,
        grid_spec=pltpu.PrefetchScalarGridSpec(
            num_scalar_prefetch=1,
            grid=(R // tm,),
            in_specs=[pl.BlockSpec(memory_space=pl.ANY)],
            out_specs=pl.BlockSpec((tm, D), lambda i, tok: (i, 0)),
            scratch_shapes=[pltpu.VMEM((tm, D), src.dtype), pltpu.SemaphoreType.DMA(())]),
        out_shape=jax.ShapeDtypeStruct((R, D), BF16),
        compiler_params=_cparams(1),
    )(tok_of_slot, src)


def _ln2_dense_kernel(x_ref, f_ref, g2_ref, lg_ref, lb_ref, o_ref, *, alpha, n_ctx_blocks, row_off):
    is_ctx = (pl.program_id(0) + row_off) < n_ctx_blocks
    y = alpha * x_ref[...] + _pick(g2_ref[...], is_ctx) * f_ref[...]
    o_ref[...] = _layer_norm(y, lg_ref[...], lb_ref[...])


def _ln2_moe_kernel(sa_ref, sb_ref, x_ref, y_ref, gate_ref, g2_ref, lg_ref, lb_ref, o_ref, bufa_ref, bufb_ref,
                    sem, *, alpha, n_ctx_blocks, row_off):
    i = pl.program_id(0)
    R = ROW_BLOCK
    base = (i + row_off) * R

    def issue(r, c):
        _row_copy(y_ref, bufa_ref, sa_ref[base + r], r, sem).start()
        _row_copy(y_ref, bufb_ref, sb_ref[base + r], r, sem).start()
        return c

    def drain(r, c):
        _row_copy(y_ref, bufa_ref, 0, r, sem).wait()
        _row_copy(y_ref, bufb_ref, 0, r, sem).wait()
        return c

    lax.fori_loop(0, R, issue, 0)
    lax.fori_loop(0, R, drain, 0)
    gate = gate_ref[...]
    f = gate[:, 0:1] * bufa_ref[...] + gate[:, 1:2] * bufb_ref[...]
    is_ctx = (i + row_off) < n_ctx_blocks
    y = alpha * x_ref[...] + _pick(g2_ref[...], is_ctx) * f
    o_ref[...] = _layer_norm(y, lg_ref[...], lb_ref[...])


def _ln2(x1, f, mod, ln_g, ln_b, layer, alpha, n_ctx_blocks, row_off, moe=None):
    S, D = x1.shape
    L = ln_g.shape[0]
    nb = S // ROW_BLOCK - row_off
    n_pre = 0 if moe is None else 2
    wrap = (lambda f_: (lambda i, *_: f_(i)))
    rows_in = pl.BlockSpec((ROW_BLOCK, D), wrap(lambda i: (i + row_off, 0)))
    rows_out = pl.BlockSpec((ROW_BLOCK, D), wrap(lambda i: (i, 0)))
    vspec = pl.BlockSpec((None, 1, D), wrap(lambda i: (layer, 0, 0)))
    mspec = pl.BlockSpec((None, SUBLANES, D), wrap(lambda i: (layer, 0, 5)))
    common = dict(alpha=alpha, n_ctx_blocks=n_ctx_blocks, row_off=row_off)
    lg, lb = ln_g.reshape(L, 1, D), ln_b.reshape(L, 1, D)
    out_shape = jax.ShapeDtypeStruct((nb * ROW_BLOCK, D), F32)
    if moe is None:
        return pl.pallas_call(
            functools.partial(_ln2_dense_kernel, **common),
            grid=(nb,),
            in_specs=[rows_in, rows_in, mspec, vspec, vspec],
            out_specs=rows_out,
            out_shape=out_shape,
            compiler_params=_cparams(1),
        )(x1, f, mod, lg, lb)
    slot_a, slot_b, gates = moe
    return pl.pallas_call(
        functools.partial(_ln2_moe_kernel, **common),
        grid_spec=pltpu.PrefetchScalarGridSpec(
            num_scalar_prefetch=n_pre,
            grid=(nb,),
            in_specs=[rows_in, pl.BlockSpec(memory_space=pl.ANY),
                      pl.BlockSpec((ROW_BLOCK, TOP_K), wrap(lambda i: (i + row_off, 0))),
                      mspec, vspec, vspec],
            out_specs=rows_out,
            scratch_shapes=[pltpu.VMEM((ROW_BLOCK, D), F32), pltpu.VMEM((ROW_BLOCK, D), F32),
                            pltpu.SemaphoreType.DMA(())]),
        out_shape=out_shape,
        compiler_params=_cparams(1),
    )(slot_a, slot_b, x1, f, gates, mod, lg, lb)


def _route_slots(idx, row0, n_experts, tm):
    S = idx.shape[0]
    n = S - row0
    e_flat = idx[row0:].reshape(-1)
    onehot = (e_flat[:, None] == jnp.arange(n_experts, dtype=jnp.int32)[None, :]).astype(jnp.int32)
    pos = jnp.sum((jnp.cumsum(onehot, axis=0) - 1) * onehot, axis=1)
    counts = jnp.sum(onehot, axis=0)
    padded = ((counts + tm - 1) // tm) * tm
    ends = jnp.cumsum(padded)
    starts = ends - padded
    slot = starts[e_flat] + pos
    n_tiles = -(-(TOP_K * n) // tm) + n_experts
    tok = jnp.repeat(jnp.arange(n, dtype=jnp.int32) + row0, TOP_K)
    tok_of_slot = jnp.full((n_tiles * tm,), row0, jnp.int32).at[slot].set(tok)
    tile_start = jnp.arange(n_tiles, dtype=jnp.int32) * tm
    tile_expert = jnp.minimum(jnp.searchsorted(ends, tile_start, side='right'), n_experts - 1).astype(jnp.int32)
    n_used = (ends[-1] // tm).astype(jnp.int32).reshape(1)
    slot2 = slot.reshape(n, TOP_K).astype(jnp.int32)
    pad = jnp.zeros((row0,), jnp.int32)
    slot_a = jnp.concatenate([pad, slot2[:, 0]])
    slot_b = jnp.concatenate([pad, slot2[:, 1]])
    return tok_of_slot, tile_expert, n_used, slot_a, slot_b


def kernel(x, c, ctx, c_ctx, w_mod, b_mod, w_in, w_out, ln1_g, ln1_b, ln2_g, ln2_b, pool_w, pool_scale,
           diff_lambda, diff_subln_g, conv_dw, conv_db, conv_ln_g, conv_ln_b, conv_pw, s5_a_re, s5_a_im,
           s5_log_dt, s5_b_re, s5_b_im, s5_c_re, s5_c_im, s5_d, s5_glu_w, s5_glu_b, ffn_w1, ffn_w3, ffn_w2,
           router_w, moe_w1, moe_w3, moe_w2):
    B, T, D = x.shape
    Tc = ctx.shape[1]
    depth = w_mod.shape[0]
    assert B == 1 and Tc % ROW_BLOCK == 0 and T % ROW_BLOCK == 0
    GW = D // N_GROUPS
    n_ctx_blocks = Tc // ROW_BLOCK
    alpha = (2.0 * depth) ** 0.25

    cc = jnp.zeros((SUBLANES, D), F32).at[0].set(c[0]).at[1].set(c_ctx)
    mod = _modulation(cc, w_mod, b_mod)
    cos, sin = _rope_tables(T, Tc, GW)
    xs = jnp.concatenate([ctx[0], x[0]], axis=0)

    POOL_B, Q_B, K_B, V_B, CONV_B, S5_B = 0, 1, 2, 3, 2, 6

    for l in range(depth):
        last = l == depth - 1
        lam_init = 0.8 - 0.6 * math.exp(-0.3 * l)
        z = _in_projection(xs, mod, w_in, l, n_ctx_blocks)

        pa = _pool_mixer(z, pool_w, pool_scale, l, n_ctx_blocks)

        q, kt, v = _qkv_prep(z, cos, sin, Q_B, K_B, V_B)
        lv = diff_lambda[l].astype(F32)
        lam = jnp.exp(jnp.sum(lv[0] * lv[1])) - jnp.exp(jnp.sum(lv[2] * lv[3])) + lam_init
        lam_row = jnp.full((1, LANES), lam, F32)
        g_row = (diff_subln_g[l].astype(F32) * (1.0 - lam_init)).reshape(1, GW)
        pb = _diff_attention(q, kt, v, lam_row, g_row, Tc)

        pcv = _conv_mixer(z, CONV_B, conv_dw, conv_db, conv_ln_g, conv_ln_b, conv_pw, l, n_ctx_blocks)

        pf = _s5_params(s5_a_re[l, 0], s5_a_im[l, 0], s5_log_dt[l, 0], s5_b_re[l, 0], s5_b_im[l, 0],
                        s5_c_re[l, 0], s5_c_im[l, 0])
        pr = _s5_params(s5_a_re[l, 1], s5_a_im[l, 1], s5_log_dt[l, 1], s5_b_re[l, 1], s5_b_im[l, 1],
                        s5_c_re[l, 1], s5_c_im[l, 1])
        yf, yr = _s5_scan(z, S5_B, pf, pr, n_ctx_blocks)
        pd = _s5_glu(yf, yr, z, S5_B, s5_d, s5_glu_w, s5_glu_b, l)

        w_out_bf = _cast_bf16(w_out, l)
        row_off = n_ctx_blocks if last else 0
        if l % 2 == 0:
            x1, h = _mix_out((pa, pb, pcv, pd), w_out_bf, xs, mod, ln1_g, ln1_b, l, alpha, n_ctx_blocks, None)
            S = xs.shape[0]
            tm = next(t for t in (768, 512, ROW_BLOCK) if S % t == 0)
            n_tiles = S // tm
            f = _swiglu_tiles(h, ffn_w1[l // 2][None], ffn_w3[l // 2][None], ffn_w2[l // 2][None],
                              jnp.zeros((n_tiles,), jnp.int32), jnp.full((1,), n_tiles, jnp.int32), tm)
            xs_new = _ln2(x1, f, mod, ln2_g, ln2_b, l, alpha, n_ctx_blocks, row_off)
        else:
            x1, h, idx, gates = _mix_out((pa, pb, pcv, pd), w_out_bf, xs, mod, ln1_g, ln1_b, l, alpha,
                                         n_ctx_blocks, router_w[l // 2])
            n_exp = router_w.shape[-1]
            row0 = row_off * ROW_BLOCK
            tok_of_slot, tile_expert, n_used, slot_a, slot_b = _route_slots(idx, row0, n_exp, FFN_TM)
            hs = _gather_rows(h, tok_of_slot, FFN_TM)
            y = _swiglu_tiles(hs, moe_w1[l // 2], moe_w3[l // 2], moe_w2[l // 2], tile_expert, n_used, FFN_TM)
            xs_new = _ln2(x1, y, mod, ln2_g, ln2_b, l, alpha, n_ctx_blocks, row_off, moe=(slot_a, slot_b, gates))
        xs = xs_new
    return xs[None]
```

```python
import functools
import math

import numpy as np
import jax
import jax.numpy as jnp
from jax import lax
from jax.experimental import pallas as pl
from jax.experimental.pallas import tpu as pltpu

F32 = jnp.float32
BF16 = jnp.bfloat16

GRID_W = 64
N_GROUPS = 4
POOL_WINDOWS = (2, 4, 8, 16)
DIFF_HEADS = 8
DIFF_QK = 32
CONV_WIDTH = 31
S5_P = 16
S5_N = 64
TOP_K = 2
ROPE_BASE = 10000.0
LN_EPS = 1e-5

LANES = 128
SUBLANES = 8
ROW_BLOCK = 256
VMEM_LIMIT = 56 * 1024 * 1024


def _cparams(n_axes, vmem=VMEM_LIMIT):
    return pltpu.CompilerParams(dimension_semantics=("arbitrary",) * n_axes, vmem_limit_bytes=vmem)


def _layer_norm(y, g, b):
    mu = jnp.mean(y, -1, keepdims=True)
    yc = y - mu
    var = jnp.mean(yc * yc, -1, keepdims=True)
    return yc * lax.rsqrt(var + LN_EPS) * g + b


def _silu(x):
    return x * jax.nn.sigmoid(x)


def _mod_kernel(cc_ref, w_ref, b_ref, o_ref):
    a = _silu(cc_ref[...])
    o_ref[...] = jnp.dot(a.astype(BF16), w_ref[...].astype(BF16), preferred_element_type=F32) + b_ref[...]


def _modulation(cc, w_mod, b_mod):
    L, D, N = w_mod.shape
    tn = 1536
    assert N % tn == 0
    return pl.pallas_call(
        _mod_kernel,
        grid=(L, N // tn),
        in_specs=[pl.BlockSpec((SUBLANES, D), lambda l, j: (0, 0)),
                  pl.BlockSpec((None, D, tn), lambda l, j: (l, 0, j)),
                  pl.BlockSpec((None, 1, tn), lambda l, j: (l, 0, j))],
        out_specs=pl.BlockSpec((None, SUBLANES, tn), lambda l, j: (l, 0, j)),
        out_shape=jax.ShapeDtypeStruct((L, SUBLANES, N), F32),
        compiler_params=_cparams(2),
    )(cc, w_mod, b_mod.reshape(L, 1, N))


def _mod_spec(layer, chunk, D, n_grid_axes):
    if n_grid_axes == 1:
        return pl.BlockSpec((None, SUBLANES, D), lambda i: (layer, 0, chunk))
    return pl.BlockSpec((None, SUBLANES, D), lambda j, i: (layer, 0, chunk))


def _pick(m, is_ctx):
    return jnp.where(is_ctx, m[1:2, :], m[0:1, :])


def _inproj_kernel(x_ref, sh_ref, sc_ref, w_ref, o_ref, wb_ref, *, n_ctx_blocks):
    i = pl.program_id(1)

    @pl.when(i == 0)
    def _():
        wb_ref[...] = w_ref[...].astype(BF16)

    is_ctx = i < n_ctx_blocks
    h = x_ref[...] * (1.0 + _pick(sc_ref[...], is_ctx)) + _pick(sh_ref[...], is_ctx)
    o_ref[...] = jnp.dot(h.astype(BF16), wb_ref[...], preferred_element_type=F32).astype(o_ref.dtype)


def _in_projection(xs, mod, w_in, layer, n_ctx_blocks):
    S, D = xs.shape
    N = w_in.shape[-1]
    tn = 1792
    assert N % tn == 0 and S % ROW_BLOCK == 0
    return pl.pallas_call(
        functools.partial(_inproj_kernel, n_ctx_blocks=n_ctx_blocks),
        grid=(N // tn, S // ROW_BLOCK),
        in_specs=[pl.BlockSpec((ROW_BLOCK, D), lambda j, i: (i, 0)),
                  _mod_spec(layer, 0, D, 2),
                  _mod_spec(layer, 1, D, 2),
                  pl.BlockSpec((None, D, tn), lambda j, i: (layer, 0, j))],
        out_specs=pl.BlockSpec((ROW_BLOCK, tn), lambda j, i: (i, j)),
        out_shape=jax.ShapeDtypeStruct((S, N), BF16),
        scratch_shapes=[pltpu.VMEM((D, tn), BF16)],
        compiler_params=_cparams(2),
    )(xs, mod, mod, w_in)


def _seq_edges(i, n_blocks, n_ctx_blocks):
    prev_ok = jnp.logical_and(i != 0, i != n_ctx_blocks)
    next_ok = jnp.logical_and(i != n_ctx_blocks - 1, i != n_blocks - 1)
    return prev_ok, next_ok


def _pool_kernel(p_ref, c_ref, n_ref, w_ref, scale_ref, o_ref, ext_ref, *, n_blocks, n_ctx_blocks):
    i = pl.program_id(0)
    R = ROW_BLOCK
    halo = SUBLANES
    prev_ok, next_ok = _seq_edges(i, n_blocks, n_ctx_blocks)
    cur = c_ref[...].astype(F32)
    pack = 2 * SUBLANES
    ext_ref[0:halo, :] = jnp.where(prev_ok, p_ref[R - pack:R, :].astype(F32)[pack - halo:], 0.0)
    ext_ref[halo:halo + R, :] = cur
    ext_ref[halo + R:halo + R + halo, :] = jnp.where(next_ok, n_ref[0:pack, :].astype(F32)[:halo], 0.0)
    rloc = lax.broadcasted_iota(jnp.int32, (R, 1), 0)
    gw = cur.shape[1] // len(POOL_WINDOWS)
    outs = []
    for g, w in enumerate(POOL_WINDOWS):
        acc = jnp.zeros((R, gw), F32)
        cnt = jnp.zeros((R, 1), F32)
        for d in range(-(w // 2), w - w // 2):
            acc = acc + ext_ref[halo + d:halo + d + R, g * gw:(g + 1) * gw]
            valid = jnp.logical_and(jnp.logical_or(rloc + d >= 0, prev_ok),
                                    jnp.logical_or(rloc + d < R, next_ok))
            cnt = cnt + valid.astype(F32)
        diff = acc / cnt - cur[:, g * gw:(g + 1) * gw]
        outs.append(jnp.dot(diff.astype(BF16), w_ref[g].astype(BF16), preferred_element_type=F32))
    o_ref[...] = jnp.concatenate(outs, axis=-1) * scale_ref[...]


def _pool_mixer(z, pool_w, pool_scale, layer, n_ctx_blocks):
    S = z.shape[0]
    nb = S // ROW_BLOCK
    GW = pool_scale.shape[-1]
    G, gw = pool_w.shape[1], pool_w.shape[2]
    return pl.pallas_call(
        functools.partial(_pool_kernel, n_blocks=nb, n_ctx_blocks=n_ctx_blocks),
        grid=(nb,),
        in_specs=[pl.BlockSpec((ROW_BLOCK, GW), lambda i: (jnp.maximum(i - 1, 0), 0)),
                  pl.BlockSpec((ROW_BLOCK, GW), lambda i: (i, 0)),
                  pl.BlockSpec((ROW_BLOCK, GW), lambda i: (jnp.minimum(i + 1, nb - 1), 0)),
                  pl.BlockSpec((None, G, gw, gw), lambda i: (layer, 0, 0, 0)),
                  pl.BlockSpec((None, 1, GW), lambda i: (layer, 0, 0))],
        out_specs=pl.BlockSpec((ROW_BLOCK, GW), lambda i: (i, 0)),
        out_shape=jax.ShapeDtypeStruct((S, GW), F32),
        scratch_shapes=[pltpu.VMEM((ROW_BLOCK + 2 * SUBLANES, GW), F32)],
        compiler_params=_cparams(1),
    )(z, z, z, pool_w, pool_scale.reshape(pool_scale.shape[0], 1, GW))


CONV_HALO = 16


def _conv_kernel(p_ref, c_ref, n_ref, dw_ref, db_ref, g_ref, b_ref, pw_ref, o_ref, ext_ref,
                 *, n_blocks, n_ctx_blocks):
    i = pl.program_id(0)
    R = ROW_BLOCK
    H = CONV_HALO
    GW = o_ref.shape[1]
    prev_ok, next_ok = _seq_edges(i, n_blocks, n_ctx_blocks)

    def glu(u):
        u = u.astype(F32)
        return u[:, :GW] * jax.nn.sigmoid(u[:, GW:])

    ext_ref[0:H, :] = jnp.where(prev_ok, glu(p_ref[R - H:R, :]), 0.0)
    ext_ref[H:H + R, :] = glu(c_ref[...])
    ext_ref[H + R:H + R + H, :] = jnp.where(next_ok, glu(n_ref[0:H, :]), 0.0)
    off = H - CONV_WIDTH // 2
    acc = jnp.zeros((R, GW), F32)
    for j in range(CONV_WIDTH):
        acc = acc + ext_ref[off + j:off + j + R, :] * dw_ref[j:j + 1, :]
    y = _layer_norm(acc + db_ref[...], g_ref[...], b_ref[...])
    o_ref[...] = jnp.dot(_silu(y).astype(BF16), pw_ref[...].astype(BF16), preferred_element_type=F32)


def _conv_mixer(z, col_block, conv_dw, conv_db, conv_ln_g, conv_ln_b, conv_pw, layer, n_ctx_blocks):
    S = z.shape[0]
    nb = S // ROW_BLOCK
    GW = conv_db.shape[-1]
    L = conv_db.shape[0]
    vec = lambda a: a.reshape(L, 1, GW)
    vspec = pl.BlockSpec((None, 1, GW), lambda i: (layer, 0, 0))
    return pl.pallas_call(
        functools.partial(_conv_kernel, n_blocks=nb, n_ctx_blocks=n_ctx_blocks),
        grid=(nb,),
        in_specs=[pl.BlockSpec((ROW_BLOCK, 2 * GW), lambda i: (jnp.maximum(i - 1, 0), col_block)),
                  pl.BlockSpec((ROW_BLOCK, 2 * GW), lambda i: (i, col_block)),
                  pl.BlockSpec((ROW_BLOCK, 2 * GW), lambda i: (jnp.minimum(i + 1, nb - 1), col_block)),
                  pl.BlockSpec((None, CONV_WIDTH, GW), lambda i: (layer, 0, 0)),
                  vspec, vspec, vspec,
                  pl.BlockSpec((None, GW, GW), lambda i: (layer, 0, 0))],
        out_specs=pl.BlockSpec((ROW_BLOCK, GW), lambda i: (i, 0)),
        out_shape=jax.ShapeDtypeStruct((S, GW), F32),
        scratch_shapes=[pltpu.VMEM((ROW_BLOCK + 2 * CONV_HALO, GW), F32)],
        compiler_params=_cparams(1),
    )(z, z, z, conv_dw, vec(conv_db), vec(conv_ln_g), vec(conv_ln_b), conv_pw)


def _rope_tables(T, Tc, width):
    ax = DIFF_QK // 2
    inv = ROPE_BASE ** (-jnp.arange(0, ax, 2, dtype=F32) / ax)
    t = jnp.arange(T)
    row = (t // GRID_W).astype(F32)
    col = (t % GRID_W).astype(F32)
    ang = jnp.stack([row[:, None] * inv, col[:, None] * inv], axis=1)
    cos = jnp.cos(ang)[:, :, None, :]
    sin = jnp.sin(ang)[:, :, None, :]
    cos = jnp.broadcast_to(cos, (T, 2, 2, ax // 2)).reshape(T, DIFF_QK)
    sin = jnp.concatenate([-sin, sin], axis=2).reshape(T, DIFF_QK)
    reps = width // DIFF_QK
    cos = jnp.concatenate([jnp.ones((Tc, DIFF_QK), F32), cos], axis=0)
    sin = jnp.concatenate([jnp.zeros((Tc, DIFF_QK), F32), sin], axis=0)
    return jnp.tile(cos, (1, reps)), jnp.tile(sin, (1, reps))


def _qkv_prep_kernel(q_ref, k_ref, v_ref, cos_ref, sin_ref, qo_ref, ko_ref, vo_ref):
    W = q_ref.shape[1]
    half = DIFF_QK // 4
    lane = lax.broadcasted_iota(jnp.int32, (1, W), 1)
    first = (lane % (2 * half)) < half
    cos = cos_ref[...]
    sin = sin_ref[...]

    def rope(x):
        partner = jnp.where(first, pltpu.roll(x, W - half, 1), pltpu.roll(x, half, 1))
        return x * cos + partner * sin

    qo_ref[...] = (rope(q_ref[...].astype(F32)) * (DIFF_QK ** -0.5 * math.log2(math.e))).T.astype(BF16)
    ko_ref[...] = rope(k_ref[...].astype(F32)).astype(BF16)
    vt = v_ref[...].astype(F32).T.astype(BF16)
    dv = LANES // 2
    ones = jnp.ones((dv, vt.shape[1]), BF16)
    for h in range(W // dv):
        vo_ref[2 * h * dv:(2 * h + 1) * dv, :] = vt[h * dv:(h + 1) * dv, :]
        vo_ref[(2 * h + 1) * dv:(2 * h + 2) * dv, :] = ones


def _qkv_prep(z, cos, sin, q_blk, k_blk, v_blk):
    S = z.shape[0]
    W = cos.shape[1]
    nb = S // ROW_BLOCK
    row = lambda c: pl.BlockSpec((ROW_BLOCK, W), lambda i: (i, c))
    return pl.pallas_call(
        _qkv_prep_kernel,
        grid=(nb,),
        in_specs=[row(q_blk), row(k_blk), row(v_blk), row(0), row(0)],
        out_specs=[pl.BlockSpec((W, ROW_BLOCK), lambda i: (0, i)), row(0),
                   pl.BlockSpec((2 * W, ROW_BLOCK), lambda i: (0, i))],
        out_shape=[jax.ShapeDtypeStruct((W, S), BF16), jax.ShapeDtypeStruct((S, W), BF16),
                   jax.ShapeDtypeStruct((2 * W, S), BF16)],
        compiler_params=_cparams(1),
    )(z, z, z, cos, sin)


ATTN_TQ = 256
ATTN_TK = (4096, 2048, 1024)


def _attn_kernel(qt_ref, k_ref, vt_ref, lam_ref, g_ref, o_ref, qq_ref, m_ref, acc_ref,
                 *, n_ctx, n_ctx_blocks, n_lat_chunks, tk):
    i = pl.program_id(1)
    tq = ATTN_TQ
    dv = LANES // 2
    qt = qt_ref[...]
    feat = lax.broadcasted_iota(jnp.int32, (LANES, 1), 0)
    zero = jnp.zeros_like(qt)
    for hh in range(2):
        for comp in range(2):
            lo = hh * dv + comp * DIFF_QK
            keep = jnp.logical_and(feat >= lo, feat < lo + DIFF_QK)
            qq_ref[hh, :, comp * tq:(comp + 1) * tq] = jnp.where(keep, qt, zero)

    def attend(start, size, first):
        kk = k_ref[pl.ds(start, size), :]
        for hh in range(2):
            s = jnp.dot(kk, qq_ref[hh], preferred_element_type=F32)
            vv = vt_ref[hh * LANES:(hh + 1) * LANES, pl.ds(start, size)]
            mx = jnp.max(s, axis=0, keepdims=True)
            if first:
                m_ref[hh] = mx
                p = jnp.exp2(s - mx)
                acc_ref[hh] = jnp.dot(vv, p.astype(BF16), preferred_element_type=F32)
            else:
                m_old = m_ref[hh]
                m_new = jnp.maximum(m_old, mx)
                m_ref[hh] = m_new
                p = jnp.exp2(s - m_new)
                acc_ref[hh] = (jnp.exp2(m_old - m_new) * acc_ref[hh]
                               + jnp.dot(vv, p.astype(BF16), preferred_element_type=F32))

    attend(0, n_ctx, True)
    n_steps = jnp.where(i < n_ctx_blocks, 0, n_lat_chunks)

    def body(c, carry):
        attend(pl.multiple_of(n_ctx + c * tk, LANES), tk, False)
        return carry

    lax.fori_loop(0, n_steps, body, 0)

    lam = lam_ref[...]
    outs = []
    for hh in range(2):
        acc = acc_ref[hh]
        ratio = acc[:dv] / acc[dv:dv + 1]
        o = ratio[:, :tq] - lam * ratio[:, tq:]
        r = lax.rsqrt(jnp.sum(o * o, axis=0, keepdims=True) / dv + LN_EPS)
        outs.append(o * r)
    o_ref[...] = (jnp.concatenate(outs, axis=0) * g_ref[...]).T


def _diff_attention(qt, k, vt, lam, g, n_ctx):
    W, S = qt.shape
    assert S % ATTN_TQ == 0 and n_ctx % ATTN_TQ == 0
    tk = next(t for t in ATTN_TK if (S - n_ctx) % t == 0)
    nq = S // ATTN_TQ
    return pl.pallas_call(
        functools.partial(_attn_kernel, n_ctx=n_ctx, n_ctx_blocks=n_ctx // ATTN_TQ,
                          n_lat_chunks=(S - n_ctx) // tk, tk=tk),
        grid=(W // LANES, nq),
        in_specs=[pl.BlockSpec((LANES, ATTN_TQ), lambda p, i: (p, i)),
                  pl.BlockSpec((S, LANES), lambda p, i: (0, p)),
                  pl.BlockSpec((2 * LANES, S), lambda p, i: (p, 0)),
                  pl.BlockSpec((1, ATTN_TQ), lambda p, i: (0, 0)),
                  pl.BlockSpec((LANES, 1), lambda p, i: (p, 0))],
        out_specs=pl.BlockSpec((ATTN_TQ, LANES), lambda p, i: (i, p)),
        out_shape=jax.ShapeDtypeStruct((S, W), F32),
        scratch_shapes=[pltpu.VMEM((2, LANES, 2 * ATTN_TQ), BF16),
                        pltpu.VMEM((2, 1, 2 * ATTN_TQ), F32),
                        pltpu.VMEM((2, LANES, 2 * ATTN_TQ), F32)],
        compiler_params=_cparams(2),
    )(qt, k, vt, lam, g)


def _s5_params(a_re, a_im, log_dt, b_re, b_im, c_re, c_im):
    G, N = a_re.shape
    P = b_re.shape[-1]
    A = lax.complex(a_re.astype(F32), a_im.astype(F32))
    dt = jnp.exp(log_dt.astype(F32))[:, None]
    abar = jnp.exp(dt * A)
    bbar = ((abar - 1.0) / A)[..., None] * lax.complex(b_re.astype(F32), b_im.astype(F32))
    eye = jnp.eye(G, dtype=F32)
    wb = lambda m: jnp.einsum('gh,gnp->gphn', eye, m).reshape(G * P, G * N)
    w_in = jnp.concatenate([wb(jnp.real(bbar)), wb(jnp.imag(bbar))], axis=1)
    cm = lambda m: jnp.einsum('gh,gpn->hngp', eye, m.astype(F32)).reshape(G * N, G * P)
    w_out = jnp.concatenate([cm(c_re), -cm(c_im)], axis=0)
    coef = jnp.concatenate([jnp.real(abar).reshape(1, G * N), jnp.imag(abar).reshape(1, G * N)], axis=1)
    return w_in.astype(BF16), coef, w_out.astype(BF16)


def _s5_kernel(uf_ref, ur_ref, wbf_ref, wbr_ref, af_ref, ar_ref, cf_ref, cr_ref, yf_ref, yr_ref,
               bf_ref, br_ref, hf_ref, hr_ref):
    i = pl.program_id(0)
    R = ROW_BLOCK
    NS = af_ref.shape[1] // 2

    @pl.when(i == 0)
    def _():
        hf_ref[...] = jnp.zeros_like(hf_ref)
        hr_ref[...] = jnp.zeros_like(hr_ref)

    bf_ref[...] = jnp.dot(uf_ref[...].astype(BF16), wbf_ref[...], preferred_element_type=F32)
    br_ref[...] = jnp.dot(ur_ref[...].astype(BF16), wbr_ref[...], preferred_element_type=F32)

    def scan(buf_ref, a_ref, h_ref, reverse):
        are = a_ref[:, :NS]
        aim = a_ref[:, NS:]

        def step(t, carry):
            hre, him = carry
            row = (R - 1 - t) if reverse else t
            bu = buf_ref[pl.ds(row, 1), :]
            nre = are * hre - aim * him + bu[:, :NS]
            nim = are * him + aim * hre + bu[:, NS:]
            buf_ref[pl.ds(row, 1), :] = jnp.concatenate([nre, nim], axis=1)
            return nre, nim

        hre, him = lax.fori_loop(0, R, step, (h_ref[:, :NS], h_ref[:, NS:]))
        h_ref[...] = jnp.concatenate([hre, him], axis=1)

    scan(bf_ref, af_ref, hf_ref, False)
    scan(br_ref, ar_ref, hr_ref, True)
    yf_ref[...] = jnp.dot(bf_ref[...].astype(BF16), cf_ref[...], preferred_element_type=F32)
    yr_ref[...] = jnp.dot(br_ref[...].astype(BF16), cr_ref[...], preferred_element_type=F32)


def _s5_scan(z, col_block, pf, pr, n_ctx_blocks):
    S = z.shape[0]
    nb = S // ROW_BLOCK
    GW = pf[0].shape[0]
    NS2 = pf[0].shape[1]

    def rev_block(i):
        return jnp.where(i < n_ctx_blocks, n_ctx_blocks - 1 - i, nb - 1 - i + n_ctx_blocks)

    full = lambda a: pl.BlockSpec(a.shape, lambda i: (0, 0))
    return pl.pallas_call(
        _s5_kernel,
        grid=(nb,),
        in_specs=[pl.BlockSpec((ROW_BLOCK, GW), lambda i: (i, col_block)),
                  pl.BlockSpec((ROW_BLOCK, GW), lambda i: (rev_block(i), col_block)),
                  full(pf[0]), full(pr[0]), full(pf[1]), full(pr[1]), full(pf[2]), full(pr[2])],
        out_specs=[pl.BlockSpec((ROW_BLOCK, GW), lambda i: (i, 0)),
                   pl.BlockSpec((ROW_BLOCK, GW), lambda i: (rev_block(i), 0))],
        out_shape=[jax.ShapeDtypeStruct((S, GW), F32), jax.ShapeDtypeStruct((S, GW), F32)],
        scratch_shapes=[pltpu.VMEM((ROW_BLOCK, NS2), F32), pltpu.VMEM((ROW_BLOCK, NS2), F32),
                        pltpu.VMEM((1, NS2), F32), pltpu.VMEM((1, NS2), F32)],
        compiler_params=_cparams(1),
    )(z, z, pf[0], pr[0], pf[1], pr[1], pf[2], pr[2])


def _s5_glu_kernel(yf_ref, yr_ref, u_ref, d_ref, w_ref, b_ref, o_ref):
    y = yf_ref[...] + yr_ref[...] + d_ref[...] * u_ref[...].astype(F32)
    zz = jax.nn.gelu(y)
    gate = jnp.dot(zz.astype(BF16), w_ref[...].astype(BF16), preferred_element_type=F32) + b_ref[...]
    o_ref[...] = zz * jax.nn.sigmoid(gate)


def _s5_glu(yf, yr, z, col_block, s5_d, glu_w, glu_b, layer):
    S, GW = yf.shape
    L = s5_d.shape[0]
    row = lambda c: pl.BlockSpec((ROW_BLOCK, GW), lambda i: (i, c))
    vspec = pl.BlockSpec((None, 1, GW), lambda i: (layer, 0, 0))
    return pl.pallas_call(
        _s5_glu_kernel,
        grid=(S // ROW_BLOCK,),
        in_specs=[row(0), row(0), row(col_block), vspec,
                  pl.BlockSpec((None, GW, GW), lambda i: (layer, 0, 0)), vspec],
        out_specs=row(0),
        out_shape=jax.ShapeDtypeStruct((S, GW), F32),
        compiler_params=_cparams(1),
    )(yf, yr, z, s5_d.reshape(L, 1, GW), glu_w, glu_b.reshape(L, 1, GW))


def _cast_kernel(x_ref, o_ref):
    o_ref[...] = x_ref[...].astype(o_ref.dtype)


def _cast_bf16(w, layer):
    _, K, N = w.shape
    tk = 512
    return pl.pallas_call(
        _cast_kernel,
        grid=(K // tk,),
        in_specs=[pl.BlockSpec((None, tk, N), lambda i: (layer, i, 0))],
        out_specs=pl.BlockSpec((tk, N), lambda i: (i, 0)),
        out_shape=jax.ShapeDtypeStruct((K, N), BF16),
        compiler_params=_cparams(1),
    )(w)


def _mixout_kernel(*refs, alpha, n_ctx_blocks, route, h_dtype):
    (pa_ref, pb_ref, pc_ref, pd_ref, w_ref, x_ref, g1_ref, lg_ref, lb_ref, sh_ref, sc_ref) = refs[:11]
    if route:
        rw_ref, x1_ref, h_ref, idx_ref, gate_ref = refs[11:]
    else:
        x1_ref, h_ref = refs[11:]
    i = pl.program_id(0)
    is_ctx = i < n_ctx_blocks
    GW = pa_ref.shape[1]
    mix = jnp.zeros(x_ref.shape, F32)
    for k, p_ref in enumerate((pa_ref, pb_ref, pc_ref, pd_ref)):
        mix = mix + jnp.dot(p_ref[...].astype(BF16), w_ref[k * GW:(k + 1) * GW, :],
                            preferred_element_type=F32)
    y = alpha * x_ref[...] + _pick(g1_ref[...], is_ctx) * mix
    x1 = _layer_norm(y, lg_ref[...], lb_ref[...])
    x1_ref[...] = x1
    h = x1 * (1.0 + _pick(sc_ref[...], is_ctx)) + _pick(sh_ref[...], is_ctx)
    h_ref[...] = h.astype(h_dtype)
    if route:
        rw = rw_ref[...]
        h_hi = h.astype(BF16)
        h_lo = (h - h_hi.astype(F32)).astype(BF16)
        w_hi = rw.astype(BF16)
        w_lo = (rw - w_hi.astype(F32)).astype(BF16)
        logits = (jnp.dot(h_hi, w_hi, preferred_element_type=F32)
                  + (jnp.dot(h_lo, w_hi, preferred_element_type=F32)
                     + jnp.dot(h_hi, w_lo, preferred_element_type=F32)))
        n_exp = rw_ref.shape[1]
        lane = lax.broadcasted_iota(jnp.int32, logits.shape, 1)
        m1 = jnp.max(logits, axis=-1, keepdims=True)
        i1 = jnp.min(jnp.where(logits == m1, lane, n_exp), axis=-1, keepdims=True)
        rest = jnp.where(lane == i1, -jnp.inf, logits)
        m2 = jnp.max(rest, axis=-1, keepdims=True)
        i2 = jnp.min(jnp.where(rest == m2, lane, n_exp), axis=-1, keepdims=True)
        e2 = jnp.exp(m2 - m1)
        idx_ref[...] = jnp.concatenate([i1, i2], axis=1)
        gate_ref[...] = jnp.concatenate([1.0 / (1.0 + e2), e2 / (1.0 + e2)], axis=1)


def _mix_out(parts, w_out_bf, xs, mod, ln_g, ln_b, layer, alpha, n_ctx_blocks, router_w):
    S, D = xs.shape
    GW = parts[0].shape[1]
    L = ln_g.shape[0]
    route = router_w is not None
    h_dtype = F32 if route else BF16
    part = pl.BlockSpec((ROW_BLOCK, GW), lambda i: (i, 0))
    rows = pl.BlockSpec((ROW_BLOCK, D), lambda i: (i, 0))
    vspec = pl.BlockSpec((None, 1, D), lambda i: (layer, 0, 0))
    in_specs = [part, part, part, part,
                pl.BlockSpec((D, D), lambda i: (0, 0)), rows,
                _mod_spec(layer, 2, D, 1), vspec, vspec, _mod_spec(layer, 3, D, 1), _mod_spec(layer, 4, D, 1)]
    args = list(parts) + [w_out_bf, xs, mod, ln_g.reshape(L, 1, D), ln_b.reshape(L, 1, D), mod, mod]
    out_specs = [rows, rows]
    out_shape = [jax.ShapeDtypeStruct((S, D), F32), jax.ShapeDtypeStruct((S, D), h_dtype)]
    if route:
        E = router_w.shape[-1]
        in_specs.append(pl.BlockSpec((D, E), lambda i: (0, 0)))
        args.append(router_w)
        out_specs += [pl.BlockSpec((ROW_BLOCK, TOP_K), lambda i: (i, 0))] * 2
        out_shape += [jax.ShapeDtypeStruct((S, TOP_K), jnp.int32), jax.ShapeDtypeStruct((S, TOP_K), F32)]
    return pl.pallas_call(
        functools.partial(_mixout_kernel, alpha=alpha, n_ctx_blocks=n_ctx_blocks, route=route, h_dtype=h_dtype),
        grid=(S // ROW_BLOCK,),
        in_specs=in_specs,
        out_specs=out_specs,
        out_shape=out_shape,
        compiler_params=_cparams(1),
    )(*args)


FFN_TM = 512
FFN_TF = 512
FFN_TN = 512


def _expert_changed(te_ref, i):
    prev = te_ref[jnp.maximum(i - 1, 0)]
    return jnp.logical_or(i == 0, te_ref[i] != prev)


def _ffn_up_kernel(te_ref, nu_ref, h_ref, w1_ref, w3_ref, o_ref, w1b_ref, w3b_ref):
    i = pl.program_id(1)

    @pl.when(_expert_changed(te_ref, i))
    def _():
        w1b_ref[...] = w1_ref[...].astype(BF16)
        w3b_ref[...] = w3_ref[...].astype(BF16)

    @pl.when(i < nu_ref[0])
    def _():
        h = h_ref[...]
        a = jnp.dot(h, w1b_ref[...], preferred_element_type=F32)
        b = jnp.dot(h, w3b_ref[...], preferred_element_type=F32)
        o_ref[...] = (_silu(a) * b).astype(o_ref.dtype)

    @pl.when(i >= nu_ref[0])
    def _():
        o_ref[...] = jnp.zeros_like(o_ref)


def _ffn_down_kernel(te_ref, nu_ref, g_ref, w2_ref, o_ref, w2b_ref):
    i = pl.program_id(1)

    @pl.when(_expert_changed(te_ref, i))
    def _():
        w2b_ref[...] = w2_ref[...].astype(BF16)

    @pl.when(i < nu_ref[0])
    def _():
        o_ref[...] = jnp.dot(g_ref[...], w2b_ref[...], preferred_element_type=F32)

    @pl.when(i >= nu_ref[0])
    def _():
        o_ref[...] = jnp.zeros_like(o_ref)


def _swiglu_tiles(hs, w1, w3, w2, tile_expert, n_used, tm):
    R, D = hs.shape
    _, _, F = w1.shape
    n_tiles = R // tm
    tf = FFN_TF if F % FFN_TF == 0 else F
    tn = FFN_TN
    assert R % tm == 0 and F % tf == 0 and D % tn == 0
    g = pl.pallas_call(
        _ffn_up_kernel,
        grid_spec=pltpu.PrefetchScalarGridSpec(
            num_scalar_prefetch=2,
            grid=(F // tf, n_tiles),
            in_specs=[pl.BlockSpec((tm, D), lambda j, i, te, nu: (i, 0)),
                      pl.BlockSpec((None, D, tf), lambda j, i, te, nu: (te[i], 0, j)),
                      pl.BlockSpec((None, D, tf), lambda j, i, te, nu: (te[i], 0, j))],
            out_specs=pl.BlockSpec((tm, tf), lambda j, i, te, nu: (i, j)),
            scratch_shapes=[pltpu.VMEM((D, tf), BF16), pltpu.VMEM((D, tf), BF16)]),
        out_shape=jax.ShapeDtypeStruct((R, F), BF16),
        compiler_params=_cparams(2),
    )(tile_expert, n_used, hs, w1, w3)
    return pl.pallas_call(
        _ffn_down_kernel,
        grid_spec=pltpu.PrefetchScalarGridSpec(
            num_scalar_prefetch=2,
            grid=(D // tn, n_tiles),
            in_specs=[pl.BlockSpec((tm, F), lambda j, i, te, nu: (i, 0)),
                      pl.BlockSpec((None, F, tn), lambda j, i, te, nu: (te[i], 0, j))],
            out_specs=pl.BlockSpec((tm, tn), lambda j, i, te, nu: (i, j)),
            scratch_shapes=[pltpu.VMEM((F, tn), BF16)]),
        out_shape=jax.ShapeDtypeStruct((R, D), F32),
        compiler_params=_cparams(2),
    )(tile_expert, n_used, g, w2)


def _row_copy(src_ref, dst_ref, src_row, dst_row, sem):
    return pltpu.make_async_copy(src_ref.at[pl.ds(src_row, 1)], dst_ref.at[pl.ds(dst_row, 1)], sem)


def _gather_kernel(tok_ref, src_ref, o_ref, buf_ref, sem):
    i = pl.program_id(0)
    tm = buf_ref.shape[0]

    def issue(r, c):
        _row_copy(src_ref, buf_ref, tok_ref[i * tm + r], r, sem).start()
        return c

    def drain(r, c):
        _row_copy(src_ref, buf_ref, 0, r, sem).wait()
        return c

    lax.fori_loop(0, tm, issue, 0)
    lax.fori_loop(0, tm, drain, 0)
    o_ref[...] = buf_ref[...].astype(o_ref.dtype)


def _gather_rows(src, tok_of_slot, tm):
    R = tok_of_slot.shape[0]
    D = src.shape[1]
    return pl.pallas_call(
        _gather_kernel,
        grid_spec=pltpu.PrefetchScalarGridSpec(
            num_scalar_prefetch=1,
            grid=(R // tm,),
            in_specs=[pl.BlockSpec(memory_space=pl.ANY)],
            out_specs=pl.BlockSpec((tm, D), lambda i, tok: (i, 0)),
            scratch_shapes=[pltpu.VMEM((tm, D), src.dtype), pltpu.SemaphoreType.DMA(())]),
        out_shape=jax.ShapeDtypeStruct((R, D), BF16),
        compiler_params=_cparams(1),
    )(tok_of_slot, src)


def _ln2_dense_kernel(x_ref, f_ref, g2_ref, lg_ref, lb_ref, o_ref, *, alpha, n_ctx_blocks, row_off):
    is_ctx = (pl.program_id(0) + row_off) < n_ctx_blocks
    y = alpha * x_ref[...] + _pick(g2_ref[...], is_ctx) * f_ref[...]
    o_ref[...] = _layer_norm(y, lg_ref[...], lb_ref[...])


def _ln2_moe_kernel(sa_ref, sb_ref, x_ref, y_ref, gate_ref, g2_ref, lg_ref, lb_ref, o_ref, bufa_ref, bufb_ref,
                    sem, *, alpha, n_ctx_blocks, row_off):
    i = pl.program_id(0)
    R = ROW_BLOCK
    base = (i + row_off) * R

    def issue(r, c):
        _row_copy(y_ref, bufa_ref, sa_ref[base + r], r, sem).start()
        _row_copy(y_ref, bufb_ref, sb_ref[base + r], r, sem).start()
        return c

    def drain(r, c):
        _row_copy(y_ref, bufa_ref, 0, r, sem).wait()
        _row_copy(y_ref, bufb_ref, 0, r, sem).wait()
        return c

    lax.fori_loop(0, R, issue, 0)
    lax.fori_loop(0, R, drain, 0)
    gate = gate_ref[...]
    f = gate[:, 0:1] * bufa_ref[...] + gate[:, 1:2] * bufb_ref[...]
    is_ctx = (i + row_off) < n_ctx_blocks
    y = alpha * x_ref[...] + _pick(g2_ref[...], is_ctx) * f
    o_ref[...] = _layer_norm(y, lg_ref[...], lb_ref[...])


def _ln2(x1, f, mod, ln_g, ln_b, layer, alpha, n_ctx_blocks, row_off, moe=None):
    S, D = x1.shape
    L = ln_g.shape[0]
    nb = S // ROW_BLOCK - row_off
    n_pre = 0 if moe is None else 2
    wrap = (lambda f_: (lambda i, *_: f_(i)))
    rows_in = pl.BlockSpec((ROW_BLOCK, D), wrap(lambda i: (i + row_off, 0)))
    rows_out = pl.BlockSpec((ROW_BLOCK, D), wrap(lambda i: (i, 0)))
    vspec = pl.BlockSpec((None, 1, D), wrap(lambda i: (layer, 0, 0)))
    mspec = pl.BlockSpec((None, SUBLANES, D), wrap(lambda i: (layer, 0, 5)))
    common = dict(alpha=alpha, n_ctx_blocks=n_ctx_blocks, row_off=row_off)
    lg, lb = ln_g.reshape(L, 1, D), ln_b.reshape(L, 1, D)
    out_shape = jax.ShapeDtypeStruct((nb * ROW_BLOCK, D), F32)
    if moe is None:
        return pl.pallas_call(
            functools.partial(_ln2_dense_kernel, **common),
            grid=(nb,),
            in_specs=[rows_in, rows_in, mspec, vspec, vspec],
            out_specs=rows_out,
            out_shape=out_shape,
            compiler_params=_cparams(1),
        )(x1, f, mod, lg, lb)
    slot_a, slot_b, gates = moe
    return pl.pallas_call(
        functools.partial(_ln2_moe_kernel, **common),
        grid_spec=pltpu.PrefetchScalarGridSpec(
            num_scalar_prefetch=n_pre,
            grid=(nb,),
            in_specs=[rows_in, pl.BlockSpec(memory_space=pl.ANY),
                      pl.BlockSpec((ROW_BLOCK, TOP_K), wrap(lambda i: (i + row_off, 0))),
                      mspec, vspec, vspec],
            out_specs=rows_out,
            scratch_shapes=[pltpu.VMEM((ROW_BLOCK, D), F32), pltpu.VMEM((ROW_BLOCK, D), F32),
                            pltpu.SemaphoreType.DMA(())]),
        out_shape=out_shape,
        compiler_params=_cparams(1),
    )(slot_a, slot_b, x1, f, gates, mod, lg, lb)


def _route_slots(idx, row0, n_experts, tm):
    S = idx.shape[0]
    n = S - row0
    e_flat = idx[row0:].reshape(-1)
    onehot = (e_flat[:, None] == jnp.arange(n_experts, dtype=jnp.int32)[None, :]).astype(jnp.int32)
    pos = jnp.sum((jnp.cumsum(onehot, axis=0) - 1) * onehot, axis=1)
    counts = jnp.sum(onehot, axis=0)
    padded = ((counts + tm - 1) // tm) * tm
    ends = jnp.cumsum(padded)
    starts = ends - padded
    slot = starts[e_flat] + pos
    n_tiles = -(-(TOP_K * n) // tm) + n_experts
    tok = jnp.repeat(jnp.arange(n, dtype=jnp.int32) + row0, TOP_K)
    tok_of_slot = jnp.full((n_tiles * tm,), row0, jnp.int32).at[slot].set(tok)
    tile_start = jnp.arange(n_tiles, dtype=jnp.int32) * tm
    tile_expert = jnp.minimum(jnp.searchsorted(ends, tile_start, side='right'), n_experts - 1).astype(jnp.int32)
    n_used = (ends[-1] // tm).astype(jnp.int32).reshape(1)
    slot2 = slot.reshape(n, TOP_K).astype(jnp.int32)
    pad = jnp.zeros((row0,), jnp.int32)
    slot_a = jnp.concatenate([pad, slot2[:, 0]])
    slot_b = jnp.concatenate([pad, slot2[:, 1]])
    return tok_of_slot, tile_expert, n_used, slot_a, slot_b


def kernel(x, c, ctx, c_ctx, w_mod, b_mod, w_in, w_out, ln1_g, ln1_b, ln2_g, ln2_b, pool_w, pool_scale,
           diff_lambda, diff_subln_g, conv_dw, conv_db, conv_ln_g, conv_ln_b, conv_pw, s5_a_re, s5_a_im,
           s5_log_dt, s5_b_re, s5_b_im, s5_c_re, s5_c_im, s5_d, s5_glu_w, s5_glu_b, ffn_w1, ffn_w3, ffn_w2,
           router_w, moe_w1, moe_w3, moe_w2):
    B, T, D = x.shape
    Tc = ctx.shape[1]
    depth = w_mod.shape[0]
    assert B == 1 and Tc % ROW_BLOCK == 0 and T % ROW_BLOCK == 0
    GW = D // N_GROUPS
    n_ctx_blocks = Tc // ROW_BLOCK
    alpha = (2.0 * depth) ** 0.25

    cc = jnp.zeros((SUBLANES, D), F32).at[0].set(c[0]).at[1].set(c_ctx)
    mod = _modulation(cc, w_mod, b_mod)
    cos, sin = _rope_tables(T, Tc, GW)
    xs = jnp.concatenate([ctx[0], x[0]], axis=0)

    POOL_B, Q_B, K_B, V_B, CONV_B, S5_B = 0, 1, 2, 3, 2, 6

    for l in range(depth):
        last = l == depth - 1
        lam_init = 0.8 - 0.6 * math.exp(-0.3 * l)
        z = _in_projection(xs, mod, w_in, l, n_ctx_blocks)

        pa = _pool_mixer(z, pool_w, pool_scale, l, n_ctx_blocks)

        qt, kk, vt = _qkv_prep(z, cos, sin, Q_B, K_B, V_B)
        lv = diff_lambda[l].astype(F32)
        lam = jnp.exp(jnp.sum(lv[0] * lv[1])) - jnp.exp(jnp.sum(lv[2] * lv[3])) + lam_init
        lam_row = jnp.full((1, ATTN_TQ), lam, F32)
        g_col = (diff_subln_g[l].astype(F32) * (1.0 - lam_init)).reshape(GW, 1)
        pb = _diff_attention(qt, kk, vt, lam_row, g_col, Tc)

        pcv = _conv_mixer(z, CONV_B, conv_dw, conv_db, conv_ln_g, conv_ln_b, conv_pw, l, n_ctx_blocks)

        pf = _s5_params(s5_a_re[l, 0], s5_a_im[l, 0], s5_log_dt[l, 0], s5_b_re[l, 0], s5_b_im[l, 0],
                        s5_c_re[l, 0], s5_c_im[l, 0])
        pr = _s5_params(s5_a_re[l, 1], s5_a_im[l, 1], s5_log_dt[l, 1], s5_b_re[l, 1], s5_b_im[l, 1],
                        s5_c_re[l, 1], s5_c_im[l, 1])
        yf, yr = _s5_scan(z, S5_B, pf, pr, n_ctx_blocks)
        pd = _s5_glu(yf, yr, z, S5_B, s5_d, s5_glu_w, s5_glu_b, l)

        w_out_bf = _cast_bf16(w_out, l)
        row_off = n_ctx_blocks if last else 0
        if l % 2 == 0:
            x1, h = _mix_out((pa, pb, pcv, pd), w_out_bf, xs, mod, ln1_g, ln1_b, l, alpha, n_ctx_blocks, None)
            S = xs.shape[0]
            tm = next(t for t in (768, 512, ROW_BLOCK) if S % t == 0)
            n_tiles = S // tm
            f = _swiglu_tiles(h, ffn_w1[l // 2][None], ffn_w3[l // 2][None], ffn_w2[l // 2][None],
                              jnp.zeros((n_tiles,), jnp.int32), jnp.full((1,), n_tiles, jnp.int32), tm)
            xs_new = _ln2(x1, f, mod, ln2_g, ln2_b, l, alpha, n_ctx_blocks, row_off)
        else:
            x1, h, idx, gates = _mix_out((pa, pb, pcv, pd), w_out_bf, xs, mod, ln1_g, ln1_b, l, alpha,
                                         n_ctx_blocks, router_w[l // 2])
            n_exp = router_w.shape[-1]
            row0 = row_off * ROW_BLOCK
            tok_of_slot, tile_expert, n_used, slot_a, slot_b = _route_slots(idx, row0, n_exp, FFN_TM)
            hs = _gather_rows(h, tok_of_slot, FFN_TM)
            y = _swiglu_tiles(hs, moe_w1[l // 2], moe_w3[l // 2], moe_w2[l // 2], tile_expert, n_used, FFN_TM)
            xs_new = _ln2(x1, y, mod, ln2_g, ln2_b, l, alpha, n_ctx_blocks, row_off, moe=(slot_a, slot_b, gates))
        xs = xs_new
    return xs[None]
```

```python
import functools
import math

import numpy as np
import jax
import jax.numpy as jnp
from jax import lax
from jax.experimental import pallas as pl
from jax.experimental.pallas import tpu as pltpu

F32 = jnp.float32
BF16 = jnp.bfloat16

GRID_W = 64
N_GROUPS = 4
POOL_WINDOWS = (2, 4, 8, 16)
DIFF_HEADS = 8
DIFF_QK = 32
CONV_WIDTH = 31
S5_P = 16
S5_N = 64
TOP_K = 2
ROPE_BASE = 10000.0
LN_EPS = 1e-5

LANES = 128
SUBLANES = 8
ROW_BLOCK = 256
VMEM_LIMIT = 56 * 1024 * 1024


def _cparams(n_axes, vmem=VMEM_LIMIT):
    return pltpu.CompilerParams(dimension_semantics=("arbitrary",) * n_axes, vmem_limit_bytes=vmem)


def _layer_norm(y, g, b):
    mu = jnp.mean(y, -1, keepdims=True)
    yc = y - mu
    var = jnp.mean(yc * yc, -1, keepdims=True)
    return yc * lax.rsqrt(var + LN_EPS) * g + b


def _silu(x):
    return x * jax.nn.sigmoid(x)


def _mod_kernel(cc_ref, w_ref, b_ref, o_ref):
    a = _silu(cc_ref[...])
    o_ref[...] = jnp.dot(a.astype(BF16), w_ref[...].astype(BF16), preferred_element_type=F32) + b_ref[...]


def _modulation(cc, w_mod, b_mod):
    L, D, N = w_mod.shape
    tn = 1536
    assert N % tn == 0
    return pl.pallas_call(
        _mod_kernel,
        grid=(L, N // tn),
        in_specs=[pl.BlockSpec((SUBLANES, D), lambda l, j: (0, 0)),
                  pl.BlockSpec((None, D, tn), lambda l, j: (l, 0, j)),
                  pl.BlockSpec((None, 1, tn), lambda l, j: (l, 0, j))],
        out_specs=pl.BlockSpec((None, SUBLANES, tn), lambda l, j: (l, 0, j)),
        out_shape=jax.ShapeDtypeStruct((L, SUBLANES, N), F32),
        compiler_params=_cparams(2),
    )(cc, w_mod, b_mod.reshape(L, 1, N))


def _mod_spec(layer, chunk, D, n_grid_axes):
    if n_grid_axes == 1:
        return pl.BlockSpec((None, SUBLANES, D), lambda i: (layer, 0, chunk))
    return pl.BlockSpec((None, SUBLANES, D), lambda j, i: (layer, 0, chunk))


def _pick(m, is_ctx):
    return jnp.where(is_ctx, m[1:2, :], m[0:1, :])


def _inproj_kernel(x_ref, sh_ref, sc_ref, w_ref, o_ref, wb_ref, *, n_ctx_blocks):
    i = pl.program_id(1)

    @pl.when(i == 0)
    def _():
        wb_ref[...] = w_ref[...].astype(BF16)

    is_ctx = i < n_ctx_blocks
    h = x_ref[...] * (1.0 + _pick(sc_ref[...], is_ctx)) + _pick(sh_ref[...], is_ctx)
    o_ref[...] = jnp.dot(h.astype(BF16), wb_ref[...], preferred_element_type=F32).astype(o_ref.dtype)


def _in_projection(xs, mod, w_in, layer, n_ctx_blocks):
    S, D = xs.shape
    N = w_in.shape[-1]
    tn = 1792
    assert N % tn == 0 and S % ROW_BLOCK == 0
    return pl.pallas_call(
        functools.partial(_inproj_kernel, n_ctx_blocks=n_ctx_blocks),
        grid=(N // tn, S // ROW_BLOCK),
        in_specs=[pl.BlockSpec((ROW_BLOCK, D), lambda j, i: (i, 0)),
                  _mod_spec(layer, 0, D, 2),
                  _mod_spec(layer, 1, D, 2),
                  pl.BlockSpec((None, D, tn), lambda j, i: (layer, 0, j))],
        out_specs=pl.BlockSpec((ROW_BLOCK, tn), lambda j, i: (i, j)),
        out_shape=jax.ShapeDtypeStruct((S, N), BF16),
        scratch_shapes=[pltpu.VMEM((D, tn), BF16)],
        compiler_params=_cparams(2),
    )(xs, mod, mod, w_in)


def _seq_edges(i, n_blocks, n_ctx_blocks):
    prev_ok = jnp.logical_and(i != 0, i != n_ctx_blocks)
    next_ok = jnp.logical_and(i != n_ctx_blocks - 1, i != n_blocks - 1)
    return prev_ok, next_ok


def _pool_kernel(p_ref, c_ref, n_ref, w_ref, scale_ref, o_ref, ext_ref, *, n_blocks, n_ctx_blocks):
    i = pl.program_id(0)
    R = ROW_BLOCK
    halo = SUBLANES
    prev_ok, next_ok = _seq_edges(i, n_blocks, n_ctx_blocks)
    cur = c_ref[...].astype(F32)
    pack = 2 * SUBLANES
    ext_ref[0:halo, :] = jnp.where(prev_ok, p_ref[R - pack:R, :].astype(F32)[pack - halo:], 0.0)
    ext_ref[halo:halo + R, :] = cur
    ext_ref[halo + R:halo + R + halo, :] = jnp.where(next_ok, n_ref[0:pack, :].astype(F32)[:halo], 0.0)
    rloc = lax.broadcasted_iota(jnp.int32, (R, 1), 0)
    gw = cur.shape[1] // len(POOL_WINDOWS)
    outs = []
    for g, w in enumerate(POOL_WINDOWS):
        acc = jnp.zeros((R, gw), F32)
        cnt = jnp.zeros((R, 1), F32)
        for d in range(-(w // 2), w - w // 2):
            acc = acc + ext_ref[halo + d:halo + d + R, g * gw:(g + 1) * gw]
            valid = jnp.logical_and(jnp.logical_or(rloc + d >= 0, prev_ok),
                                    jnp.logical_or(rloc + d < R, next_ok))
            cnt = cnt + valid.astype(F32)
        diff = acc / cnt - cur[:, g * gw:(g + 1) * gw]
        outs.append(jnp.dot(diff.astype(BF16), w_ref[g].astype(BF16), preferred_element_type=F32))
    o_ref[...] = jnp.concatenate(outs, axis=-1) * scale_ref[...]


def _pool_mixer(z, pool_w, pool_scale, layer, n_ctx_blocks):
    S = z.shape[0]
    nb = S // ROW_BLOCK
    GW = pool_scale.shape[-1]
    G, gw = pool_w.shape[1], pool_w.shape[2]
    return pl.pallas_call(
        functools.partial(_pool_kernel, n_blocks=nb, n_ctx_blocks=n_ctx_blocks),
        grid=(nb,),
        in_specs=[pl.BlockSpec((ROW_BLOCK, GW), lambda i: (jnp.maximum(i - 1, 0), 0)),
                  pl.BlockSpec((ROW_BLOCK, GW), lambda i: (i, 0)),
                  pl.BlockSpec((ROW_BLOCK, GW), lambda i: (jnp.minimum(i + 1, nb - 1), 0)),
                  pl.BlockSpec((None, G, gw, gw), lambda i: (layer, 0, 0, 0)),
                  pl.BlockSpec((None, 1, GW), lambda i: (layer, 0, 0))],
        out_specs=pl.BlockSpec((ROW_BLOCK, GW), lambda i: (i, 0)),
        out_shape=jax.ShapeDtypeStruct((S, GW), F32),
        scratch_shapes=[pltpu.VMEM((ROW_BLOCK + 2 * SUBLANES, GW), F32)],
        compiler_params=_cparams(1),
    )(z, z, z, pool_w, pool_scale.reshape(pool_scale.shape[0], 1, GW))


CONV_HALO = 16


def _conv_kernel(p_ref, c_ref, n_ref, dw_ref, db_ref, g_ref, b_ref, pw_ref, o_ref, ext_ref,
                 *, n_blocks, n_ctx_blocks):
    i = pl.program_id(0)
    R = ROW_BLOCK
    H = CONV_HALO
    GW = o_ref.shape[1]
    prev_ok, next_ok = _seq_edges(i, n_blocks, n_ctx_blocks)

    def glu(u):
        u = u.astype(F32)
        return u[:, :GW] * jax.nn.sigmoid(u[:, GW:])

    ext_ref[0:H, :] = jnp.where(prev_ok, glu(p_ref[R - H:R, :]), 0.0)
    ext_ref[H:H + R, :] = glu(c_ref[...])
    ext_ref[H + R:H + R + H, :] = jnp.where(next_ok, glu(n_ref[0:H, :]), 0.0)
    off = H - CONV_WIDTH // 2
    acc = jnp.zeros((R, GW), F32)
    for j in range(CONV_WIDTH):
        acc = acc + ext_ref[off + j:off + j + R, :] * dw_ref[j:j + 1, :]
    y = _layer_norm(acc + db_ref[...], g_ref[...], b_ref[...])
    o_ref[...] = jnp.dot(_silu(y).astype(BF16), pw_ref[...].astype(BF16), preferred_element_type=F32)


def _conv_mixer(z, col_block, conv_dw, conv_db, conv_ln_g, conv_ln_b, conv_pw, layer, n_ctx_blocks):
    S = z.shape[0]
    nb = S // ROW_BLOCK
    GW = conv_db.shape[-1]
    L = conv_db.shape[0]
    vec = lambda a: a.reshape(L, 1, GW)
    vspec = pl.BlockSpec((None, 1, GW), lambda i: (layer, 0, 0))
    return pl.pallas_call(
        functools.partial(_conv_kernel, n_blocks=nb, n_ctx_blocks=n_ctx_blocks),
        grid=(nb,),
        in_specs=[pl.BlockSpec((ROW_BLOCK, 2 * GW), lambda i: (jnp.maximum(i - 1, 0), col_block)),
                  pl.BlockSpec((ROW_BLOCK, 2 * GW), lambda i: (i, col_block)),
                  pl.BlockSpec((ROW_BLOCK, 2 * GW), lambda i: (jnp.minimum(i + 1, nb - 1), col_block)),
                  pl.BlockSpec((None, CONV_WIDTH, GW), lambda i: (layer, 0, 0)),
                  vspec, vspec, vspec,
                  pl.BlockSpec((None, GW, GW), lambda i: (layer, 0, 0))],
        out_specs=pl.BlockSpec((ROW_BLOCK, GW), lambda i: (i, 0)),
        out_shape=jax.ShapeDtypeStruct((S, GW), F32),
        scratch_shapes=[pltpu.VMEM((ROW_BLOCK + 2 * CONV_HALO, GW), F32)],
        compiler_params=_cparams(1),
    )(z, z, z, conv_dw, vec(conv_db), vec(conv_ln_g), vec(conv_ln_b), conv_pw)


def _rope_tables(T, Tc, width):
    ax = DIFF_QK // 2
    inv = ROPE_BASE ** (-jnp.arange(0, ax, 2, dtype=F32) / ax)
    t = jnp.arange(T)
    row = (t // GRID_W).astype(F32)
    col = (t % GRID_W).astype(F32)
    ang = jnp.stack([row[:, None] * inv, col[:, None] * inv], axis=1)
    cos = jnp.cos(ang)[:, :, None, :]
    sin = jnp.sin(ang)[:, :, None, :]
    cos = jnp.broadcast_to(cos, (T, 2, 2, ax // 2)).reshape(T, DIFF_QK)
    sin = jnp.concatenate([-sin, sin], axis=2).reshape(T, DIFF_QK)
    reps = width // DIFF_QK
    cos = jnp.concatenate([jnp.ones((Tc, DIFF_QK), F32), cos], axis=0)
    sin = jnp.concatenate([jnp.zeros((Tc, DIFF_QK), F32), sin], axis=0)
    return jnp.tile(cos, (1, reps)), jnp.tile(sin, (1, reps))


def _qkv_prep_kernel(q_ref, k_ref, v_ref, cos_ref, sin_ref, qo_ref, ko_ref, vo_ref):
    W = q_ref.shape[1]
    half = DIFF_QK // 4
    lane = lax.broadcasted_iota(jnp.int32, (1, W), 1)
    first = (lane % (2 * half)) < half
    cos = cos_ref[...]
    sin = sin_ref[...]

    def rope(x):
        partner = jnp.where(first, pltpu.roll(x, W - half, 1), pltpu.roll(x, half, 1))
        return x * cos + partner * sin

    qo_ref[...] = (rope(q_ref[...].astype(F32)) * (DIFF_QK ** -0.5 * math.log2(math.e))).T.astype(BF16)
    ko_ref[...] = rope(k_ref[...].astype(F32)).astype(BF16)
    vt = v_ref[...].astype(F32).T.astype(BF16)
    dv = LANES // 2
    ones = jnp.ones((dv, vt.shape[1]), BF16)
    for h in range(W // dv):
        vo_ref[2 * h * dv:(2 * h + 1) * dv, :] = vt[h * dv:(h + 1) * dv, :]
        vo_ref[(2 * h + 1) * dv:(2 * h + 2) * dv, :] = ones


def _qkv_prep(z, cos, sin, q_blk, k_blk, v_blk):
    S = z.shape[0]
    W = cos.shape[1]
    nb = S // ROW_BLOCK
    row = lambda c: pl.BlockSpec((ROW_BLOCK, W), lambda i: (i, c))
    return pl.pallas_call(
        _qkv_prep_kernel,
        grid=(nb,),
        in_specs=[row(q_blk), row(k_blk), row(v_blk), row(0), row(0)],
        out_specs=[pl.BlockSpec((W, ROW_BLOCK), lambda i: (0, i)), row(0),
                   pl.BlockSpec((2 * W, ROW_BLOCK), lambda i: (0, i))],
        out_shape=[jax.ShapeDtypeStruct((W, S), BF16), jax.ShapeDtypeStruct((S, W), BF16),
                   jax.ShapeDtypeStruct((2 * W, S), BF16)],
        compiler_params=_cparams(1),
    )(z, z, z, cos, sin)


ATTN_TQ = 256
ATTN_TK = (4096, 2048, 1024)


def _attn_kernel(qt_ref, k_ref, vt_ref, lam_ref, g_ref, o_ref, qq_ref, m_ref, acc_ref,
                 *, n_ctx, n_ctx_blocks, n_lat_chunks, tk):
    i = pl.program_id(1)
    tq = ATTN_TQ
    dv = LANES // 2
    qt = qt_ref[...]
    feat = lax.broadcasted_iota(jnp.int32, (LANES, 1), 0)
    zero = jnp.zeros_like(qt)
    for hh in range(2):
        for comp in range(2):
            lo = hh * dv + comp * DIFF_QK
            keep = jnp.logical_and(feat >= lo, feat < lo + DIFF_QK)
            qq_ref[hh, :, comp * tq:(comp + 1) * tq] = jnp.where(keep, qt, zero)

    def attend(start, size, first):
        kk = k_ref[pl.ds(start, size), :]
        for hh in range(2):
            s = jnp.dot(kk, qq_ref[hh], preferred_element_type=F32)
            vv = vt_ref[hh * LANES:(hh + 1) * LANES, pl.ds(start, size)]
            mx = jnp.max(s, axis=0, keepdims=True)
            if first:
                m_ref[hh] = mx
                p = jnp.exp2(s - mx)
                acc_ref[hh] = jnp.dot(vv, p.astype(BF16), preferred_element_type=F32)
            else:
                m_old = m_ref[hh]
                m_new = jnp.maximum(m_old, mx)
                m_ref[hh] = m_new
                p = jnp.exp2(s - m_new)
                acc_ref[hh] = (jnp.exp2(m_old - m_new) * acc_ref[hh]
                               + jnp.dot(vv, p.astype(BF16), preferred_element_type=F32))

    attend(0, n_ctx, True)
    n_steps = jnp.where(i < n_ctx_blocks, 0, n_lat_chunks)

    def body(c, carry):
        attend(pl.multiple_of(n_ctx + c * tk, LANES), tk, False)
        return carry

    lax.fori_loop(0, n_steps, body, 0)

    lam = lam_ref[...]
    outs = []
    for hh in range(2):
        acc = acc_ref[hh]
        ratio = acc[:dv] / acc[dv:dv + 1]
        o = ratio[:, :tq] - lam * ratio[:, tq:]
        r = lax.rsqrt(jnp.sum(o * o, axis=0, keepdims=True) / dv + LN_EPS)
        outs.append(o * r)
    o_ref[...] = (jnp.concatenate(outs, axis=0) * g_ref[...]).T


def _diff_attention(qt, k, vt, lam, g, n_ctx):
    W, S = qt.shape
    assert S % ATTN_TQ == 0 and n_ctx % ATTN_TQ == 0
    tk = next(t for t in ATTN_TK if (S - n_ctx) % t == 0)
    nq = S // ATTN_TQ
    return pl.pallas_call(
        functools.partial(_attn_kernel, n_ctx=n_ctx, n_ctx_blocks=n_ctx // ATTN_TQ,
                          n_lat_chunks=(S - n_ctx) // tk, tk=tk),
        grid=(W // LANES, nq),
        in_specs=[pl.BlockSpec((LANES, ATTN_TQ), lambda p, i: (p, i)),
                  pl.BlockSpec((S, LANES), lambda p, i: (0, p)),
                  pl.BlockSpec((2 * LANES, S), lambda p, i: (p, 0)),
                  pl.BlockSpec((1, ATTN_TQ), lambda p, i: (0, 0)),
                  pl.BlockSpec((LANES, 1), lambda p, i: (p, 0))],
        out_specs=pl.BlockSpec((ATTN_TQ, LANES), lambda p, i: (i, p)),
        out_shape=jax.ShapeDtypeStruct((S, W), F32),
        scratch_shapes=[pltpu.VMEM((2, LANES, 2 * ATTN_TQ), BF16),
                        pltpu.VMEM((2, 1, 2 * ATTN_TQ), F32),
                        pltpu.VMEM((2, LANES, 2 * ATTN_TQ), F32)],
        compiler_params=_cparams(2),
    )(qt, k, vt, lam, g)


S5_SEGS = SUBLANES
S5_KB = 32
S5_GB = 8


def _s5_params(a_re, a_im, log_dt, b_re, b_im, c_re, c_im, seg_len):
    G, N = a_re.shape
    P = b_re.shape[-1]
    nblk = G // S5_GB
    a_re, a_im = a_re.astype(F32), a_im.astype(F32)
    dt = jnp.exp(log_dt.astype(F32))[:, None]
    lr, li = dt * a_re, dt * a_im
    mag = jnp.exp(lr)
    ar, ai = mag * jnp.cos(li), mag * jnp.sin(li)
    den = a_re * a_re + a_im * a_im
    qr = ((ar - 1.0) * a_re + ai * a_im) / den
    qi = (ai * a_re - (ar - 1.0) * a_im) / den
    b_re, b_im = b_re.astype(F32), b_im.astype(F32)
    br = qr[..., None] * b_re - qi[..., None] * b_im
    bi = qr[..., None] * b_im + qi[..., None] * b_re
    mag_l = jnp.exp(seg_len * lr)
    alr, ali = mag_l * jnp.cos(seg_len * li), mag_l * jnp.sin(seg_len * li)
    eye = jnp.eye(S5_GB, dtype=F32)
    wb = lambda m: jnp.einsum('gh,bgnp->bgphn', eye, m.reshape(nblk, S5_GB, N, P)).reshape(
        nblk, S5_GB * P, S5_GB * N)
    w_in = jnp.concatenate([wb(br), wb(bi)], axis=2)
    cm = lambda m: jnp.einsum('gh,bgpn->bhngp', eye, m.astype(F32).reshape(nblk, S5_GB, P, N)).reshape(
        nblk, S5_GB * N, S5_GB * P)
    w_out = jnp.concatenate([cm(c_re), -cm(c_im)], axis=1)
    row = lambda r, i: jnp.concatenate([r.reshape(1, G * N), i.reshape(1, G * N)], axis=1)
    coef = jnp.broadcast_to(row(ar, ai), (S5_SEGS, 2 * G * N))
    return w_in.astype(BF16), coef, row(alr, ali), w_out.astype(BF16)


def _s5_kernel(*refs, emit_out):
    n_seg = S5_SEGS
    uf_ref = refs[0]
    ur_refs = refs[1:1 + n_seg]
    rest = refs[1 + n_seg:]
    if emit_out:
        (wbf_ref, wbr_ref, af_ref, ar_ref, ef_ref, er_ref, alf_ref, alr_ref, cf_ref, cr_ref,
         yf_ref, yr_ref, stage_ref, bf_ref, br_ref, hf_ref, hr_ref) = rest
    else:
        (wbf_ref, wbr_ref, af_ref, ar_ref, ef_out_ref, er_out_ref,
         stage_ref, bf_ref, br_ref, hf_ref, hr_ref) = rest
    g = pl.program_id(0)
    KB = S5_KB
    R = n_seg * KB
    NS = af_ref.shape[1] // 2
    nblk = wbf_ref.shape[0]
    wi = wbf_ref.shape[1]
    ws = wbf_ref.shape[2] // 2

    def cmul_add(a_row, h, add):
        are, aim = a_row[:, :NS], a_row[:, NS:]
        hre, him = h[:, :NS], h[:, NS:]
        return jnp.concatenate([are * hre - aim * him + add[:, :NS], are * him + aim * hre + add[:, NS:]], axis=1)

    @pl.when(g == 0)
    def _():
        if emit_out:
            def chain(e_ref, al_ref, order):
                al = al_ref[...]
                c = jnp.zeros((1, 2 * NS), F32)
                rows = [None] * n_seg
                for s in order:
                    rows[s] = c
                    c = cmul_add(al, c, e_ref[s:s + 1, :])
                return jnp.concatenate(rows, axis=0)
            hf_ref[...] = chain(ef_ref, alf_ref, range(n_seg))
            hr_ref[...] = chain(er_ref, alr_ref, range(n_seg - 1, -1, -1))
        else:
            hf_ref[...] = jnp.zeros_like(hf_ref)
            hr_ref[...] = jnp.zeros_like(hr_ref)

    def interleaved(load_seg):
        for s in range(n_seg):
            blk = load_seg(s).astype(F32)
            for c in range(nblk):
                stage_ref[c, s * KB:(s + 1) * KB, :] = blk[:, c * wi:(c + 1) * wi]
        rows = [jnp.concatenate([stage_ref[c, pl.ds(kk, n_seg, stride=KB), :] for c in range(nblk)], axis=1)
                for kk in range(KB)]
        return jnp.concatenate(rows, axis=0).astype(BF16)

    def project_in(u, w_ref, buf_ref):
        for b in range(nblk):
            res = jnp.dot(u[:, b * wi:(b + 1) * wi], w_ref[b], preferred_element_type=F32)
            buf_ref[:, b * ws:(b + 1) * ws] = res[:, :ws]
            buf_ref[:, NS + b * ws:NS + (b + 1) * ws] = res[:, ws:]

    def scan(buf_ref, a_ref, h_ref, reverse):
        def step(t, h):
            kk = (KB - 1 - t) if reverse else t
            r0 = pl.multiple_of(kk * n_seg, n_seg)
            new = cmul_add(a_ref[...], h, buf_ref[pl.ds(r0, n_seg), :])
            if emit_out:
                buf_ref[pl.ds(r0, n_seg), :] = new
            return new
        h_ref[...] = lax.fori_loop(0, KB, step, h_ref[...])

    def project_out(buf_ref, c_ref, y_ref):
        for b in range(nblk):
            hcat = jnp.concatenate([buf_ref[:, b * ws:(b + 1) * ws], buf_ref[:, NS + b * ws:NS + (b + 1) * ws]],
                                   axis=1).astype(BF16)
            stage_ref[b] = jnp.dot(hcat, c_ref[b], preferred_element_type=F32)
        for s in range(n_seg):
            y_ref[s] = jnp.concatenate([stage_ref[c, pl.ds(s, KB, stride=n_seg), :] for c in range(nblk)], axis=1)

    project_in(interleaved(lambda s: uf_ref[s]), wbf_ref, bf_ref)
    scan(bf_ref, af_ref, hf_ref, False)
    if emit_out:
        project_out(bf_ref, cf_ref, yf_ref)
    project_in(interleaved(lambda s: ur_refs[s][...]), wbr_ref, br_ref)
    scan(br_ref, ar_ref, hr_ref, True)
    if emit_out:
        project_out(br_ref, cr_ref, yr_ref)
    else:
        ef_out_ref[...] = hf_ref[...]
        er_out_ref[...] = hr_ref[...]


def _s5_pass(z, col_block, pf, pr, ends, n_ctx):
    S, NZ = z.shape
    GW = pf[0].shape[1] * pf[0].shape[0]
    NS2 = pf[1].shape[1]
    seg_len = S // S5_SEGS
    steps = seg_len // S5_KB
    nblocks = S // S5_KB
    assert S % (S5_SEGS * S5_KB) == 0 and n_ctx % S5_KB == 0
    ctx_blocks = n_ctx // S5_KB
    emit_out = ends is not None
    z4 = z.reshape(S5_SEGS, steps, S5_KB, NZ)
    z3 = z.reshape(nblocks, S5_KB, NZ)

    def rev_spec(s):
        return pl.BlockSpec((None, S5_KB, GW),
                            lambda g: ((s * steps + steps - 1 - g + ctx_blocks) % nblocks, 0, col_block))

    const = lambda a: pl.BlockSpec(a.shape, lambda g: (0,) * a.ndim)
    in_specs = [pl.BlockSpec((S5_SEGS, None, S5_KB, GW), lambda g: (0, g, 0, col_block))]
    in_specs += [rev_spec(s) for s in range(S5_SEGS)]
    args = [z4] + [z3] * S5_SEGS
    weights = [pf[0], pr[0], pf[1], pr[1]]
    if emit_out:
        weights += [ends[0], ends[1], pf[2], pr[2], pf[3], pr[3]]
    in_specs += [const(a) for a in weights]
    args += weights
    scratch = [pltpu.VMEM((pf[0].shape[0], S5_SEGS * S5_KB, pf[0].shape[1]), F32),
               pltpu.VMEM((S5_SEGS * S5_KB, NS2), F32), pltpu.VMEM((S5_SEGS * S5_KB, NS2), F32),
               pltpu.VMEM((S5_SEGS, NS2), F32), pltpu.VMEM((S5_SEGS, NS2), F32)]
    if emit_out:
        yshape = jax.ShapeDtypeStruct((S5_SEGS, steps, S5_KB, GW), F32)
        out_shape = [yshape, yshape]
        out_specs = [pl.BlockSpec((S5_SEGS, None, S5_KB, GW), lambda g: (0, g, 0, 0)),
                     pl.BlockSpec((S5_SEGS, None, S5_KB, GW), lambda g: (0, steps - 1 - g, 0, 0))]
    else:
        eshape = jax.ShapeDtypeStruct((S5_SEGS, NS2), F32)
        out_shape = [eshape, eshape]
        out_specs = [pl.BlockSpec((S5_SEGS, NS2), lambda g: (0, 0))] * 2
    return pl.pallas_call(
        functools.partial(_s5_kernel, emit_out=emit_out),
        grid=(steps,),
        in_specs=in_specs,
        out_specs=out_specs,
        out_shape=out_shape,
        scratch_shapes=scratch,
        compiler_params=_cparams(1),
    )(*args)


def _s5_scan(z, col_block, pf, pr, n_ctx):
    S = z.shape[0]
    GW = pf[0].shape[1] * pf[0].shape[0]
    ends = _s5_pass(z, col_block, pf, pr, None, n_ctx)
    yf, yr = _s5_pass(z, col_block, pf, pr, ends, n_ctx)
    return yf.reshape(S, GW), yr.reshape(S, GW)


def _s5_glu_kernel(yf_ref, yr_ref, u_ref, d_ref, w_ref, b_ref, o_ref):
    y = yf_ref[...] + yr_ref[...] + d_ref[...] * u_ref[...].astype(F32)
    zz = jax.nn.gelu(y)
    gate = jnp.dot(zz.astype(BF16), w_ref[...].astype(BF16), preferred_element_type=F32) + b_ref[...]
    o_ref[...] = zz * jax.nn.sigmoid(gate)


def _s5_glu(yf, yr, z, col_block, s5_d, glu_w, glu_b, layer, n_ctx_blocks):
    S, GW = yf.shape
    L = s5_d.shape[0]
    nb = S // ROW_BLOCK
    row = lambda c: pl.BlockSpec((ROW_BLOCK, GW), lambda i: (i, c))
    rot = pl.BlockSpec((ROW_BLOCK, GW), lambda i: ((i + nb - n_ctx_blocks) % nb, 0))
    vspec = pl.BlockSpec((None, 1, GW), lambda i: (layer, 0, 0))
    return pl.pallas_call(
        _s5_glu_kernel,
        grid=(nb,),
        in_specs=[row(0), rot, row(col_block), vspec,
                  pl.BlockSpec((None, GW, GW), lambda i: (layer, 0, 0)), vspec],
        out_specs=row(0),
        out_shape=jax.ShapeDtypeStruct((S, GW), F32),
        compiler_params=_cparams(1),
    )(yf, yr, z, s5_d.reshape(L, 1, GW), glu_w, glu_b.reshape(L, 1, GW))


def _cast_kernel(x_ref, o_ref):
    o_ref[...] = x_ref[...].astype(o_ref.dtype)


def _cast_bf16(w, layer):
    _, K, N = w.shape
    tk = 512
    return pl.pallas_call(
        _cast_kernel,
        grid=(K // tk,),
        in_specs=[pl.BlockSpec((None, tk, N), lambda i: (layer, i, 0))],
        out_specs=pl.BlockSpec((tk, N), lambda i: (i, 0)),
        out_shape=jax.ShapeDtypeStruct((K, N), BF16),
        compiler_params=_cparams(1),
    )(w)


def _mixout_kernel(*refs, alpha, n_ctx_blocks, route, h_dtype):
    (pa_ref, pb_ref, pc_ref, pd_ref, w_ref, x_ref, g1_ref, lg_ref, lb_ref, sh_ref, sc_ref) = refs[:11]
    if route:
        rw_ref, x1_ref, h_ref, idx_ref, gate_ref = refs[11:]
    else:
        x1_ref, h_ref = refs[11:]
    i = pl.program_id(0)
    is_ctx = i < n_ctx_blocks
    GW = pa_ref.shape[1]
    mix = jnp.zeros(x_ref.shape, F32)
    for k, p_ref in enumerate((pa_ref, pb_ref, pc_ref, pd_ref)):
        mix = mix + jnp.dot(p_ref[...].astype(BF16), w_ref[k * GW:(k + 1) * GW, :],
                            preferred_element_type=F32)
    y = alpha * x_ref[...] + _pick(g1_ref[...], is_ctx) * mix
    x1 = _layer_norm(y, lg_ref[...], lb_ref[...])
    x1_ref[...] = x1
    h = x1 * (1.0 + _pick(sc_ref[...], is_ctx)) + _pick(sh_ref[...], is_ctx)
    h_ref[...] = h.astype(h_dtype)
    if route:
        rw = rw_ref[...]
        h_hi = h.astype(BF16)
        h_lo = (h - h_hi.astype(F32)).astype(BF16)
        w_hi = rw.astype(BF16)
        w_lo = (rw - w_hi.astype(F32)).astype(BF16)
        logits = (jnp.dot(h_hi, w_hi, preferred_element_type=F32)
                  + (jnp.dot(h_lo, w_hi, preferred_element_type=F32)
                     + jnp.dot(h_hi, w_lo, preferred_element_type=F32)))
        n_exp = rw_ref.shape[1]
        lane = lax.broadcasted_iota(jnp.int32, logits.shape, 1)
        m1 = jnp.max(logits, axis=-1, keepdims=True)
        i1 = jnp.min(jnp.where(logits == m1, lane, n_exp), axis=-1, keepdims=True)
        rest = jnp.where(lane == i1, -jnp.inf, logits)
        m2 = jnp.max(rest, axis=-1, keepdims=True)
        i2 = jnp.min(jnp.where(rest == m2, lane, n_exp), axis=-1, keepdims=True)
        e2 = jnp.exp(m2 - m1)
        idx_ref[...] = jnp.concatenate([i1, i2], axis=1)
        gate_ref[...] = jnp.concatenate([1.0 / (1.0 + e2), e2 / (1.0 + e2)], axis=1)


def _mix_out(parts, w_out_bf, xs, mod, ln_g, ln_b, layer, alpha, n_ctx_blocks, router_w):
    S, D = xs.shape
    GW = parts[0].shape[1]
    L = ln_g.shape[0]
    route = router_w is not None
    h_dtype = F32 if route else BF16
    part = pl.BlockSpec((ROW_BLOCK, GW), lambda i: (i, 0))
    rows = pl.BlockSpec((ROW_BLOCK, D), lambda i: (i, 0))
    vspec = pl.BlockSpec((None, 1, D), lambda i: (layer, 0, 0))
    in_specs = [part, part, part, part,
                pl.BlockSpec((D, D), lambda i: (0, 0)), rows,
                _mod_spec(layer, 2, D, 1), vspec, vspec, _mod_spec(layer, 3, D, 1), _mod_spec(layer, 4, D, 1)]
    args = list(parts) + [w_out_bf, xs, mod, ln_g.reshape(L, 1, D), ln_b.reshape(L, 1, D), mod, mod]
    out_specs = [rows, rows]
    out_shape = [jax.ShapeDtypeStruct((S, D), F32), jax.ShapeDtypeStruct((S, D), h_dtype)]
    if route:
        E = router_w.shape[-1]
        in_specs.append(pl.BlockSpec((D, E), lambda i: (0, 0)))
        args.append(router_w)
        out_specs += [pl.BlockSpec((ROW_BLOCK, TOP_K), lambda i: (i, 0))] * 2
        out_shape += [jax.ShapeDtypeStruct((S, TOP_K), jnp.int32), jax.ShapeDtypeStruct((S, TOP_K), F32)]
    return pl.pallas_call(
        functools.partial(_mixout_kernel, alpha=alpha, n_ctx_blocks=n_ctx_blocks, route=route, h_dtype=h_dtype),
        grid=(S // ROW_BLOCK,),
        in_specs=in_specs,
        out_specs=out_specs,
        out_shape=out_shape,
        compiler_params=_cparams(1),
    )(*args)


FFN_TM = 512
FFN_TF = 512
FFN_TN = 512


def _expert_changed(te_ref, i):
    prev = te_ref[jnp.maximum(i - 1, 0)]
    return jnp.logical_or(i == 0, te_ref[i] != prev)


def _ffn_up_kernel(te_ref, nu_ref, h_ref, w1_ref, w3_ref, o_ref, w1b_ref, w3b_ref):
    i = pl.program_id(1)

    @pl.when(_expert_changed(te_ref, i))
    def _():
        w1b_ref[...] = w1_ref[...].astype(BF16)
        w3b_ref[...] = w3_ref[...].astype(BF16)

    @pl.when(i < nu_ref[0])
    def _():
        h = h_ref[...]
        a = jnp.dot(h, w1b_ref[...], preferred_element_type=F32)
        b = jnp.dot(h, w3b_ref[...], preferred_element_type=F32)
        o_ref[...] = (_silu(a) * b).astype(o_ref.dtype)

    @pl.when(i >= nu_ref[0])
    def _():
        o_ref[...] = jnp.zeros_like(o_ref)


def _ffn_down_kernel(te_ref, nu_ref, g_ref, w2_ref, o_ref, w2b_ref):
    i = pl.program_id(1)

    @pl.when(_expert_changed(te_ref, i))
    def _():
        w2b_ref[...] = w2_ref[...].astype(BF16)

    @pl.when(i < nu_ref[0])
    def _():
        o_ref[...] = jnp.dot(g_ref[...], w2b_ref[...], preferred_element_type=F32)

    @pl.when(i >= nu_ref[0])
    def _():
        o_ref[...] = jnp.zeros_like(o_ref)


def _swiglu_tiles(hs, w1, w3, w2, tile_expert, n_used, tm):
    R, D = hs.shape
    _, _, F = w1.shape
    n_tiles = R // tm
    tf = FFN_TF if F % FFN_TF == 0 else F
    tn = FFN_TN
    assert R % tm == 0 and F % tf == 0 and D % tn == 0
    g = pl.pallas_call(
        _ffn_up_kernel,
        grid_spec=pltpu.PrefetchScalarGridSpec(
            num_scalar_prefetch=2,
            grid=(F // tf, n_tiles),
            in_specs=[pl.BlockSpec((tm, D), lambda j, i, te, nu: (i, 0)),
                      pl.BlockSpec((None, D, tf), lambda j, i, te, nu: (te[i], 0, j)),
                      pl.BlockSpec((None, D, tf), lambda j, i, te, nu: (te[i], 0, j))],
            out_specs=pl.BlockSpec((tm, tf), lambda j, i, te, nu: (i, j)),
            scratch_shapes=[pltpu.VMEM((D, tf), BF16), pltpu.VMEM((D, tf), BF16)]),
        out_shape=jax.ShapeDtypeStruct((R, F), BF16),
        compiler_params=_cparams(2),
    )(tile_expert, n_used, hs, w1, w3)
    return pl.pallas_call(
        _ffn_down_kernel,
        grid_spec=pltpu.PrefetchScalarGridSpec(
            num_scalar_prefetch=2,
            grid=(D // tn, n_tiles),
            in_specs=[pl.BlockSpec((tm, F), lambda j, i, te, nu: (i, 0)),
                      pl.BlockSpec((None, F, tn), lambda j, i, te, nu: (te[i], 0, j))],
            out_specs=pl.BlockSpec((tm, tn), lambda j, i, te, nu: (i, j)),
            scratch_shapes=[pltpu.VMEM((F, tn), BF16)]),
        out_shape=jax.ShapeDtypeStruct((R, D), F32),
        compiler_params=_cparams(2),
    )(tile_expert, n_used, g, w2)


def _row_copy(src_ref, dst_ref, src_row, dst_row, sem):
    return pltpu.make_async_copy(src_ref.at[pl.ds(src_row, 1)], dst_ref.at[pl.ds(dst_row, 1)], sem)


DMA_UNROLL = 8


def _gather_kernel(tok_ref, nu_ref, src_ref, o_ref, buf_ref, sem):
    i = pl.program_id(0)
    tm = buf_ref.shape[0]

    def issue(r, c):
        _row_copy(src_ref, buf_ref, tok_ref[i * tm + r], r, sem).start()
        return c

    def drain(r, c):
        _row_copy(src_ref, buf_ref, 0, r, sem).wait()
        return c

    @pl.when(i < nu_ref[0])
    def _():
        lax.fori_loop(0, tm, issue, 0, unroll=DMA_UNROLL)
        lax.fori_loop(0, tm, drain, 0, unroll=DMA_UNROLL)
        o_ref[...] = buf_ref[...].astype(o_ref.dtype)

    @pl.when(i >= nu_ref[0])
    def _():
        o_ref[...] = jnp.zeros_like(o_ref)


def _gather_rows(src, tok_of_slot, n_used, tm):
    R = tok_of_slot.shape[0]
    D = src.shape[1]
    return pl.pallas_call(
        _gather_kernel,
        grid_spec=pltpu.PrefetchScalarGridSpec(
            num_scalar_prefetch=2,
            grid=(R // tm,),
            in_specs=[pl.BlockSpec(memory_space=pl.ANY)],
            out_specs=pl.BlockSpec((tm, D), lambda i, tok, nu: (i, 0)),
            scratch_shapes=[pltpu.VMEM((tm, D), src.dtype), pltpu.SemaphoreType.DMA(())]),
        out_shape=jax.ShapeDtypeStruct((R, D), BF16),
        compiler_params=_cparams(1),
    )(tok_of_slot, n_used, src)


def _ln2_dense_kernel(x_ref, f_ref, g2_ref, lg_ref, lb_ref, o_ref, *, alpha, n_ctx_blocks, row_off):
    is_ctx = (pl.program_id(0) + row_off) < n_ctx_blocks
    y = alpha * x_ref[...] + _pick(g2_ref[...], is_ctx) * f_ref[...]
    o_ref[...] = _layer_norm(y, lg_ref[...], lb_ref[...])


def _ln2_moe_kernel(sa_ref, sb_ref, x_ref, y_ref, gate_ref, g2_ref, lg_ref, lb_ref, o_ref, bufa_ref, bufb_ref,
                    sem, *, alpha, n_ctx_blocks, row_off):
    i = pl.program_id(0)
    R = ROW_BLOCK
    base = (i + row_off) * R

    def issue(r, c):
        _row_copy(y_ref, bufa_ref, sa_ref[base + r], r, sem).start()
        _row_copy(y_ref, bufb_ref, sb_ref[base + r], r, sem).start()
        return c

    def drain(r, c):
        _row_copy(y_ref, bufa_ref, 0, r, sem).wait()
        _row_copy(y_ref, bufb_ref, 0, r, sem).wait()
        return c

    lax.fori_loop(0, R, issue, 0, unroll=DMA_UNROLL // 2)
    lax.fori_loop(0, R, drain, 0, unroll=DMA_UNROLL // 2)
    gate = gate_ref[...]
    f = gate[:, 0:1] * bufa_ref[...] + gate[:, 1:2] * bufb_ref[...]
    is_ctx = (i + row_off) < n_ctx_blocks
    y = alpha * x_ref[...] + _pick(g2_ref[...], is_ctx) * f
    o_ref[...] = _layer_norm(y, lg_ref[...], lb_ref[...])


def _ln2(x1, f, mod, ln_g, ln_b, layer, alpha, n_ctx_blocks, row_off, moe=None):
    S, D = x1.shape
    L = ln_g.shape[0]
    nb = S // ROW_BLOCK - row_off
    n_pre = 0 if moe is None else 2
    wrap = (lambda f_: (lambda i, *_: f_(i)))
    rows_in = pl.BlockSpec((ROW_BLOCK, D), wrap(lambda i: (i + row_off, 0)))
    rows_out = pl.BlockSpec((ROW_BLOCK, D), wrap(lambda i: (i, 0)))
    vspec = pl.BlockSpec((None, 1, D), wrap(lambda i: (layer, 0, 0)))
    mspec = pl.BlockSpec((None, SUBLANES, D), wrap(lambda i: (layer, 0, 5)))
    common = dict(alpha=alpha, n_ctx_blocks=n_ctx_blocks, row_off=row_off)
    lg, lb = ln_g.reshape(L, 1, D), ln_b.reshape(L, 1, D)
    out_shape = jax.ShapeDtypeStruct((nb * ROW_BLOCK, D), F32)
    if moe is None:
        return pl.pallas_call(
            functools.partial(_ln2_dense_kernel, **common),
            grid=(nb,),
            in_specs=[rows_in, rows_in, mspec, vspec, vspec],
            out_specs=rows_out,
            out_shape=out_shape,
            compiler_params=_cparams(1),
        )(x1, f, mod, lg, lb)
    slot_a, slot_b, gates = moe
    return pl.pallas_call(
        functools.partial(_ln2_moe_kernel, **common),
        grid_spec=pltpu.PrefetchScalarGridSpec(
            num_scalar_prefetch=n_pre,
            grid=(nb,),
            in_specs=[rows_in, pl.BlockSpec(memory_space=pl.ANY),
                      pl.BlockSpec((ROW_BLOCK, TOP_K), wrap(lambda i: (i + row_off, 0))),
                      mspec, vspec, vspec],
            out_specs=rows_out,
            scratch_shapes=[pltpu.VMEM((ROW_BLOCK, D), F32), pltpu.VMEM((ROW_BLOCK, D), F32),
                            pltpu.SemaphoreType.DMA(())]),
        out_shape=out_shape,
        compiler_params=_cparams(1),
    )(slot_a, slot_b, x1, f, gates, mod, lg, lb)


def _route_slots(idx, row0, n_experts, tm):
    S = idx.shape[0]
    n = S - row0
    e_flat = idx[row0:].reshape(-1)
    onehot = (e_flat[:, None] == jnp.arange(n_experts, dtype=jnp.int32)[None, :]).astype(jnp.int32)
    pos = jnp.sum((jnp.cumsum(onehot, axis=0) - 1) * onehot, axis=1)
    counts = jnp.sum(onehot, axis=0)
    padded = ((counts + tm - 1) // tm) * tm
    ends = jnp.cumsum(padded)
    starts = ends - padded
    slot = starts[e_flat] + pos
    n_tiles = -(-(TOP_K * n) // tm) + n_experts
    tok = jnp.repeat(jnp.arange(n, dtype=jnp.int32) + row0, TOP_K)
    tok_of_slot = jnp.full((n_tiles * tm,), row0, jnp.int32).at[slot].set(tok)
    tile_start = jnp.arange(n_tiles, dtype=jnp.int32) * tm
    tile_expert = jnp.minimum(jnp.searchsorted(ends, tile_start, side='right'), n_experts - 1).astype(jnp.int32)
    n_used = (ends[-1] // tm).astype(jnp.int32).reshape(1)
    slot2 = slot.reshape(n, TOP_K).astype(jnp.int32)
    pad = jnp.zeros((row0,), jnp.int32)
    slot_a = jnp.concatenate([pad, slot2[:, 0]])
    slot_b = jnp.concatenate([pad, slot2[:, 1]])
    return tok_of_slot, tile_expert, n_used, slot_a, slot_b


def kernel(x, c, ctx, c_ctx, w_mod, b_mod, w_in, w_out, ln1_g, ln1_b, ln2_g, ln2_b, pool_w, pool_scale,
           diff_lambda, diff_subln_g, conv_dw, conv_db, conv_ln_g, conv_ln_b, conv_pw, s5_a_re, s5_a_im,
           s5_log_dt, s5_b_re, s5_b_im, s5_c_re, s5_c_im, s5_d, s5_glu_w, s5_glu_b, ffn_w1, ffn_w3, ffn_w2,
           router_w, moe_w1, moe_w3, moe_w2):
    B, T, D = x.shape
    Tc = ctx.shape[1]
    depth = w_mod.shape[0]
    assert B == 1 and Tc % ROW_BLOCK == 0 and T % ROW_BLOCK == 0
    GW = D // N_GROUPS
    n_ctx_blocks = Tc // ROW_BLOCK
    alpha = (2.0 * depth) ** 0.25

    cc = jnp.zeros((SUBLANES, D), F32).at[0].set(c[0]).at[1].set(c_ctx)
    mod = _modulation(cc, w_mod, b_mod)
    cos, sin = _rope_tables(T, Tc, GW)
    xs = jnp.concatenate([ctx[0], x[0]], axis=0)

    POOL_B, Q_B, K_B, V_B, CONV_B, S5_B = 0, 1, 2, 3, 2, 6

    for l in range(depth):
        last = l == depth - 1
        lam_init = 0.8 - 0.6 * math.exp(-0.3 * l)
        z = _in_projection(xs, mod, w_in, l, n_ctx_blocks)

        pa = _pool_mixer(z, pool_w, pool_scale, l, n_ctx_blocks)

        qt, kk, vt = _qkv_prep(z, cos, sin, Q_B, K_B, V_B)
        lv = diff_lambda[l].astype(F32)
        lam = jnp.exp(jnp.sum(lv[0] * lv[1])) - jnp.exp(jnp.sum(lv[2] * lv[3])) + lam_init
        lam_row = jnp.full((1, ATTN_TQ), lam, F32)
        g_col = (diff_subln_g[l].astype(F32) * (1.0 - lam_init)).reshape(GW, 1)
        pb = _diff_attention(qt, kk, vt, lam_row, g_col, Tc)

        pcv = _conv_mixer(z, CONV_B, conv_dw, conv_db, conv_ln_g, conv_ln_b, conv_pw, l, n_ctx_blocks)

        seg_len = xs.shape[0] // S5_SEGS
        pf = _s5_params(s5_a_re[l, 0], s5_a_im[l, 0], s5_log_dt[l, 0], s5_b_re[l, 0], s5_b_im[l, 0],
                        s5_c_re[l, 0], s5_c_im[l, 0], seg_len)
        pr = _s5_params(s5_a_re[l, 1], s5_a_im[l, 1], s5_log_dt[l, 1], s5_b_re[l, 1], s5_b_im[l, 1],
                        s5_c_re[l, 1], s5_c_im[l, 1], seg_len)
        yf, yr = _s5_scan(z, S5_B, pf, pr, Tc)
        pd = _s5_glu(yf, yr, z, S5_B, s5_d, s5_glu_w, s5_glu_b, l, n_ctx_blocks)

        w_out_bf = _cast_bf16(w_out, l)
        row_off = n_ctx_blocks if last else 0
        if l % 2 == 0:
            x1, h = _mix_out((pa, pb, pcv, pd), w_out_bf, xs, mod, ln1_g, ln1_b, l, alpha, n_ctx_blocks, None)
            S = xs.shape[0]
            tm = next(t for t in (768, 512, ROW_BLOCK) if S % t == 0)
            n_tiles = S // tm
            f = _swiglu_tiles(h, ffn_w1[l // 2][None], ffn_w3[l // 2][None], ffn_w2[l // 2][None],
                              jnp.zeros((n_tiles,), jnp.int32), jnp.full((1,), n_tiles, jnp.int32), tm)
            xs_new = _ln2(x1, f, mod, ln2_g, ln2_b, l, alpha, n_ctx_blocks, row_off)
        else:
            x1, h, idx, gates = _mix_out((pa, pb, pcv, pd), w_out_bf, xs, mod, ln1_g, ln1_b, l, alpha,
                                         n_ctx_blocks, router_w[l // 2])
            n_exp = router_w.shape[-1]
            row0 = row_off * ROW_BLOCK
            tok_of_slot, tile_expert, n_used, slot_a, slot_b = _route_slots(idx, row0, n_exp, FFN_TM)
            hs = _gather_rows(h, tok_of_slot, n_used, FFN_TM)
            y = _swiglu_tiles(hs, moe_w1[l // 2], moe_w3[l // 2], moe_w2[l // 2], tile_expert, n_used, FFN_TM)
            xs_new = _ln2(x1, y, mod, ln2_g, ln2_b, l, alpha, n_ctx_blocks, row_off, moe=(slot_a, slot_b, gates))
        xs = xs_new
    return xs[None]
```

```python
import functools
import math

import numpy as np
import jax
import jax.numpy as jnp
from jax import lax
from jax.experimental import pallas as pl
from jax.experimental.pallas import tpu as pltpu

F32 = jnp.float32
BF16 = jnp.bfloat16

GRID_W = 64
N_GROUPS = 4
POOL_WINDOWS = (2, 4, 8, 16)
DIFF_HEADS = 8
DIFF_QK = 32
CONV_WIDTH = 31
S5_P = 16
S5_N = 64
TOP_K = 2
ROPE_BASE = 10000.0
LN_EPS = 1e-5

LANES = 128
SUBLANES = 8
ROW_BLOCK = 256
VMEM_LIMIT = 56 * 1024 * 1024


def _cparams(n_axes, vmem=VMEM_LIMIT):
    return pltpu.CompilerParams(dimension_semantics=("arbitrary",) * n_axes, vmem_limit_bytes=vmem)


def _layer_norm(y, g, b):
    mu = jnp.mean(y, -1, keepdims=True)
    yc = y - mu
    var = jnp.mean(yc * yc, -1, keepdims=True)
    return yc * lax.rsqrt(var + LN_EPS) * g + b


def _silu(x):
    return x * jax.nn.sigmoid(x)


def _mod_kernel(cc_ref, w_ref, b_ref, o_ref):
    a = _silu(cc_ref[...])
    o_ref[...] = jnp.dot(a.astype(BF16), w_ref[...].astype(BF16), preferred_element_type=F32) + b_ref[...]


def _modulation(cc, w_mod, b_mod):
    L, D, N = w_mod.shape
    tn = 1536
    assert N % tn == 0
    return pl.pallas_call(
        _mod_kernel,
        grid=(L, N // tn),
        in_specs=[pl.BlockSpec((SUBLANES, D), lambda l, j: (0, 0)),
                  pl.BlockSpec((None, D, tn), lambda l, j: (l, 0, j)),
                  pl.BlockSpec((None, 1, tn), lambda l, j: (l, 0, j))],
        out_specs=pl.BlockSpec((None, SUBLANES, tn), lambda l, j: (l, 0, j)),
        out_shape=jax.ShapeDtypeStruct((L, SUBLANES, N), F32),
        compiler_params=_cparams(2),
    )(cc, w_mod, b_mod.reshape(L, 1, N))


def _mod_spec(layer, chunk, D, n_grid_axes):
    if n_grid_axes == 1:
        return pl.BlockSpec((None, SUBLANES, D), lambda i: (layer, 0, chunk))
    return pl.BlockSpec((None, SUBLANES, D), lambda j, i: (layer, 0, chunk))


def _pick(m, is_ctx):
    return jnp.where(is_ctx, m[1:2, :], m[0:1, :])


def _inproj_kernel(x_ref, sh_ref, sc_ref, w_ref, o_ref, wb_ref, *, n_ctx_blocks):
    i = pl.program_id(1)

    @pl.when(i == 0)
    def _():
        wb_ref[...] = w_ref[...].astype(BF16)

    is_ctx = i < n_ctx_blocks
    h = x_ref[...] * (1.0 + _pick(sc_ref[...], is_ctx)) + _pick(sh_ref[...], is_ctx)
    o_ref[...] = jnp.dot(h.astype(BF16), wb_ref[...], preferred_element_type=F32).astype(o_ref.dtype)


def _in_projection(xs, mod, w_in, layer, n_ctx_blocks):
    S, D = xs.shape
    N = w_in.shape[-1]
    tn = 1792
    assert N % tn == 0 and S % ROW_BLOCK == 0
    return pl.pallas_call(
        functools.partial(_inproj_kernel, n_ctx_blocks=n_ctx_blocks),
        grid=(N // tn, S // ROW_BLOCK),
        in_specs=[pl.BlockSpec((ROW_BLOCK, D), lambda j, i: (i, 0)),
                  _mod_spec(layer, 0, D, 2),
                  _mod_spec(layer, 1, D, 2),
                  pl.BlockSpec((None, D, tn), lambda j, i: (layer, 0, j))],
        out_specs=pl.BlockSpec((ROW_BLOCK, tn), lambda j, i: (i, j)),
        out_shape=jax.ShapeDtypeStruct((S, N), BF16),
        scratch_shapes=[pltpu.VMEM((D, tn), BF16)],
        compiler_params=_cparams(2),
    )(xs, mod, mod, w_in)


def _seq_edges(i, n_blocks, n_ctx_blocks):
    prev_ok = jnp.logical_and(i != 0, i != n_ctx_blocks)
    next_ok = jnp.logical_and(i != n_ctx_blocks - 1, i != n_blocks - 1)
    return prev_ok, next_ok


def _pool_kernel(p_ref, c_ref, n_ref, w_ref, scale_ref, o_ref, ext_ref, *, n_blocks, n_ctx_blocks):
    i = pl.program_id(0)
    R = ROW_BLOCK
    halo = SUBLANES
    prev_ok, next_ok = _seq_edges(i, n_blocks, n_ctx_blocks)
    cur = c_ref[...].astype(F32)
    pack = 2 * SUBLANES
    ext_ref[0:halo, :] = jnp.where(prev_ok, p_ref[R - pack:R, :].astype(F32)[pack - halo:], 0.0)
    ext_ref[halo:halo + R, :] = cur
    ext_ref[halo + R:halo + R + halo, :] = jnp.where(next_ok, n_ref[0:pack, :].astype(F32)[:halo], 0.0)
    rloc = lax.broadcasted_iota(jnp.int32, (R, 1), 0)
    gw = cur.shape[1] // len(POOL_WINDOWS)
    outs = []
    for g, w in enumerate(POOL_WINDOWS):
        acc = jnp.zeros((R, gw), F32)
        cnt = jnp.zeros((R, 1), F32)
        for d in range(-(w // 2), w - w // 2):
            acc = acc + ext_ref[halo + d:halo + d + R, g * gw:(g + 1) * gw]
            valid = jnp.logical_and(jnp.logical_or(rloc + d >= 0, prev_ok),
                                    jnp.logical_or(rloc + d < R, next_ok))
            cnt = cnt + valid.astype(F32)
        diff = acc / cnt - cur[:, g * gw:(g + 1) * gw]
        outs.append(jnp.dot(diff.astype(BF16), w_ref[g].astype(BF16), preferred_element_type=F32))
    o_ref[...] = jnp.concatenate(outs, axis=-1) * scale_ref[...]


def _pool_mixer(z, pool_w, pool_scale, layer, n_ctx_blocks):
    S = z.shape[0]
    nb = S // ROW_BLOCK
    GW = pool_scale.shape[-1]
    G, gw = pool_w.shape[1], pool_w.shape[2]
    return pl.pallas_call(
        functools.partial(_pool_kernel, n_blocks=nb, n_ctx_blocks=n_ctx_blocks),
        grid=(nb,),
        in_specs=[pl.BlockSpec((ROW_BLOCK, GW), lambda i: (jnp.maximum(i - 1, 0), 0)),
                  pl.BlockSpec((ROW_BLOCK, GW), lambda i: (i, 0)),
                  pl.BlockSpec((ROW_BLOCK, GW), lambda i: (jnp.minimum(i + 1, nb - 1), 0)),
                  pl.BlockSpec((None, G, gw, gw), lambda i: (layer, 0, 0, 0)),
                  pl.BlockSpec((None, 1, GW), lambda i: (layer, 0, 0))],
        out_specs=pl.BlockSpec((ROW_BLOCK, GW), lambda i: (i, 0)),
        out_shape=jax.ShapeDtypeStruct((S, GW), F32),
        scratch_shapes=[pltpu.VMEM((ROW_BLOCK + 2 * SUBLANES, GW), F32)],
        compiler_params=_cparams(1),
    )(z, z, z, pool_w, pool_scale.reshape(pool_scale.shape[0], 1, GW))


CONV_HALO = 16


def _conv_kernel(p_ref, c_ref, n_ref, dw_ref, db_ref, g_ref, b_ref, pw_ref, o_ref, ext_ref,
                 *, n_blocks, n_ctx_blocks):
    i = pl.program_id(0)
    R = ROW_BLOCK
    H = CONV_HALO
    GW = o_ref.shape[1]
    prev_ok, next_ok = _seq_edges(i, n_blocks, n_ctx_blocks)

    def glu(u):
        u = u.astype(F32)
        return u[:, :GW] * jax.nn.sigmoid(u[:, GW:])

    ext_ref[0:H, :] = jnp.where(prev_ok, glu(p_ref[R - H:R, :]), 0.0)
    ext_ref[H:H + R, :] = glu(c_ref[...])
    ext_ref[H + R:H + R + H, :] = jnp.where(next_ok, glu(n_ref[0:H, :]), 0.0)
    off = H - CONV_WIDTH // 2
    acc = jnp.zeros((R, GW), F32)
    for j in range(CONV_WIDTH):
        acc = acc + ext_ref[off + j:off + j + R, :] * dw_ref[j:j + 1, :]
    y = _layer_norm(acc + db_ref[...], g_ref[...], b_ref[...])
    o_ref[...] = jnp.dot(_silu(y).astype(BF16), pw_ref[...].astype(BF16), preferred_element_type=F32)


def _conv_mixer(z, col_block, conv_dw, conv_db, conv_ln_g, conv_ln_b, conv_pw, layer, n_ctx_blocks):
    S = z.shape[0]
    nb = S // ROW_BLOCK
    GW = conv_db.shape[-1]
    L = conv_db.shape[0]
    vec = lambda a: a.reshape(L, 1, GW)
    vspec = pl.BlockSpec((None, 1, GW), lambda i: (layer, 0, 0))
    return pl.pallas_call(
        functools.partial(_conv_kernel, n_blocks=nb, n_ctx_blocks=n_ctx_blocks),
        grid=(nb,),
        in_specs=[pl.BlockSpec((ROW_BLOCK, 2 * GW), lambda i: (jnp.maximum(i - 1, 0), col_block)),
                  pl.BlockSpec((ROW_BLOCK, 2 * GW), lambda i: (i, col_block)),
                  pl.BlockSpec((ROW_BLOCK, 2 * GW), lambda i: (jnp.minimum(i + 1, nb - 1), col_block)),
                  pl.BlockSpec((None, CONV_WIDTH, GW), lambda i: (layer, 0, 0)),
                  vspec, vspec, vspec,
                  pl.BlockSpec((None, GW, GW), lambda i: (layer, 0, 0))],
        out_specs=pl.BlockSpec((ROW_BLOCK, GW), lambda i: (i, 0)),
        out_shape=jax.ShapeDtypeStruct((S, GW), F32),
        scratch_shapes=[pltpu.VMEM((ROW_BLOCK + 2 * CONV_HALO, GW), F32)],
        compiler_params=_cparams(1),
    )(z, z, z, conv_dw, vec(conv_db), vec(conv_ln_g), vec(conv_ln_b), conv_pw)


def _rope_tables(T, Tc, width):
    ax = DIFF_QK // 2
    inv = ROPE_BASE ** (-jnp.arange(0, ax, 2, dtype=F32) / ax)
    t = jnp.arange(T)
    row = (t // GRID_W).astype(F32)
    col = (t % GRID_W).astype(F32)
    ang = jnp.stack([row[:, None] * inv, col[:, None] * inv], axis=1)
    cos = jnp.cos(ang)[:, :, None, :]
    sin = jnp.sin(ang)[:, :, None, :]
    cos = jnp.broadcast_to(cos, (T, 2, 2, ax // 2)).reshape(T, DIFF_QK)
    sin = jnp.concatenate([-sin, sin], axis=2).reshape(T, DIFF_QK)
    reps = width // DIFF_QK
    cos = jnp.concatenate([jnp.ones((Tc, DIFF_QK), F32), cos], axis=0)
    sin = jnp.concatenate([jnp.zeros((Tc, DIFF_QK), F32), sin], axis=0)
    return jnp.tile(cos, (1, reps)), jnp.tile(sin, (1, reps))


def _qkv_prep_kernel(q_ref, k_ref, v_ref, cos_ref, sin_ref, qo_ref, ko_ref, vo_ref, kn_ref):
    W = q_ref.shape[1]
    half = DIFF_QK // 4
    lane = lax.broadcasted_iota(jnp.int32, (1, W), 1)
    first = (lane % (2 * half)) < half
    cos = jnp.concatenate([cos_ref[...]] * (W // LANES), axis=1)
    sin = jnp.concatenate([sin_ref[...]] * (W // LANES), axis=1)

    def rope(x):
        partner = jnp.where(first, pltpu.roll(x, W - half, 1), pltpu.roll(x, half, 1))
        return x * cos + partner * sin

    qo_ref[...] = (rope(q_ref[...].astype(F32)) * (DIFF_QK ** -0.5 * math.log2(math.e))).T.astype(BF16)
    kb = rope(k_ref[...].astype(F32)).astype(BF16)
    ko_ref[...] = kb
    ksq = kb.astype(F32).T
    ksq = ksq * ksq
    for grp in range(W // DIFF_QK):
        n2 = jnp.sum(ksq[grp * DIFF_QK:(grp + 1) * DIFF_QK, :], axis=0, keepdims=True)
        kn_ref[grp:grp + 1, :] = jnp.broadcast_to(jnp.max(n2, axis=1, keepdims=True), (1, LANES))
    vt = v_ref[...].astype(F32).T.astype(BF16)
    dv = LANES // 2
    ones = jnp.ones((ATTN_VROWS - dv, vt.shape[1]), BF16)
    for h in range(W // dv):
        vo_ref[h * ATTN_VROWS:h * ATTN_VROWS + dv, :] = vt[h * dv:(h + 1) * dv, :]
        vo_ref[h * ATTN_VROWS + dv:(h + 1) * ATTN_VROWS, :] = ones


def _qkv_prep(z, cos, sin, W, q_blk, k_blk, v_blk):
    S = z.shape[0]
    nb = S // ROW_BLOCK
    row = lambda c: pl.BlockSpec((ROW_BLOCK, W), lambda i: (i, c))
    tab = pl.BlockSpec((ROW_BLOCK, LANES), lambda i: (i, 0))
    return pl.pallas_call(
        _qkv_prep_kernel,
        grid=(nb,),
        in_specs=[row(q_blk), row(k_blk), row(v_blk), tab, tab],
        out_specs=[pl.BlockSpec((W, ROW_BLOCK), lambda i: (0, i)), row(0),
                   pl.BlockSpec((DIFF_HEADS * ATTN_VROWS, ROW_BLOCK), lambda i: (0, i)),
                   pl.BlockSpec((None, W // DIFF_QK, LANES), lambda i: (i, 0, 0))],
        out_shape=[jax.ShapeDtypeStruct((W, S), BF16), jax.ShapeDtypeStruct((S, W), BF16),
                   jax.ShapeDtypeStruct((DIFF_HEADS * ATTN_VROWS, S), BF16),
                   jax.ShapeDtypeStruct((nb, W // DIFF_QK, LANES), F32)],
        compiler_params=_cparams(1),
    )(z, z, z, cos, sin)


ATTN_TQ = 256
ATTN_VROWS = LANES // 2 + 2 * SUBLANES
ATTN_MARGIN = 64.0
ATTN_TK = (4096, 2048, 1024)


def _attn_kernel(qt_ref, k_ref, vt_ref, lam_ref, g_ref, kn_ref, o_ref, qq_ref, m_ref, acc_ref,
                 *, n_ctx, n_ctx_blocks, n_lat_chunks, tk):
    i = pl.program_id(1)
    tq = ATTN_TQ
    dv = LANES // 2
    qt = qt_ref[...]
    feat = lax.broadcasted_iota(jnp.int32, (LANES, 1), 0)
    zero = jnp.zeros_like(qt)
    for hh in range(2):
        for comp in range(2):
            lo = hh * dv + comp * DIFF_QK
            keep = jnp.logical_and(feat >= lo, feat < lo + DIFF_QK)
            qq_ref[hh, :, comp * tq:(comp + 1) * tq] = jnp.where(keep, qt, zero)

    def attend(start, size, mode):
        kk = k_ref[pl.ds(start, size), :]
        for hh in range(2):
            s = jnp.dot(kk, qq_ref[hh], preferred_element_type=F32)
            vv = vt_ref[hh * ATTN_VROWS:(hh + 1) * ATTN_VROWS, pl.ds(start, size)]
            mx = jnp.max(s, axis=0, keepdims=True)
            if mode == "first":
                m_ref[hh] = mx
                p = jnp.exp2(s - mx)
                acc_ref[hh] = jnp.dot(vv, p.astype(BF16), preferred_element_type=F32)
            elif mode == "exact":
                m_old = m_ref[hh]
                m_new = jnp.maximum(m_old, mx)
                m_ref[hh] = m_new
                p = jnp.exp2(s - m_new)
                acc_ref[hh] = (jnp.exp2(m_old - m_new) * acc_ref[hh]
                               + jnp.dot(vv, p.astype(BF16), preferred_element_type=F32))
            else:
                m_old = m_ref[hh]
                p = jnp.exp2(s - m_old)
                m_new = jnp.maximum(m_old, mx)
                m_ref[hh] = m_new
                acc_ref[hh] = (jnp.exp2(m_old - m_new)
                               * (acc_ref[hh] + jnp.dot(vv, p.astype(BF16), preferred_element_type=F32)))

    attend(0, n_ctx, "first")
    n_steps = jnp.where(i < n_ctx_blocks, 0, n_lat_chunks)

    pair = pl.program_id(0)
    col = lax.broadcasted_iota(jnp.int32, (1, 2 * tq), 1)
    excess = jnp.full((1, 2 * tq), -jnp.inf, F32)
    for hh in range(2):
        qf = qq_ref[hh].astype(F32)
        qn = jnp.sqrt(jnp.sum(qf * qf, axis=0, keepdims=True))
        grp = (2 * pair + hh) * 2
        kn = jnp.where(col < tq, kn_ref[grp], kn_ref[grp + 1])
        excess = jnp.maximum(excess, qn * kn - m_ref[hh])
    safe = jnp.max(excess) < ATTN_MARGIN

    def loop(mode):
        def body(c, carry):
            attend(pl.multiple_of(n_ctx + c * tk, LANES), tk, mode)
            return carry
        lax.fori_loop(0, n_steps, body, 0)

    @pl.when(safe)
    def _():
        loop("deferred")

    @pl.when(jnp.logical_not(safe))
    def _():
        loop("exact")

    lam = lam_ref[...]
    outs = []
    for hh in range(2):
        acc = acc_ref[hh]
        ratio = acc[:dv] / acc[dv:dv + 1]
        o = ratio[:, :tq] - lam * ratio[:, tq:]
        r = lax.rsqrt(jnp.sum(o * o, axis=0, keepdims=True) / dv + LN_EPS)
        outs.append(o * r)
    o_ref[...] = (jnp.concatenate(outs, axis=0) * g_ref[...]).T


def _diff_attention(qt, k, vt, lam, g, kn, n_ctx):
    W, S = qt.shape
    assert S % ATTN_TQ == 0 and n_ctx % ATTN_TQ == 0
    tk = next(t for t in ATTN_TK if (S - n_ctx) % t == 0)
    nq = S // ATTN_TQ
    return pl.pallas_call(
        functools.partial(_attn_kernel, n_ctx=n_ctx, n_ctx_blocks=n_ctx // ATTN_TQ,
                          n_lat_chunks=(S - n_ctx) // tk, tk=tk),
        grid=(W // LANES, nq),
        in_specs=[pl.BlockSpec((LANES, ATTN_TQ), lambda p, i: (p, i)),
                  pl.BlockSpec((S, LANES), lambda p, i: (0, p)),
                  pl.BlockSpec((2 * ATTN_VROWS, S), lambda p, i: (p, 0)),
                  pl.BlockSpec((1, ATTN_TQ), lambda p, i: (0, 0)),
                  pl.BlockSpec((LANES, 1), lambda p, i: (p, 0)),
                  pl.BlockSpec(memory_space=pltpu.SMEM)],
        out_specs=pl.BlockSpec((ATTN_TQ, LANES), lambda p, i: (i, p)),
        out_shape=jax.ShapeDtypeStruct((S, W), F32),
        scratch_shapes=[pltpu.VMEM((2, LANES, 2 * ATTN_TQ), BF16),
                        pltpu.VMEM((2, 1, 2 * ATTN_TQ), F32),
                        pltpu.VMEM((2, ATTN_VROWS, 2 * ATTN_TQ), F32)],
        compiler_params=_cparams(2),
    )(qt, k, vt, lam, g, kn)


S5_SEGS = SUBLANES
S5_KB = 32
S5_GB = 8


def _s5_params(a_re, a_im, log_dt, b_re, b_im, c_re, c_im, seg_len):
    G, N = a_re.shape
    P = b_re.shape[-1]
    nblk = G // S5_GB
    a_re, a_im = a_re.astype(F32), a_im.astype(F32)
    dt = jnp.exp(log_dt.astype(F32))[:, None]
    lr, li = dt * a_re, dt * a_im
    mag = jnp.exp(lr)
    ar, ai = mag * jnp.cos(li), mag * jnp.sin(li)
    den = a_re * a_re + a_im * a_im
    qr = ((ar - 1.0) * a_re + ai * a_im) / den
    qi = (ai * a_re - (ar - 1.0) * a_im) / den
    b_re, b_im = b_re.astype(F32), b_im.astype(F32)
    br = qr[..., None] * b_re - qi[..., None] * b_im
    bi = qr[..., None] * b_im + qi[..., None] * b_re
    mag_l = jnp.exp(seg_len * lr)
    alr, ali = mag_l * jnp.cos(seg_len * li), mag_l * jnp.sin(seg_len * li)
    eye = jnp.eye(S5_GB, dtype=F32)
    wb = lambda m: jnp.einsum('gh,bgnp->bgphn', eye, m.reshape(nblk, S5_GB, N, P)).reshape(
        nblk, S5_GB * P, S5_GB * N)
    w_in = jnp.concatenate([wb(br), wb(bi)], axis=2)
    cm = lambda m: jnp.einsum('gh,bgpn->bhngp', eye, m.astype(F32).reshape(nblk, S5_GB, P, N)).reshape(
        nblk, S5_GB * N, S5_GB * P)
    w_out = jnp.concatenate([cm(c_re), -cm(c_im)], axis=1)
    row = lambda r, i: jnp.concatenate([r.reshape(1, G * N), i.reshape(1, G * N)], axis=1)
    coef = jnp.broadcast_to(row(ar, ai), (S5_SEGS, 2 * G * N))
    return w_in.astype(BF16), coef, row(alr, ali), w_out.astype(BF16)


def _s5_kernel(*refs, emit_out):
    n_seg = S5_SEGS
    uf_ref = refs[0]
    ur_refs = refs[1:1 + n_seg]
    rest = refs[1 + n_seg:]
    if emit_out:
        (wbf_ref, wbr_ref, af_ref, ar_ref, ef_ref, er_ref, alf_ref, alr_ref, cf_ref, cr_ref,
         yf_ref, yr_ref, stage_ref, bf_ref, br_ref, hf_ref, hr_ref) = rest
    else:
        (wbf_ref, wbr_ref, af_ref, ar_ref, ef_out_ref, er_out_ref,
         stage_ref, bf_ref, br_ref, hf_ref, hr_ref) = rest
    g = pl.program_id(0)
    KB = S5_KB
    R = n_seg * KB
    NS = af_ref.shape[1] // 2
    nblk = wbf_ref.shape[0]
    wi = wbf_ref.shape[1]
    ws = wbf_ref.shape[2] // 2

    def cmul_add(a_row, h, add):
        are, aim = a_row[:, :NS], a_row[:, NS:]
        hre, him = h[:, :NS], h[:, NS:]
        return jnp.concatenate([are * hre - aim * him + add[:, :NS], are * him + aim * hre + add[:, NS:]], axis=1)

    @pl.when(g == 0)
    def _():
        if emit_out:
            def chain(e_ref, al_ref, order):
                al = al_ref[...]
                c = jnp.zeros((1, 2 * NS), F32)
                rows = [None] * n_seg
                for s in order:
                    rows[s] = c
                    c = cmul_add(al, c, e_ref[s:s + 1, :])
                return jnp.concatenate(rows, axis=0)
            hf_ref[...] = chain(ef_ref, alf_ref, range(n_seg))
            hr_ref[...] = chain(er_ref, alr_ref, range(n_seg - 1, -1, -1))
        else:
            hf_ref[...] = jnp.zeros_like(hf_ref)
            hr_ref[...] = jnp.zeros_like(hr_ref)

    def interleaved(load_seg):
        for s in range(n_seg):
            blk = load_seg(s).astype(F32)
            for c in range(nblk):
                stage_ref[c, s * KB:(s + 1) * KB, :] = blk[:, c * wi:(c + 1) * wi]
        rows = [jnp.concatenate([stage_ref[c, pl.ds(kk, n_seg, stride=KB), :] for c in range(nblk)], axis=1)
                for kk in range(KB)]
        return jnp.concatenate(rows, axis=0).astype(BF16)

    def project_in(u, w_ref, buf_ref):
        for b in range(nblk):
            res = jnp.dot(u[:, b * wi:(b + 1) * wi], w_ref[b], preferred_element_type=F32)
            buf_ref[:, b * ws:(b + 1) * ws] = res[:, :ws]
            buf_ref[:, NS + b * ws:NS + (b + 1) * ws] = res[:, ws:]

    def scan(buf_ref, a_ref, h_ref, reverse):
        def step(t, h):
            kk = (KB - 1 - t) if reverse else t
            r0 = pl.multiple_of(kk * n_seg, n_seg)
            new = cmul_add(a_ref[...], h, buf_ref[pl.ds(r0, n_seg), :])
            if emit_out:
                buf_ref[pl.ds(r0, n_seg), :] = new
            return new
        h_ref[...] = lax.fori_loop(0, KB, step, h_ref[...])

    def project_out(buf_ref, c_ref, y_ref):
        for b in range(nblk):
            hcat = jnp.concatenate([buf_ref[:, b * ws:(b + 1) * ws], buf_ref[:, NS + b * ws:NS + (b + 1) * ws]],
                                   axis=1).astype(BF16)
            stage_ref[b] = jnp.dot(hcat, c_ref[b], preferred_element_type=F32)
        for s in range(n_seg):
            y_ref[s] = jnp.concatenate([stage_ref[c, pl.ds(s, KB, stride=n_seg), :] for c in range(nblk)], axis=1)

    project_in(interleaved(lambda s: uf_ref[s]), wbf_ref, bf_ref)
    scan(bf_ref, af_ref, hf_ref, False)
    if emit_out:
        project_out(bf_ref, cf_ref, yf_ref)
    project_in(interleaved(lambda s: ur_refs[s][...]), wbr_ref, br_ref)
    scan(br_ref, ar_ref, hr_ref, True)
    if emit_out:
        project_out(br_ref, cr_ref, yr_ref)
    else:
        ef_out_ref[...] = hf_ref[...]
        er_out_ref[...] = hr_ref[...]


def _s5_pass(z, col_block, pf, pr, ends, n_ctx):
    S, NZ = z.shape
    GW = pf[0].shape[1] * pf[0].shape[0]
    NS2 = pf[1].shape[1]
    seg_len = S // S5_SEGS
    steps = seg_len // S5_KB
    nblocks = S // S5_KB
    assert S % (S5_SEGS * S5_KB) == 0 and n_ctx % S5_KB == 0
    ctx_blocks = n_ctx // S5_KB
    emit_out = ends is not None
    z4 = z.reshape(S5_SEGS, steps, S5_KB, NZ)
    z3 = z.reshape(nblocks, S5_KB, NZ)

    def rev_spec(s):
        return pl.BlockSpec((None, S5_KB, GW),
                            lambda g: ((s * steps + steps - 1 - g + ctx_blocks) % nblocks, 0, col_block))

    const = lambda a: pl.BlockSpec(a.shape, lambda g: (0,) * a.ndim)
    in_specs = [pl.BlockSpec((S5_SEGS, None, S5_KB, GW), lambda g: (0, g, 0, col_block))]
    in_specs += [rev_spec(s) for s in range(S5_SEGS)]
    args = [z4] + [z3] * S5_SEGS
    weights = [pf[0], pr[0], pf[1], pr[1]]
    if emit_out:
        weights += [ends[0], ends[1], pf[2], pr[2], pf[3], pr[3]]
    in_specs += [const(a) for a in weights]
    args += weights
    scratch = [pltpu.VMEM((pf[0].shape[0], S5_SEGS * S5_KB, pf[0].shape[1]), F32),
               pltpu.VMEM((S5_SEGS * S5_KB, NS2), F32), pltpu.VMEM((S5_SEGS * S5_KB, NS2), F32),
               pltpu.VMEM((S5_SEGS, NS2), F32), pltpu.VMEM((S5_SEGS, NS2), F32)]
    if emit_out:
        yshape = jax.ShapeDtypeStruct((S5_SEGS, steps, S5_KB, GW), F32)
        out_shape = [yshape, yshape]
        out_specs = [pl.BlockSpec((S5_SEGS, None, S5_KB, GW), lambda g: (0, g, 0, 0)),
                     pl.BlockSpec((S5_SEGS, None, S5_KB, GW), lambda g: (0, steps - 1 - g, 0, 0))]
    else:
        eshape = jax.ShapeDtypeStruct((S5_SEGS, NS2), F32)
        out_shape = [eshape, eshape]
        out_specs = [pl.BlockSpec((S5_SEGS, NS2), lambda g: (0, 0))] * 2
    return pl.pallas_call(
        functools.partial(_s5_kernel, emit_out=emit_out),
        grid=(steps,),
        in_specs=in_specs,
        out_specs=out_specs,
        out_shape=out_shape,
        scratch_shapes=scratch,
        compiler_params=_cparams(1),
    )(*args)


def _s5_scan(z, col_block, pf, pr, n_ctx):
    S = z.shape[0]
    GW = pf[0].shape[1] * pf[0].shape[0]
    ends = _s5_pass(z, col_block, pf, pr, None, n_ctx)
    yf, yr = _s5_pass(z, col_block, pf, pr, ends, n_ctx)
    return yf.reshape(S, GW), yr.reshape(S, GW)


def _s5_glu_kernel(yf_ref, yr_ref, u_ref, d_ref, w_ref, b_ref, o_ref):
    y = yf_ref[...] + yr_ref[...] + d_ref[...] * u_ref[...].astype(F32)
    zz = jax.nn.gelu(y)
    gate = jnp.dot(zz.astype(BF16), w_ref[...].astype(BF16), preferred_element_type=F32) + b_ref[...]
    o_ref[...] = zz * jax.nn.sigmoid(gate)


def _s5_glu(yf, yr, z, col_block, s5_d, glu_w, glu_b, layer, n_ctx_blocks):
    S, GW = yf.shape
    L = s5_d.shape[0]
    nb = S // ROW_BLOCK
    row = lambda c: pl.BlockSpec((ROW_BLOCK, GW), lambda i: (i, c))
    rot = pl.BlockSpec((ROW_BLOCK, GW), lambda i: ((i + nb - n_ctx_blocks) % nb, 0))
    vspec = pl.BlockSpec((None, 1, GW), lambda i: (layer, 0, 0))
    return pl.pallas_call(
        _s5_glu_kernel,
        grid=(nb,),
        in_specs=[row(0), rot, row(col_block), vspec,
                  pl.BlockSpec((None, GW, GW), lambda i: (layer, 0, 0)), vspec],
        out_specs=row(0),
        out_shape=jax.ShapeDtypeStruct((S, GW), F32),
        compiler_params=_cparams(1),
    )(yf, yr, z, s5_d.reshape(L, 1, GW), glu_w, glu_b.reshape(L, 1, GW))


def _cast_kernel(x_ref, o_ref):
    o_ref[...] = x_ref[...].astype(o_ref.dtype)


def _cast_bf16(w, layer):
    _, K, N = w.shape
    tk = 512
    return pl.pallas_call(
        _cast_kernel,
        grid=(K // tk,),
        in_specs=[pl.BlockSpec((None, tk, N), lambda i: (layer, i, 0))],
        out_specs=pl.BlockSpec((tk, N), lambda i: (i, 0)),
        out_shape=jax.ShapeDtypeStruct((K, N), BF16),
        compiler_params=_cparams(1),
    )(w)


def _mixout_kernel(*refs, alpha, n_ctx_blocks, route, h_dtype):
    (pa_ref, pb_ref, pc_ref, pd_ref, w_ref, x_ref, g1_ref, lg_ref, lb_ref, sh_ref, sc_ref) = refs[:11]
    if route:
        rw_ref, x1_ref, h_ref, idx_ref, gate_ref = refs[11:]
    else:
        x1_ref, h_ref = refs[11:]
    i = pl.program_id(0)
    is_ctx = i < n_ctx_blocks
    GW = pa_ref.shape[1]
    mix = jnp.zeros(x_ref.shape, F32)
    for k, p_ref in enumerate((pa_ref, pb_ref, pc_ref, pd_ref)):
        mix = mix + jnp.dot(p_ref[...].astype(BF16), w_ref[k * GW:(k + 1) * GW, :],
                            preferred_element_type=F32)
    y = alpha * x_ref[...] + _pick(g1_ref[...], is_ctx) * mix
    x1 = _layer_norm(y, lg_ref[...], lb_ref[...])
    x1_ref[...] = x1
    h = x1 * (1.0 + _pick(sc_ref[...], is_ctx)) + _pick(sh_ref[...], is_ctx)
    h_ref[...] = h.astype(h_dtype)
    if route:
        rw = rw_ref[...]
        h_hi = h.astype(BF16)
        h_lo = (h - h_hi.astype(F32)).astype(BF16)
        w_hi = rw.astype(BF16)
        w_lo = (rw - w_hi.astype(F32)).astype(BF16)
        logits = (jnp.dot(h_hi, w_hi, preferred_element_type=F32)
                  + (jnp.dot(h_lo, w_hi, preferred_element_type=F32)
                     + jnp.dot(h_hi, w_lo, preferred_element_type=F32)))
        n_exp = rw_ref.shape[1]
        lane = lax.broadcasted_iota(jnp.int32, logits.shape, 1)
        m1 = jnp.max(logits, axis=-1, keepdims=True)
        i1 = jnp.min(jnp.where(logits == m1, lane, n_exp), axis=-1, keepdims=True)
        rest = jnp.where(lane == i1, -jnp.inf, logits)
        m2 = jnp.max(rest, axis=-1, keepdims=True)
        i2 = jnp.min(jnp.where(rest == m2, lane, n_exp), axis=-1, keepdims=True)
        e2 = jnp.exp(m2 - m1)
        idx_ref[...] = jnp.concatenate([i1, i2], axis=1)
        gate_ref[...] = jnp.concatenate([1.0 / (1.0 + e2), e2 / (1.0 + e2)], axis=1)


def _mix_out(parts, w_out_bf, xs, mod, ln_g, ln_b, layer, alpha, n_ctx_blocks, router_w):
    S, D = xs.shape
    GW = parts[0].shape[1]
    L = ln_g.shape[0]
    route = router_w is not None
    h_dtype = F32 if route else BF16
    part = pl.BlockSpec((ROW_BLOCK, GW), lambda i: (i, 0))
    rows = pl.BlockSpec((ROW_BLOCK, D), lambda i: (i, 0))
    vspec = pl.BlockSpec((None, 1, D), lambda i: (layer, 0, 0))
    in_specs = [part, part, part, part,
                pl.BlockSpec((D, D), lambda i: (0, 0)), rows,
                _mod_spec(layer, 2, D, 1), vspec, vspec, _mod_spec(layer, 3, D, 1), _mod_spec(layer, 4, D, 1)]
    args = list(parts) + [w_out_bf, xs, mod, ln_g.reshape(L, 1, D), ln_b.reshape(L, 1, D), mod, mod]
    out_specs = [rows, rows]
    out_shape = [jax.ShapeDtypeStruct((S, D), F32), jax.ShapeDtypeStruct((S, D), h_dtype)]
    if route:
        E = router_w.shape[-1]
        in_specs.append(pl.BlockSpec((D, E), lambda i: (0, 0)))
        args.append(router_w)
        out_specs += [pl.BlockSpec((ROW_BLOCK, TOP_K), lambda i: (i, 0))] * 2
        out_shape += [jax.ShapeDtypeStruct((S, TOP_K), jnp.int32), jax.ShapeDtypeStruct((S, TOP_K), F32)]
    return pl.pallas_call(
        functools.partial(_mixout_kernel, alpha=alpha, n_ctx_blocks=n_ctx_blocks, route=route, h_dtype=h_dtype),
        grid=(S // ROW_BLOCK,),
        in_specs=in_specs,
        out_specs=out_specs,
        out_shape=out_shape,
        compiler_params=_cparams(1),
    )(*args)


FFN_TM = 512
FFN_TF = 512
FFN_TN = 512


def _expert_changed(te_ref, i):
    prev = te_ref[jnp.maximum(i - 1, 0)]
    return jnp.logical_or(i == 0, te_ref[i] != prev)


def _ffn_up_kernel(te_ref, nu_ref, h_ref, w1_ref, w3_ref, o_ref, w1b_ref, w3b_ref):
    i = pl.program_id(1)

    @pl.when(_expert_changed(te_ref, i))
    def _():
        w1b_ref[...] = w1_ref[...].astype(BF16)
        w3b_ref[...] = w3_ref[...].astype(BF16)

    @pl.when(i < nu_ref[0])
    def _():
        h = h_ref[...]
        a = jnp.dot(h, w1b_ref[...], preferred_element_type=F32)
        b = jnp.dot(h, w3b_ref[...], preferred_element_type=F32)
        o_ref[...] = (_silu(a) * b).astype(o_ref.dtype)

    @pl.when(i >= nu_ref[0])
    def _():
        o_ref[...] = jnp.zeros_like(o_ref)


def _ffn_down_kernel(te_ref, nu_ref, g_ref, w2_ref, o_ref, w2b_ref):
    i = pl.program_id(1)

    @pl.when(_expert_changed(te_ref, i))
    def _():
        w2b_ref[...] = w2_ref[...].astype(BF16)

    @pl.when(i < nu_ref[0])
    def _():
        o_ref[...] = jnp.dot(g_ref[...], w2b_ref[...], preferred_element_type=F32)

    @pl.when(i >= nu_ref[0])
    def _():
        o_ref[...] = jnp.zeros_like(o_ref)


def _swiglu_tiles(hs, w1, w3, w2, tile_expert, n_used, tm):
    R, D = hs.shape
    _, _, F = w1.shape
    n_tiles = R // tm
    tf = FFN_TF if F % FFN_TF == 0 else F
    tn = FFN_TN
    assert R % tm == 0 and F % tf == 0 and D % tn == 0
    g = pl.pallas_call(
        _ffn_up_kernel,
        grid_spec=pltpu.PrefetchScalarGridSpec(
            num_scalar_prefetch=2,
            grid=(F // tf, n_tiles),
            in_specs=[pl.BlockSpec((tm, D), lambda j, i, te, nu: (i, 0)),
                      pl.BlockSpec((None, D, tf), lambda j, i, te, nu: (te[i], 0, j)),
                      pl.BlockSpec((None, D, tf), lambda j, i, te, nu: (te[i], 0, j))],
            out_specs=pl.BlockSpec((tm, tf), lambda j, i, te, nu: (i, j)),
            scratch_shapes=[pltpu.VMEM((D, tf), BF16), pltpu.VMEM((D, tf), BF16)]),
        out_shape=jax.ShapeDtypeStruct((R, F), BF16),
        compiler_params=_cparams(2),
    )(tile_expert, n_used, hs, w1, w3)
    return pl.pallas_call(
        _ffn_down_kernel,
        grid_spec=pltpu.PrefetchScalarGridSpec(
            num_scalar_prefetch=2,
            grid=(D // tn, n_tiles),
            in_specs=[pl.BlockSpec((tm, F), lambda j, i, te, nu: (i, 0)),
                      pl.BlockSpec((None, F, tn), lambda j, i, te, nu: (te[i], 0, j))],
            out_specs=pl.BlockSpec((tm, tn), lambda j, i, te, nu: (i, j)),
            scratch_shapes=[pltpu.VMEM((F, tn), BF16)]),
        out_shape=jax.ShapeDtypeStruct((R, D), F32),
        compiler_params=_cparams(2),
    )(tile_expert, n_used, g, w2)


def _row_copy(src_ref, dst_ref, src_row, dst_row, sem):
    return pltpu.make_async_copy(src_ref.at[pl.ds(src_row, 1)], dst_ref.at[pl.ds(dst_row, 1)], sem)


DMA_UNROLL = 8


def _gather_kernel(tok_ref, nu_ref, src_ref, o_ref, buf_ref, sem):
    i = pl.program_id(0)
    tm = buf_ref.shape[0]

    def issue(r, c):
        _row_copy(src_ref, buf_ref, tok_ref[i * tm + r], r, sem).start()
        return c

    def drain(r, c):
        _row_copy(src_ref, buf_ref, 0, r, sem).wait()
        return c

    @pl.when(i < nu_ref[0])
    def _():
        lax.fori_loop(0, tm, issue, 0, unroll=DMA_UNROLL)
        lax.fori_loop(0, tm, drain, 0, unroll=DMA_UNROLL)
        o_ref[...] = buf_ref[...].astype(o_ref.dtype)

    @pl.when(i >= nu_ref[0])
    def _():
        o_ref[...] = jnp.zeros_like(o_ref)


def _gather_rows(src, tok_of_slot, n_used, tm):
    R = tok_of_slot.shape[0]
    D = src.shape[1]
    return pl.pallas_call(
        _gather_kernel,
        grid_spec=pltpu.PrefetchScalarGridSpec(
            num_scalar_prefetch=2,
            grid=(R // tm,),
            in_specs=[pl.BlockSpec(memory_space=pl.ANY)],
            out_specs=pl.BlockSpec((tm, D), lambda i, tok, nu: (i, 0)),
            scratch_shapes=[pltpu.VMEM((tm, D), src.dtype), pltpu.SemaphoreType.DMA(())]),
        out_shape=jax.ShapeDtypeStruct((R, D), BF16),
        compiler_params=_cparams(1),
    )(tok_of_slot, n_used, src)


def _ln2_dense_kernel(x_ref, f_ref, g2_ref, lg_ref, lb_ref, o_ref, *, alpha, n_ctx_blocks, row_off):
    is_ctx = (pl.program_id(0) + row_off) < n_ctx_blocks
    y = alpha * x_ref[...] + _pick(g2_ref[...], is_ctx) * f_ref[...]
    o_ref[...] = _layer_norm(y, lg_ref[...], lb_ref[...])


def _ln2_moe_kernel(sa_ref, sb_ref, x_ref, y_ref, gate_ref, g2_ref, lg_ref, lb_ref, o_ref, bufa_ref, bufb_ref,
                    sem, *, alpha, n_ctx_blocks, row_off):
    i = pl.program_id(0)
    R = ROW_BLOCK
    base = (i + row_off) * R

    def issue(r, c):
        _row_copy(y_ref, bufa_ref, sa_ref[base + r], r, sem).start()
        _row_copy(y_ref, bufb_ref, sb_ref[base + r], r, sem).start()
        return c

    def drain(r, c):
        _row_copy(y_ref, bufa_ref, 0, r, sem).wait()
        _row_copy(y_ref, bufb_ref, 0, r, sem).wait()
        return c

    lax.fori_loop(0, R, issue, 0, unroll=DMA_UNROLL // 2)
    lax.fori_loop(0, R, drain, 0, unroll=DMA_UNROLL // 2)
    gate = gate_ref[...]
    f = gate[:, 0:1] * bufa_ref[...] + gate[:, 1:2] * bufb_ref[...]
    is_ctx = (i + row_off) < n_ctx_blocks
    y = alpha * x_ref[...] + _pick(g2_ref[...], is_ctx) * f
    o_ref[...] = _layer_norm(y, lg_ref[...], lb_ref[...])


def _ln2(x1, f, mod, ln_g, ln_b, layer, alpha, n_ctx_blocks, row_off, moe=None):
    S, D = x1.shape
    L = ln_g.shape[0]
    nb = S // ROW_BLOCK - row_off
    n_pre = 0 if moe is None else 2
    wrap = (lambda f_: (lambda i, *_: f_(i)))
    rows_in = pl.BlockSpec((ROW_BLOCK, D), wrap(lambda i: (i + row_off, 0)))
    rows_out = pl.BlockSpec((ROW_BLOCK, D), wrap(lambda i: (i, 0)))
    vspec = pl.BlockSpec((None, 1, D), wrap(lambda i: (layer, 0, 0)))
    mspec = pl.BlockSpec((None, SUBLANES, D), wrap(lambda i: (layer, 0, 5)))
    common = dict(alpha=alpha, n_ctx_blocks=n_ctx_blocks, row_off=row_off)
    lg, lb = ln_g.reshape(L, 1, D), ln_b.reshape(L, 1, D)
    out_shape = jax.ShapeDtypeStruct((nb * ROW_BLOCK, D), F32)
    if moe is None:
        return pl.pallas_call(
            functools.partial(_ln2_dense_kernel, **common),
            grid=(nb,),
            in_specs=[rows_in, rows_in, mspec, vspec, vspec],
            out_specs=rows_out,
            out_shape=out_shape,
            compiler_params=_cparams(1),
        )(x1, f, mod, lg, lb)
    slot_a, slot_b, gates = moe
    return pl.pallas_call(
        functools.partial(_ln2_moe_kernel, **common),
        grid_spec=pltpu.PrefetchScalarGridSpec(
            num_scalar_prefetch=n_pre,
            grid=(nb,),
            in_specs=[rows_in, pl.BlockSpec(memory_space=pl.ANY),
                      pl.BlockSpec((ROW_BLOCK, TOP_K), wrap(lambda i: (i + row_off, 0))),
                      mspec, vspec, vspec],
            out_specs=rows_out,
            scratch_shapes=[pltpu.VMEM((ROW_BLOCK, D), F32), pltpu.VMEM((ROW_BLOCK, D), F32),
                            pltpu.SemaphoreType.DMA(())]),
        out_shape=out_shape,
        compiler_params=_cparams(1),
    )(slot_a, slot_b, x1, f, gates, mod, lg, lb)


def _route_slots(idx, row0, n_experts, tm):
    S = idx.shape[0]
    n = S - row0
    e_flat = idx[row0:].reshape(-1)
    onehot = (e_flat[:, None] == jnp.arange(n_experts, dtype=jnp.int32)[None, :]).astype(jnp.int32)
    pos = jnp.sum((jnp.cumsum(onehot, axis=0) - 1) * onehot, axis=1)
    counts = jnp.sum(onehot, axis=0)
    padded = ((counts + tm - 1) // tm) * tm
    ends = jnp.cumsum(padded)
    starts = ends - padded
    slot = starts[e_flat] + pos
    n_tiles = -(-(TOP_K * n) // tm) + n_experts
    tok = jnp.repeat(jnp.arange(n, dtype=jnp.int32) + row0, TOP_K)
    tok_of_slot = jnp.full((n_tiles * tm,), row0, jnp.int32).at[slot].set(tok)
    tile_start = jnp.arange(n_tiles, dtype=jnp.int32) * tm
    tile_expert = jnp.minimum(jnp.searchsorted(ends, tile_start, side='right'), n_experts - 1).astype(jnp.int32)
    n_used = (ends[-1] // tm).astype(jnp.int32).reshape(1)
    slot2 = slot.reshape(n, TOP_K).astype(jnp.int32)
    pad = jnp.zeros((row0,), jnp.int32)
    slot_a = jnp.concatenate([pad, slot2[:, 0]])
    slot_b = jnp.concatenate([pad, slot2[:, 1]])
    return tok_of_slot, tile_expert, n_used, slot_a, slot_b


def kernel(x, c, ctx, c_ctx, w_mod, b_mod, w_in, w_out, ln1_g, ln1_b, ln2_g, ln2_b, pool_w, pool_scale,
           diff_lambda, diff_subln_g, conv_dw, conv_db, conv_ln_g, conv_ln_b, conv_pw, s5_a_re, s5_a_im,
           s5_log_dt, s5_b_re, s5_b_im, s5_c_re, s5_c_im, s5_d, s5_glu_w, s5_glu_b, ffn_w1, ffn_w3, ffn_w2,
           router_w, moe_w1, moe_w3, moe_w2):
    B, T, D = x.shape
    Tc = ctx.shape[1]
    depth = w_mod.shape[0]
    assert B == 1 and Tc % ROW_BLOCK == 0 and T % ROW_BLOCK == 0
    GW = D // N_GROUPS
    n_ctx_blocks = Tc // ROW_BLOCK
    alpha = (2.0 * depth) ** 0.25

    cc = jnp.zeros((SUBLANES, D), F32).at[0].set(c[0]).at[1].set(c_ctx)
    mod = _modulation(cc, w_mod, b_mod)
    cos, sin = _rope_tables(T, Tc, LANES)
    xs = jnp.concatenate([ctx[0], x[0]], axis=0)

    POOL_B, Q_B, K_B, V_B, CONV_B, S5_B = 0, 1, 2, 3, 2, 6

    for l in range(depth):
        last = l == depth - 1
        lam_init = 0.8 - 0.6 * math.exp(-0.3 * l)
        z = _in_projection(xs, mod, w_in, l, n_ctx_blocks)

        pa = _pool_mixer(z, pool_w, pool_scale, l, n_ctx_blocks)

        qt, kk, vt, kn2 = _qkv_prep(z, cos, sin, GW, Q_B, K_B, V_B)
        kn = jnp.sqrt(jnp.max(kn2[n_ctx_blocks:, :, 0], axis=0)) * (1.0 + 2.0 ** -6)
        lv = diff_lambda[l].astype(F32)
        lam = jnp.exp(jnp.sum(lv[0] * lv[1])) - jnp.exp(jnp.sum(lv[2] * lv[3])) + lam_init
        lam_row = jnp.full((1, ATTN_TQ), lam, F32)
        g_col = (diff_subln_g[l].astype(F32) * (1.0 - lam_init)).reshape(GW, 1)
        pb = _diff_attention(qt, kk, vt, lam_row, g_col, kn, Tc)

        pcv = _conv_mixer(z, CONV_B, conv_dw, conv_db, conv_ln_g, conv_ln_b, conv_pw, l, n_ctx_blocks)

        seg_len = xs.shape[0] // S5_SEGS
        pf = _s5_params(s5_a_re[l, 0], s5_a_im[l, 0], s5_log_dt[l, 0], s5_b_re[l, 0], s5_b_im[l, 0],
                        s5_c_re[l, 0], s5_c_im[l, 0], seg_len)
        pr = _s5_params(s5_a_re[l, 1], s5_a_im[l, 1], s5_log_dt[l, 1], s5_b_re[l, 1], s5_b_im[l, 1],
                        s5_c_re[l, 1], s5_c_im[l, 1], seg_len)
        yf, yr = _s5_scan(z, S5_B, pf, pr, Tc)
        pd = _s5_glu(yf, yr, z, S5_B, s5_d, s5_glu_w, s5_glu_b, l, n_ctx_blocks)

        w_out_bf = _cast_bf16(w_out, l)
        row_off = n_ctx_blocks if last else 0
        if l % 2 == 0:
            x1, h = _mix_out((pa, pb, pcv, pd), w_out_bf, xs, mod, ln1_g, ln1_b, l, alpha, n_ctx_blocks, None)
            S = xs.shape[0]
            tm = next(t for t in (768, 512, ROW_BLOCK) if S % t == 0)
            n_tiles = S // tm
            f = _swiglu_tiles(h, ffn_w1[l // 2][None], ffn_w3[l // 2][None], ffn_w2[l // 2][None],
                              jnp.zeros((n_tiles,), jnp.int32), jnp.full((1,), n_tiles, jnp.int32), tm)
            xs_new = _ln2(x1, f, mod, ln2_g, ln2_b, l, alpha, n_ctx_blocks, row_off)
        else:
            x1, h, idx, gates = _mix_out((pa, pb, pcv, pd), w_out_bf, xs, mod, ln1_g, ln1_b, l, alpha,
                                         n_ctx_blocks, router_w[l // 2])
            n_exp = router_w.shape[-1]
            row0 = row_off * ROW_BLOCK
            tok_of_slot, tile_expert, n_used, slot_a, slot_b = _route_slots(idx, row0, n_exp, FFN_TM)
            hs = _gather_rows(h, tok_of_slot, n_used, FFN_TM)
            y = _swiglu_tiles(hs, moe_w1[l // 2], moe_w3[l // 2], moe_w2[l // 2], tile_expert, n_used, FFN_TM)
            xs_new = _ln2(x1, y, mod, ln2_g, ln2_b, l, alpha, n_ctx_blocks, row_off, moe=(slot_a, slot_b, gates))
        xs = xs_new
    return xs[None]
```

```python
import functools
import math

import numpy as np
import jax
import jax.numpy as jnp
from jax import lax
from jax.experimental import pallas as pl
from jax.experimental.pallas import tpu as pltpu

F32 = jnp.float32
BF16 = jnp.bfloat16

GRID_W = 64
N_GROUPS = 4
POOL_WINDOWS = (2, 4, 8, 16)
DIFF_HEADS = 8
DIFF_QK = 32
CONV_WIDTH = 31
S5_P = 16
S5_N = 64
TOP_K = 2
ROPE_BASE = 10000.0
LN_EPS = 1e-5

LANES = 128
SUBLANES = 8
ROW_BLOCK = 256
VMEM_LIMIT = 56 * 1024 * 1024


def _cparams(n_axes, vmem=VMEM_LIMIT):
    return pltpu.CompilerParams(dimension_semantics=("arbitrary",) * n_axes, vmem_limit_bytes=vmem)


def _layer_norm(y, g, b):
    mu = jnp.mean(y, -1, keepdims=True)
    yc = y - mu
    var = jnp.mean(yc * yc, -1, keepdims=True)
    return yc * lax.rsqrt(var + LN_EPS) * g + b


def _silu(x):
    return x * jax.nn.sigmoid(x)


def _mod_kernel(cc_ref, w_ref, b_ref, o_ref):
    a = _silu(cc_ref[...])
    o_ref[...] = jnp.dot(a.astype(BF16), w_ref[...].astype(BF16), preferred_element_type=F32) + b_ref[...]


def _modulation(cc, w_mod, b_mod):
    L, D, N = w_mod.shape
    tn = 1536
    assert N % tn == 0
    return pl.pallas_call(
        _mod_kernel,
        grid=(L, N // tn),
        in_specs=[pl.BlockSpec((SUBLANES, D), lambda l, j: (0, 0)),
                  pl.BlockSpec((None, D, tn), lambda l, j: (l, 0, j)),
                  pl.BlockSpec((None, 1, tn), lambda l, j: (l, 0, j))],
        out_specs=pl.BlockSpec((None, SUBLANES, tn), lambda l, j: (l, 0, j)),
        out_shape=jax.ShapeDtypeStruct((L, SUBLANES, N), F32),
        compiler_params=_cparams(2),
    )(cc, w_mod, b_mod.reshape(L, 1, N))


def _mod_spec(layer, chunk, D, n_grid_axes):
    if n_grid_axes == 1:
        return pl.BlockSpec((None, SUBLANES, D), lambda i: (layer, 0, chunk))
    return pl.BlockSpec((None, SUBLANES, D), lambda j, i: (layer, 0, chunk))


def _pick(m, is_ctx):
    return jnp.where(is_ctx, m[1:2, :], m[0:1, :])


def _inproj_kernel(x_ref, sh_ref, sc_ref, w_ref, o_ref, wb_ref, *, n_ctx_blocks):
    i = pl.program_id(1)

    @pl.when(i == 0)
    def _():
        wb_ref[...] = w_ref[...].astype(BF16)

    is_ctx = i < n_ctx_blocks
    h = x_ref[...] * (1.0 + _pick(sc_ref[...], is_ctx)) + _pick(sh_ref[...], is_ctx)
    o_ref[...] = jnp.dot(h.astype(BF16), wb_ref[...], preferred_element_type=F32).astype(o_ref.dtype)


def _in_projection(xs, mod, w_in, layer, n_ctx_blocks):
    S, D = xs.shape
    N = w_in.shape[-1]
    tn = 1792
    assert N % tn == 0 and S % ROW_BLOCK == 0
    return pl.pallas_call(
        functools.partial(_inproj_kernel, n_ctx_blocks=n_ctx_blocks),
        grid=(N // tn, S // ROW_BLOCK),
        in_specs=[pl.BlockSpec((ROW_BLOCK, D), lambda j, i: (i, 0)),
                  _mod_spec(layer, 0, D, 2),
                  _mod_spec(layer, 1, D, 2),
                  pl.BlockSpec((None, D, tn), lambda j, i: (layer, 0, j))],
        out_specs=pl.BlockSpec((ROW_BLOCK, tn), lambda j, i: (i, j)),
        out_shape=jax.ShapeDtypeStruct((S, N), BF16),
        scratch_shapes=[pltpu.VMEM((D, tn), BF16)],
        compiler_params=_cparams(2),
    )(xs, mod, mod, w_in)


def _seq_edges(i, n_blocks, n_ctx_blocks):
    prev_ok = jnp.logical_and(i != 0, i != n_ctx_blocks)
    next_ok = jnp.logical_and(i != n_ctx_blocks - 1, i != n_blocks - 1)
    return prev_ok, next_ok


def _pool_kernel(p_ref, c_ref, n_ref, w_ref, scale_ref, o_ref, ext_ref, *, n_blocks, n_ctx_blocks):
    i = pl.program_id(0)
    R = ROW_BLOCK
    halo = SUBLANES
    prev_ok, next_ok = _seq_edges(i, n_blocks, n_ctx_blocks)
    cur = c_ref[...].astype(F32)
    pack = 2 * SUBLANES
    ext_ref[0:halo, :] = jnp.where(prev_ok, p_ref[R - pack:R, :].astype(F32)[pack - halo:], 0.0)
    ext_ref[halo:halo + R, :] = cur
    ext_ref[halo + R:halo + R + halo, :] = jnp.where(next_ok, n_ref[0:pack, :].astype(F32)[:halo], 0.0)
    rloc = lax.broadcasted_iota(jnp.int32, (R, 1), 0)
    gw = cur.shape[1] // len(POOL_WINDOWS)
    outs = []
    for g, w in enumerate(POOL_WINDOWS):
        acc = jnp.zeros((R, gw), F32)
        cnt = jnp.zeros((R, 1), F32)
        for d in range(-(w // 2), w - w // 2):
            acc = acc + ext_ref[halo + d:halo + d + R, g * gw:(g + 1) * gw]
            valid = jnp.logical_and(jnp.logical_or(rloc + d >= 0, prev_ok),
                                    jnp.logical_or(rloc + d < R, next_ok))
            cnt = cnt + valid.astype(F32)
        diff = acc / cnt - cur[:, g * gw:(g + 1) * gw]
        outs.append(jnp.dot(diff.astype(BF16), w_ref[g].astype(BF16), preferred_element_type=F32))
    o_ref[...] = jnp.concatenate(outs, axis=-1) * scale_ref[...]


def _pool_mixer(z, pool_w, pool_scale, layer, n_ctx_blocks):
    S = z.shape[0]
    nb = S // ROW_BLOCK
    GW = pool_scale.shape[-1]
    G, gw = pool_w.shape[1], pool_w.shape[2]
    return pl.pallas_call(
        functools.partial(_pool_kernel, n_blocks=nb, n_ctx_blocks=n_ctx_blocks),
        grid=(nb,),
        in_specs=[pl.BlockSpec((ROW_BLOCK, GW), lambda i: (jnp.maximum(i - 1, 0), 0)),
                  pl.BlockSpec((ROW_BLOCK, GW), lambda i: (i, 0)),
                  pl.BlockSpec((ROW_BLOCK, GW), lambda i: (jnp.minimum(i + 1, nb - 1), 0)),
                  pl.BlockSpec((None, G, gw, gw), lambda i: (layer, 0, 0, 0)),
                  pl.BlockSpec((None, 1, GW), lambda i: (layer, 0, 0))],
        out_specs=pl.BlockSpec((ROW_BLOCK, GW), lambda i: (i, 0)),
        out_shape=jax.ShapeDtypeStruct((S, GW), F32),
        scratch_shapes=[pltpu.VMEM((ROW_BLOCK + 2 * SUBLANES, GW), F32)],
        compiler_params=_cparams(1),
    )(z, z, z, pool_w, pool_scale.reshape(pool_scale.shape[0], 1, GW))


CONV_HALO = 16


def _conv_kernel(p_ref, c_ref, n_ref, dw_ref, db_ref, g_ref, b_ref, pw_ref, o_ref, ext_ref,
                 *, n_blocks, n_ctx_blocks):
    i = pl.program_id(0)
    R = ROW_BLOCK
    H = CONV_HALO
    GW = o_ref.shape[1]
    prev_ok, next_ok = _seq_edges(i, n_blocks, n_ctx_blocks)

    def glu(u):
        u = u.astype(F32)
        return u[:, :GW] * jax.nn.sigmoid(u[:, GW:])

    ext_ref[0:H, :] = jnp.where(prev_ok, glu(p_ref[R - H:R, :]), 0.0)
    ext_ref[H:H + R, :] = glu(c_ref[...])
    ext_ref[H + R:H + R + H, :] = jnp.where(next_ok, glu(n_ref[0:H, :]), 0.0)
    off = H - CONV_WIDTH // 2
    acc = jnp.zeros((R, GW), F32)
    for j in range(CONV_WIDTH):
        acc = acc + ext_ref[off + j:off + j + R, :] * dw_ref[j:j + 1, :]
    y = _layer_norm(acc + db_ref[...], g_ref[...], b_ref[...])
    o_ref[...] = jnp.dot(_silu(y).astype(BF16), pw_ref[...].astype(BF16), preferred_element_type=F32)


def _conv_mixer(z, col_block, conv_dw, conv_db, conv_ln_g, conv_ln_b, conv_pw, layer, n_ctx_blocks):
    S = z.shape[0]
    nb = S // ROW_BLOCK
    GW = conv_db.shape[-1]
    L = conv_db.shape[0]
    vec = lambda a: a.reshape(L, 1, GW)
    vspec = pl.BlockSpec((None, 1, GW), lambda i: (layer, 0, 0))
    return pl.pallas_call(
        functools.partial(_conv_kernel, n_blocks=nb, n_ctx_blocks=n_ctx_blocks),
        grid=(nb,),
        in_specs=[pl.BlockSpec((ROW_BLOCK, 2 * GW), lambda i: (jnp.maximum(i - 1, 0), col_block)),
                  pl.BlockSpec((ROW_BLOCK, 2 * GW), lambda i: (i, col_block)),
                  pl.BlockSpec((ROW_BLOCK, 2 * GW), lambda i: (jnp.minimum(i + 1, nb - 1), col_block)),
                  pl.BlockSpec((None, CONV_WIDTH, GW), lambda i: (layer, 0, 0)),
                  vspec, vspec, vspec,
                  pl.BlockSpec((None, GW, GW), lambda i: (layer, 0, 0))],
        out_specs=pl.BlockSpec((ROW_BLOCK, GW), lambda i: (i, 0)),
        out_shape=jax.ShapeDtypeStruct((S, GW), F32),
        scratch_shapes=[pltpu.VMEM((ROW_BLOCK + 2 * CONV_HALO, GW), F32)],
        compiler_params=_cparams(1),
    )(z, z, z, conv_dw, vec(conv_db), vec(conv_ln_g), vec(conv_ln_b), conv_pw)


def _rope_tables(T, Tc, width):
    ax = DIFF_QK // 2
    inv = ROPE_BASE ** (-jnp.arange(0, ax, 2, dtype=F32) / ax)
    t = jnp.arange(T)
    row = (t // GRID_W).astype(F32)
    col = (t % GRID_W).astype(F32)
    ang = jnp.stack([row[:, None] * inv, col[:, None] * inv], axis=1)
    cos = jnp.cos(ang)[:, :, None, :]
    sin = jnp.sin(ang)[:, :, None, :]
    cos = jnp.broadcast_to(cos, (T, 2, 2, ax // 2)).reshape(T, DIFF_QK)
    sin = jnp.concatenate([-sin, sin], axis=2).reshape(T, DIFF_QK)
    reps = width // DIFF_QK
    cos = jnp.concatenate([jnp.ones((Tc, DIFF_QK), F32), cos], axis=0)
    sin = jnp.concatenate([jnp.zeros((Tc, DIFF_QK), F32), sin], axis=0)
    return jnp.tile(cos, (1, reps)), jnp.tile(sin, (1, reps))


def _qkv_prep_kernel(q_ref, k_ref, v_ref, cos_ref, sin_ref, qo_ref, ko_ref, vo_ref, kn_ref):
    W = q_ref.shape[1]
    half = DIFF_QK // 4
    lane = lax.broadcasted_iota(jnp.int32, (1, W), 1)
    first = (lane % (2 * half)) < half
    cos = jnp.concatenate([cos_ref[...]] * (W // LANES), axis=1)
    sin = jnp.concatenate([sin_ref[...]] * (W // LANES), axis=1)

    def rope(x):
        partner = jnp.where(first, pltpu.roll(x, W - half, 1), pltpu.roll(x, half, 1))
        return x * cos + partner * sin

    qo_ref[...] = (rope(q_ref[...].astype(F32)) * (DIFF_QK ** -0.5 * math.log2(math.e))).T.astype(BF16)
    kb = rope(k_ref[...].astype(F32)).astype(BF16)
    ko_ref[...] = kb
    ksq = kb.astype(F32).T
    ksq = ksq * ksq
    for grp in range(W // DIFF_QK):
        n2 = jnp.sum(ksq[grp * DIFF_QK:(grp + 1) * DIFF_QK, :], axis=0, keepdims=True)
        kn_ref[grp:grp + 1, :] = jnp.broadcast_to(jnp.max(n2, axis=1, keepdims=True), (1, LANES))
    vt = v_ref[...].astype(F32).T.astype(BF16)
    dv = LANES // 2
    ones = jnp.ones((ATTN_VROWS - dv, vt.shape[1]), BF16)
    for h in range(W // dv):
        vo_ref[h * ATTN_VROWS:h * ATTN_VROWS + dv, :] = vt[h * dv:(h + 1) * dv, :]
        vo_ref[h * ATTN_VROWS + dv:(h + 1) * ATTN_VROWS, :] = ones


def _qkv_prep(z, cos, sin, W, q_blk, k_blk, v_blk):
    S = z.shape[0]
    nb = S // ROW_BLOCK
    row = lambda c: pl.BlockSpec((ROW_BLOCK, W), lambda i: (i, c))
    tab = pl.BlockSpec((ROW_BLOCK, LANES), lambda i: (i, 0))
    return pl.pallas_call(
        _qkv_prep_kernel,
        grid=(nb,),
        in_specs=[row(q_blk), row(k_blk), row(v_blk), tab, tab],
        out_specs=[pl.BlockSpec((W, ROW_BLOCK), lambda i: (0, i)), row(0),
                   pl.BlockSpec((DIFF_HEADS * ATTN_VROWS, ROW_BLOCK), lambda i: (0, i)),
                   pl.BlockSpec((None, W // DIFF_QK, LANES), lambda i: (i, 0, 0))],
        out_shape=[jax.ShapeDtypeStruct((W, S), BF16), jax.ShapeDtypeStruct((S, W), BF16),
                   jax.ShapeDtypeStruct((DIFF_HEADS * ATTN_VROWS, S), BF16),
                   jax.ShapeDtypeStruct((nb, W // DIFF_QK, LANES), F32)],
        compiler_params=_cparams(1),
    )(z, z, z, cos, sin)


ATTN_TQ = 256
ATTN_VROWS = LANES // 2 + 2 * SUBLANES
ATTN_MARGIN = 64.0
ATTN_TK = (4096, 2048, 1024)


def _attn_kernel(qt_ref, k_ref, vt_ref, lam_ref, g_ref, kn_ref, o_ref, qq_ref, m_ref, acc_ref,
                 *, n_ctx, n_ctx_blocks, n_lat_chunks, tk):
    i = pl.program_id(1)
    tq = ATTN_TQ
    dv = LANES // 2
    qt = qt_ref[...]
    feat = lax.broadcasted_iota(jnp.int32, (LANES, 1), 0)
    zero = jnp.zeros_like(qt)
    for hh in range(2):
        for comp in range(2):
            lo = hh * dv + comp * DIFF_QK
            keep = jnp.logical_and(feat >= lo, feat < lo + DIFF_QK)
            qq_ref[hh, :, comp * tq:(comp + 1) * tq] = jnp.where(keep, qt, zero)

    def attend(start, size, mode):
        kk = k_ref[pl.ds(start, size), :]
        for hh in range(2):
            s = jnp.dot(kk, qq_ref[hh], preferred_element_type=F32)
            vv = vt_ref[hh * ATTN_VROWS:(hh + 1) * ATTN_VROWS, pl.ds(start, size)]
            mx = jnp.max(s, axis=0, keepdims=True)
            if mode == "first":
                m_ref[hh] = mx
                p = jnp.exp2(s - mx)
                acc_ref[hh] = jnp.dot(vv, p.astype(BF16), preferred_element_type=F32)
            elif mode == "exact":
                m_old = m_ref[hh]
                m_new = jnp.maximum(m_old, mx)
                m_ref[hh] = m_new
                p = jnp.exp2(s - m_new)
                acc_ref[hh] = (jnp.exp2(m_old - m_new) * acc_ref[hh]
                               + jnp.dot(vv, p.astype(BF16), preferred_element_type=F32))
            else:
                m_old = m_ref[hh]
                p = jnp.exp2(s - m_old)
                m_new = jnp.maximum(m_old, mx)
                m_ref[hh] = m_new
                acc_ref[hh] = (jnp.exp2(m_old - m_new)
                               * (acc_ref[hh] + jnp.dot(vv, p.astype(BF16), preferred_element_type=F32)))

    attend(0, n_ctx, "first")
    n_steps = jnp.where(i < n_ctx_blocks, 0, n_lat_chunks)

    pair = pl.program_id(0)
    col = lax.broadcasted_iota(jnp.int32, (1, 2 * tq), 1)
    excess = jnp.full((1, 2 * tq), -jnp.inf, F32)
    for hh in range(2):
        qf = qq_ref[hh].astype(F32)
        qn = jnp.sqrt(jnp.sum(qf * qf, axis=0, keepdims=True))
        grp = (2 * pair + hh) * 2
        kn = jnp.where(col < tq, kn_ref[grp], kn_ref[grp + 1])
        excess = jnp.maximum(excess, qn * kn - m_ref[hh])
    safe = jnp.max(excess) < ATTN_MARGIN

    def loop(mode):
        def body(c, carry):
            attend(pl.multiple_of(n_ctx + c * tk, LANES), tk, mode)
            return carry
        lax.fori_loop(0, n_steps, body, 0)

    @pl.when(safe)
    def _():
        loop("deferred")

    @pl.when(jnp.logical_not(safe))
    def _():
        loop("exact")

    lam = lam_ref[...]
    outs = []
    for hh in range(2):
        acc = acc_ref[hh]
        ratio = acc[:dv] / acc[dv:dv + 1]
        o = ratio[:, :tq] - lam * ratio[:, tq:]
        r = lax.rsqrt(jnp.sum(o * o, axis=0, keepdims=True) / dv + LN_EPS)
        outs.append(o * r)
    o_ref[...] = (jnp.concatenate(outs, axis=0) * g_ref[...]).T


def _diff_attention(qt, k, vt, lam, g, kn, n_ctx):
    W, S = qt.shape
    assert S % ATTN_TQ == 0 and n_ctx % ATTN_TQ == 0
    tk = next(t for t in ATTN_TK if (S - n_ctx) % t == 0)
    nq = S // ATTN_TQ
    return pl.pallas_call(
        functools.partial(_attn_kernel, n_ctx=n_ctx, n_ctx_blocks=n_ctx // ATTN_TQ,
                          n_lat_chunks=(S - n_ctx) // tk, tk=tk),
        grid=(W // LANES, nq),
        in_specs=[pl.BlockSpec((LANES, ATTN_TQ), lambda p, i: (p, i)),
                  pl.BlockSpec((S, LANES), lambda p, i: (0, p)),
                  pl.BlockSpec((2 * ATTN_VROWS, S), lambda p, i: (p, 0)),
                  pl.BlockSpec((1, ATTN_TQ), lambda p, i: (0, 0)),
                  pl.BlockSpec((LANES, 1), lambda p, i: (p, 0)),
                  pl.BlockSpec(memory_space=pltpu.SMEM)],
        out_specs=pl.BlockSpec((ATTN_TQ, LANES), lambda p, i: (i, p)),
        out_shape=jax.ShapeDtypeStruct((S, W), F32),
        scratch_shapes=[pltpu.VMEM((2, LANES, 2 * ATTN_TQ), BF16),
                        pltpu.VMEM((2, 1, 2 * ATTN_TQ), F32),
                        pltpu.VMEM((2, ATTN_VROWS, 2 * ATTN_TQ), F32)],
        compiler_params=_cparams(2),
    )(qt, k, vt, lam, g, kn)


S5_SEGS = SUBLANES
S5_KB = 32
S5_GB = 8


def _s5_params(a_re, a_im, log_dt, b_re, b_im, c_re, c_im, seg_len):
    G, N = a_re.shape
    P = b_re.shape[-1]
    nblk = G // S5_GB
    a_re, a_im = a_re.astype(F32), a_im.astype(F32)
    dt = jnp.exp(log_dt.astype(F32))[:, None]
    lr, li = dt * a_re, dt * a_im
    mag = jnp.exp(lr)
    ar, ai = mag * jnp.cos(li), mag * jnp.sin(li)
    den = a_re * a_re + a_im * a_im
    qr = ((ar - 1.0) * a_re + ai * a_im) / den
    qi = (ai * a_re - (ar - 1.0) * a_im) / den
    b_re, b_im = b_re.astype(F32), b_im.astype(F32)
    br = qr[..., None] * b_re - qi[..., None] * b_im
    bi = qr[..., None] * b_im + qi[..., None] * b_re
    mag_l = jnp.exp(seg_len * lr)
    alr, ali = mag_l * jnp.cos(seg_len * li), mag_l * jnp.sin(seg_len * li)
    eye = jnp.eye(S5_GB, dtype=F32)
    wb = lambda m: jnp.einsum('gh,bgnp->bgphn', eye, m.reshape(nblk, S5_GB, N, P)).reshape(
        nblk, S5_GB * P, S5_GB * N)
    w_in = jnp.concatenate([wb(br), wb(bi)], axis=2)
    cm = lambda m: jnp.einsum('gh,bgpn->bhngp', eye, m.astype(F32).reshape(nblk, S5_GB, P, N)).reshape(
        nblk, S5_GB * N, S5_GB * P)
    w_out = jnp.concatenate([cm(c_re), -cm(c_im)], axis=1)
    row = lambda r, i: jnp.concatenate([r.reshape(1, G * N), i.reshape(1, G * N)], axis=1)
    coef = jnp.broadcast_to(row(ar, ai), (S5_SEGS, 2 * G * N))
    return w_in.astype(BF16), coef, row(alr, ali), w_out.astype(BF16)


def _s5_kernel(*refs, emit_out):
    n_seg = S5_SEGS
    uf_ref = refs[0]
    ur_refs = refs[1:1 + n_seg]
    rest = refs[1 + n_seg:]
    if emit_out:
        (wbf_ref, wbr_ref, af_ref, ar_ref, ef_ref, er_ref, alf_ref, alr_ref, cf_ref, cr_ref,
         yf_ref, yr_ref, stage_ref, bf_ref, br_ref, hf_ref, hr_ref) = rest
    else:
        (wbf_ref, wbr_ref, af_ref, ar_ref, ef_out_ref, er_out_ref,
         stage_ref, bf_ref, br_ref, hf_ref, hr_ref) = rest
    g = pl.program_id(0)
    KB = S5_KB
    R = n_seg * KB
    NS = af_ref.shape[1] // 2
    nblk = wbf_ref.shape[0]
    wi = wbf_ref.shape[1]
    ws = wbf_ref.shape[2] // 2

    def cmul_add(a_row, h, add):
        are, aim = a_row[:, :NS], a_row[:, NS:]
        hre, him = h[:, :NS], h[:, NS:]
        return jnp.concatenate([are * hre - aim * him + add[:, :NS], are * him + aim * hre + add[:, NS:]], axis=1)

    @pl.when(g == 0)
    def _():
        if emit_out:
            def chain(e_ref, al_ref, order):
                al = al_ref[...]
                c = jnp.zeros((1, 2 * NS), F32)
                rows = [None] * n_seg
                for s in order:
                    rows[s] = c
                    c = cmul_add(al, c, e_ref[s:s + 1, :])
                return jnp.concatenate(rows, axis=0)
            hf_ref[...] = chain(ef_ref, alf_ref, range(n_seg))
            hr_ref[...] = chain(er_ref, alr_ref, range(n_seg - 1, -1, -1))
        else:
            hf_ref[...] = jnp.zeros_like(hf_ref)
            hr_ref[...] = jnp.zeros_like(hr_ref)

    def interleaved(load_seg):
        for s in range(n_seg):
            blk = load_seg(s).astype(F32)
            for c in range(nblk):
                stage_ref[c, s * KB:(s + 1) * KB, :] = blk[:, c * wi:(c + 1) * wi]
        rows = [jnp.concatenate([stage_ref[c, pl.ds(kk, n_seg, stride=KB), :] for c in range(nblk)], axis=1)
                for kk in range(KB)]
        return jnp.concatenate(rows, axis=0).astype(BF16)

    def project_in(u, w_ref, buf_ref):
        for b in range(nblk):
            res = jnp.dot(u[:, b * wi:(b + 1) * wi], w_ref[b], preferred_element_type=F32)
            buf_ref[:, b * ws:(b + 1) * ws] = res[:, :ws]
            buf_ref[:, NS + b * ws:NS + (b + 1) * ws] = res[:, ws:]

    def scan(buf_ref, a_ref, h_ref, reverse):
        def step(t, h):
            kk = (KB - 1 - t) if reverse else t
            r0 = pl.multiple_of(kk * n_seg, n_seg)
            new = cmul_add(a_ref[...], h, buf_ref[pl.ds(r0, n_seg), :])
            if emit_out:
                buf_ref[pl.ds(r0, n_seg), :] = new
            return new
        h_ref[...] = lax.fori_loop(0, KB, step, h_ref[...])

    def project_out(buf_ref, c_ref, y_ref):
        for b in range(nblk):
            hcat = jnp.concatenate([buf_ref[:, b * ws:(b + 1) * ws], buf_ref[:, NS + b * ws:NS + (b + 1) * ws]],
                                   axis=1).astype(BF16)
            stage_ref[b] = jnp.dot(hcat, c_ref[b], preferred_element_type=F32)
        for s in range(n_seg):
            y_ref[s] = jnp.concatenate([stage_ref[c, pl.ds(s, KB, stride=n_seg), :] for c in range(nblk)], axis=1)

    project_in(interleaved(lambda s: uf_ref[s]), wbf_ref, bf_ref)
    scan(bf_ref, af_ref, hf_ref, False)
    if emit_out:
        project_out(bf_ref, cf_ref, yf_ref)
    project_in(interleaved(lambda s: ur_refs[s][...]), wbr_ref, br_ref)
    scan(br_ref, ar_ref, hr_ref, True)
    if emit_out:
        project_out(br_ref, cr_ref, yr_ref)
    else:
        ef_out_ref[...] = hf_ref[...]
        er_out_ref[...] = hr_ref[...]


def _s5_pass(z, col_block, pf, pr, ends, n_ctx):
    S, NZ = z.shape
    GW = pf[0].shape[1] * pf[0].shape[0]
    NS2 = pf[1].shape[1]
    seg_len = S // S5_SEGS
    steps = seg_len // S5_KB
    nblocks = S // S5_KB
    assert S % (S5_SEGS * S5_KB) == 0 and n_ctx % S5_KB == 0
    ctx_blocks = n_ctx // S5_KB
    emit_out = ends is not None
    z4 = z.reshape(S5_SEGS, steps, S5_KB, NZ)
    z3 = z.reshape(nblocks, S5_KB, NZ)

    def rev_spec(s):
        return pl.BlockSpec((None, S5_KB, GW),
                            lambda g: ((s * steps + steps - 1 - g + ctx_blocks) % nblocks, 0, col_block))

    const = lambda a: pl.BlockSpec(a.shape, lambda g: (0,) * a.ndim)
    in_specs = [pl.BlockSpec((S5_SEGS, None, S5_KB, GW), lambda g: (0, g, 0, col_block))]
    in_specs += [rev_spec(s) for s in range(S5_SEGS)]
    args = [z4] + [z3] * S5_SEGS
    weights = [pf[0], pr[0], pf[1], pr[1]]
    if emit_out:
        weights += [ends[0], ends[1], pf[2], pr[2], pf[3], pr[3]]
    in_specs += [const(a) for a in weights]
    args += weights
    scratch = [pltpu.VMEM((pf[0].shape[0], S5_SEGS * S5_KB, pf[0].shape[1]), F32),
               pltpu.VMEM((S5_SEGS * S5_KB, NS2), F32), pltpu.VMEM((S5_SEGS * S5_KB, NS2), F32),
               pltpu.VMEM((S5_SEGS, NS2), F32), pltpu.VMEM((S5_SEGS, NS2), F32)]
    if emit_out:
        yshape = jax.ShapeDtypeStruct((S5_SEGS, steps, S5_KB, GW), F32)
        out_shape = [yshape, yshape]
        out_specs = [pl.BlockSpec((S5_SEGS, None, S5_KB, GW), lambda g: (0, g, 0, 0)),
                     pl.BlockSpec((S5_SEGS, None, S5_KB, GW), lambda g: (0, steps - 1 - g, 0, 0))]
    else:
        eshape = jax.ShapeDtypeStruct((S5_SEGS, NS2), F32)
        out_shape = [eshape, eshape]
        out_specs = [pl.BlockSpec((S5_SEGS, NS2), lambda g: (0, 0))] * 2
    return pl.pallas_call(
        functools.partial(_s5_kernel, emit_out=emit_out),
        grid=(steps,),
        in_specs=in_specs,
        out_specs=out_specs,
        out_shape=out_shape,
        scratch_shapes=scratch,
        compiler_params=_cparams(1),
    )(*args)


def _s5_scan(z, col_block, pf, pr, n_ctx):
    S = z.shape[0]
    GW = pf[0].shape[1] * pf[0].shape[0]
    ends = _s5_pass(z, col_block, pf, pr, None, n_ctx)
    yf, yr = _s5_pass(z, col_block, pf, pr, ends, n_ctx)
    return yf.reshape(S, GW), yr.reshape(S, GW)


def _s5_glu_kernel(yf_ref, yr_ref, u_ref, d_ref, w_ref, b_ref, o_ref):
    y = yf_ref[...] + yr_ref[...] + d_ref[...] * u_ref[...].astype(F32)
    zz = jax.nn.gelu(y)
    gate = jnp.dot(zz.astype(BF16), w_ref[...].astype(BF16), preferred_element_type=F32) + b_ref[...]
    o_ref[...] = zz * jax.nn.sigmoid(gate)


def _s5_glu(yf, yr, z, col_block, s5_d, glu_w, glu_b, layer, n_ctx_blocks):
    S, GW = yf.shape
    L = s5_d.shape[0]
    nb = S // ROW_BLOCK
    row = lambda c: pl.BlockSpec((ROW_BLOCK, GW), lambda i: (i, c))
    rot = pl.BlockSpec((ROW_BLOCK, GW), lambda i: ((i + nb - n_ctx_blocks) % nb, 0))
    vspec = pl.BlockSpec((None, 1, GW), lambda i: (layer, 0, 0))
    return pl.pallas_call(
        _s5_glu_kernel,
        grid=(nb,),
        in_specs=[row(0), rot, row(col_block), vspec,
                  pl.BlockSpec((None, GW, GW), lambda i: (layer, 0, 0)), vspec],
        out_specs=row(0),
        out_shape=jax.ShapeDtypeStruct((S, GW), F32),
        compiler_params=_cparams(1),
    )(yf, yr, z, s5_d.reshape(L, 1, GW), glu_w, glu_b.reshape(L, 1, GW))


def _cast_kernel(x_ref, o_ref):
    o_ref[...] = x_ref[...].astype(o_ref.dtype)


def _cast_bf16(w, layer):
    _, K, N = w.shape
    tk = 512
    return pl.pallas_call(
        _cast_kernel,
        grid=(K // tk,),
        in_specs=[pl.BlockSpec((None, tk, N), lambda i: (layer, i, 0))],
        out_specs=pl.BlockSpec((tk, N), lambda i: (i, 0)),
        out_shape=jax.ShapeDtypeStruct((K, N), BF16),
        compiler_params=_cparams(1),
    )(w)


def _mixout_kernel(*refs, alpha, n_ctx_blocks, route, h_dtype):
    (pa_ref, pb_ref, pc_ref, pd_ref, w_ref, x_ref, g1_ref, lg_ref, lb_ref, sh_ref, sc_ref) = refs[:11]
    if route:
        rw_ref, x1_ref, h_ref, idx_ref, gate_ref = refs[11:]
    else:
        x1_ref, h_ref = refs[11:]
    i = pl.program_id(0)
    is_ctx = i < n_ctx_blocks
    GW = pa_ref.shape[1]
    mix = jnp.zeros(x_ref.shape, F32)
    for k, p_ref in enumerate((pa_ref, pb_ref, pc_ref, pd_ref)):
        mix = mix + jnp.dot(p_ref[...].astype(BF16), w_ref[k * GW:(k + 1) * GW, :],
                            preferred_element_type=F32)
    y = alpha * x_ref[...] + _pick(g1_ref[...], is_ctx) * mix
    x1 = _layer_norm(y, lg_ref[...], lb_ref[...])
    x1_ref[...] = x1
    h = x1 * (1.0 + _pick(sc_ref[...], is_ctx)) + _pick(sh_ref[...], is_ctx)
    h_ref[...] = h.astype(h_dtype)
    if route:
        rw = rw_ref[...]
        h_hi = h.astype(BF16)
        h_lo = (h - h_hi.astype(F32)).astype(BF16)
        w_hi = rw.astype(BF16)
        w_lo = (rw - w_hi.astype(F32)).astype(BF16)
        logits = (jnp.dot(h_hi, w_hi, preferred_element_type=F32)
                  + (jnp.dot(h_lo, w_hi, preferred_element_type=F32)
                     + jnp.dot(h_hi, w_lo, preferred_element_type=F32)))
        n_exp = rw_ref.shape[1]
        lane = lax.broadcasted_iota(jnp.int32, logits.shape, 1)
        m1 = jnp.max(logits, axis=-1, keepdims=True)
        i1 = jnp.min(jnp.where(logits == m1, lane, n_exp), axis=-1, keepdims=True)
        rest = jnp.where(lane == i1, -jnp.inf, logits)
        m2 = jnp.max(rest, axis=-1, keepdims=True)
        i2 = jnp.min(jnp.where(rest == m2, lane, n_exp), axis=-1, keepdims=True)
        e2 = jnp.exp(m2 - m1)
        idx_ref[...] = jnp.concatenate([i1, i2], axis=1)
        gate_ref[...] = jnp.concatenate([1.0 / (1.0 + e2), e2 / (1.0 + e2)], axis=1)


def _mix_out(parts, w_out_bf, xs, mod, ln_g, ln_b, layer, alpha, n_ctx_blocks, router_w):
    S, D = xs.shape
    GW = parts[0].shape[1]
    L = ln_g.shape[0]
    route = router_w is not None
    h_dtype = F32 if route else BF16
    part = pl.BlockSpec((ROW_BLOCK, GW), lambda i: (i, 0))
    rows = pl.BlockSpec((ROW_BLOCK, D), lambda i: (i, 0))
    vspec = pl.BlockSpec((None, 1, D), lambda i: (layer, 0, 0))
    in_specs = [part, part, part, part,
                pl.BlockSpec((D, D), lambda i: (0, 0)), rows,
                _mod_spec(layer, 2, D, 1), vspec, vspec, _mod_spec(layer, 3, D, 1), _mod_spec(layer, 4, D, 1)]
    args = list(parts) + [w_out_bf, xs, mod, ln_g.reshape(L, 1, D), ln_b.reshape(L, 1, D), mod, mod]
    out_specs = [rows, rows]
    out_shape = [jax.ShapeDtypeStruct((S, D), F32), jax.ShapeDtypeStruct((S, D), h_dtype)]
    if route:
        E = router_w.shape[-1]
        in_specs.append(pl.BlockSpec((D, E), lambda i: (0, 0)))
        args.append(router_w)
        out_specs += [pl.BlockSpec((ROW_BLOCK, TOP_K), lambda i: (i, 0))] * 2
        out_shape += [jax.ShapeDtypeStruct((S, TOP_K), jnp.int32), jax.ShapeDtypeStruct((S, TOP_K), F32)]
    return pl.pallas_call(
        functools.partial(_mixout_kernel, alpha=alpha, n_ctx_blocks=n_ctx_blocks, route=route, h_dtype=h_dtype),
        grid=(S // ROW_BLOCK,),
        in_specs=in_specs,
        out_specs=out_specs,
        out_shape=out_shape,
        compiler_params=_cparams(1),
    )(*args)


FFN_ST = 2048
FFN_SUB = 256
FFN_TF = 256


def _ffn_kernel(se_ref, sb_ref, sv_ref, h_ref, w1_ref, w3_ref, w2_ref, y_ref, acc_ref, w1b_ref, w3b_ref,
                w2b_ref, sem):
    st = pl.program_id(0)
    j = pl.program_id(1)
    n_sub = sv_ref[st]

    def out_copy(blk):
        dst = pl.multiple_of(st * FFN_ST, FFN_ST) + blk * FFN_SUB
        return pltpu.make_async_copy(acc_ref.at[pl.ds(blk * FFN_SUB, FFN_SUB)], y_ref.at[pl.ds(dst, FFN_SUB)], sem)

    @pl.when(n_sub > 0)
    def _():
        @pl.when(j == 0)
        def _():
            acc_ref[...] = jnp.zeros_like(acc_ref)

        w1b_ref[...] = w1_ref[...].astype(BF16)
        w3b_ref[...] = w3_ref[...].astype(BF16)
        w2b_ref[...] = w2_ref[...].astype(BF16)

        def block(blk, c):
            r0 = pl.multiple_of(blk * FFN_SUB, FFN_SUB)
            h = h_ref[pl.ds(r0, FFN_SUB), :]
            a = jnp.dot(h, w1b_ref[...], preferred_element_type=F32)
            b = jnp.dot(h, w3b_ref[...], preferred_element_type=F32)
            g = (_silu(a) * b).astype(BF16)
            acc_ref[pl.ds(r0, FFN_SUB), :] += jnp.dot(g, w2b_ref[...], preferred_element_type=F32)
            return c

        lax.fori_loop(0, n_sub, block, 0)

    @pl.when(j == pl.num_programs(1) - 1)
    def _():
        @pl.when(n_sub == 0)
        def _():
            acc_ref[...] = jnp.zeros_like(acc_ref)

        for blk in range(FFN_ST // FFN_SUB):
            out_copy(blk).start()
        for blk in range(FFN_ST // FFN_SUB):
            out_copy(blk).wait()


def _swiglu_supertiles(hs, w1, w3, w2, st_expert, st_block, st_valid):
    R, D = hs.shape
    _, _, F = w1.shape
    n_super = st_expert.shape[0]
    tf = FFN_TF
    nf = F // tf
    assert R % FFN_ST == 0 and F % tf == 0

    def chunk(st, j, sv):
        return jnp.where(sv[st] > 0, j, nf - 1)

    return pl.pallas_call(
        _ffn_kernel,
        grid_spec=pltpu.PrefetchScalarGridSpec(
            num_scalar_prefetch=3,
            grid=(n_super, nf),
            in_specs=[pl.BlockSpec((FFN_ST, D), lambda st, j, se, sb, sv: (sb[st], 0)),
                      pl.BlockSpec((None, D, tf), lambda st, j, se, sb, sv: (se[st], 0, chunk(st, j, sv))),
                      pl.BlockSpec((None, D, tf), lambda st, j, se, sb, sv: (se[st], 0, chunk(st, j, sv))),
                      pl.BlockSpec((None, tf, D), lambda st, j, se, sb, sv: (se[st], chunk(st, j, sv), 0))],
            out_specs=pl.BlockSpec(memory_space=pl.ANY),
            scratch_shapes=[pltpu.VMEM((FFN_ST, D), F32), pltpu.VMEM((D, tf), BF16), pltpu.VMEM((D, tf), BF16),
                            pltpu.VMEM((tf, D), BF16), pltpu.SemaphoreType.DMA(())]),
        out_shape=jax.ShapeDtypeStruct((R, D), F32),
        compiler_params=_cparams(2),
    )(st_expert, st_block, st_valid, hs, w1, w3, w2)


def _row_copy(src_ref, dst_ref, src_row, dst_row, sem):
    return pltpu.make_async_copy(src_ref.at[pl.ds(src_row, 1)], dst_ref.at[pl.ds(dst_row, 1)], sem)


DMA_UNROLL = 8


def _gather_kernel(tok_ref, sv_ref, src_ref, o_ref, buf_ref, sem):
    i = pl.program_id(0)
    tm = buf_ref.shape[0]
    per_st = FFN_ST // tm
    live = (i % per_st) < sv_ref[i // per_st]

    def issue(r, c):
        _row_copy(src_ref, buf_ref, tok_ref[i * tm + r], r, sem).start()
        return c

    def drain(r, c):
        _row_copy(src_ref, buf_ref, 0, r, sem).wait()
        return c

    @pl.when(live)
    def _():
        lax.fori_loop(0, tm, issue, 0, unroll=DMA_UNROLL)
        lax.fori_loop(0, tm, drain, 0, unroll=DMA_UNROLL)
        o_ref[...] = buf_ref[...].astype(o_ref.dtype)

    @pl.when(jnp.logical_not(live))
    def _():
        o_ref[...] = jnp.zeros_like(o_ref)


def _gather_rows(src, tok_of_slot, st_valid):
    R = tok_of_slot.shape[0]
    D = src.shape[1]
    tm = FFN_SUB
    return pl.pallas_call(
        _gather_kernel,
        grid_spec=pltpu.PrefetchScalarGridSpec(
            num_scalar_prefetch=2,
            grid=(R // tm,),
            in_specs=[pl.BlockSpec(memory_space=pl.ANY)],
            out_specs=pl.BlockSpec((tm, D), lambda i, tok, sv: (i, 0)),
            scratch_shapes=[pltpu.VMEM((tm, D), src.dtype), pltpu.SemaphoreType.DMA(())]),
        out_shape=jax.ShapeDtypeStruct((R, D), BF16),
        compiler_params=_cparams(1),
    )(tok_of_slot, st_valid, src)


def _ln2_dense_kernel(x_ref, f_ref, g2_ref, lg_ref, lb_ref, o_ref, *, alpha, n_ctx_blocks, row_off):
    is_ctx = (pl.program_id(0) + row_off) < n_ctx_blocks
    y = alpha * x_ref[...] + _pick(g2_ref[...], is_ctx) * f_ref[...]
    o_ref[...] = _layer_norm(y, lg_ref[...], lb_ref[...])


def _ln2_moe_kernel(sa_ref, sb_ref, x_ref, y_ref, gate_ref, g2_ref, lg_ref, lb_ref, o_ref, bufa_ref, bufb_ref,
                    sem, *, alpha, n_ctx_blocks, row_off):
    i = pl.program_id(0)
    R = ROW_BLOCK
    base = (i + row_off) * R

    def issue(r, c):
        _row_copy(y_ref, bufa_ref, sa_ref[base + r], r, sem).start()
        _row_copy(y_ref, bufb_ref, sb_ref[base + r], r, sem).start()
        return c

    def drain(r, c):
        _row_copy(y_ref, bufa_ref, 0, r, sem).wait()
        _row_copy(y_ref, bufb_ref, 0, r, sem).wait()
        return c

    lax.fori_loop(0, R, issue, 0, unroll=DMA_UNROLL // 2)
    lax.fori_loop(0, R, drain, 0, unroll=DMA_UNROLL // 2)
    gate = gate_ref[...]
    f = gate[:, 0:1] * bufa_ref[...] + gate[:, 1:2] * bufb_ref[...]
    is_ctx = (i + row_off) < n_ctx_blocks
    y = alpha * x_ref[...] + _pick(g2_ref[...], is_ctx) * f
    o_ref[...] = _layer_norm(y, lg_ref[...], lb_ref[...])


def _ln2(x1, f, mod, ln_g, ln_b, layer, alpha, n_ctx_blocks, row_off, moe=None):
    S, D = x1.shape
    L = ln_g.shape[0]
    nb = S // ROW_BLOCK - row_off
    n_pre = 0 if moe is None else 2
    wrap = (lambda f_: (lambda i, *_: f_(i)))
    rows_in = pl.BlockSpec((ROW_BLOCK, D), wrap(lambda i: (i + row_off, 0)))
    rows_out = pl.BlockSpec((ROW_BLOCK, D), wrap(lambda i: (i, 0)))
    vspec = pl.BlockSpec((None, 1, D), wrap(lambda i: (layer, 0, 0)))
    mspec = pl.BlockSpec((None, SUBLANES, D), wrap(lambda i: (layer, 0, 5)))
    common = dict(alpha=alpha, n_ctx_blocks=n_ctx_blocks, row_off=row_off)
    lg, lb = ln_g.reshape(L, 1, D), ln_b.reshape(L, 1, D)
    out_shape = jax.ShapeDtypeStruct((nb * ROW_BLOCK, D), F32)
    if moe is None:
        return pl.pallas_call(
            functools.partial(_ln2_dense_kernel, **common),
            grid=(nb,),
            in_specs=[rows_in, rows_in, mspec, vspec, vspec],
            out_specs=rows_out,
            out_shape=out_shape,
            compiler_params=_cparams(1),
        )(x1, f, mod, lg, lb)
    slot_a, slot_b, gates = moe
    return pl.pallas_call(
        functools.partial(_ln2_moe_kernel, **common),
        grid_spec=pltpu.PrefetchScalarGridSpec(
            num_scalar_prefetch=n_pre,
            grid=(nb,),
            in_specs=[rows_in, pl.BlockSpec(memory_space=pl.ANY),
                      pl.BlockSpec((ROW_BLOCK, TOP_K), wrap(lambda i: (i + row_off, 0))),
                      mspec, vspec, vspec],
            out_specs=rows_out,
            scratch_shapes=[pltpu.VMEM((ROW_BLOCK, D), F32), pltpu.VMEM((ROW_BLOCK, D), F32),
                            pltpu.SemaphoreType.DMA(())]),
        out_shape=out_shape,
        compiler_params=_cparams(1),
    )(slot_a, slot_b, x1, f, gates, mod, lg, lb)


def _route_slots(idx, row0, n_experts, tm):
    S = idx.shape[0]
    n = S - row0
    e_flat = idx[row0:].reshape(-1)
    onehot = (e_flat[:, None] == jnp.arange(n_experts, dtype=jnp.int32)[None, :]).astype(jnp.int32)
    pos = jnp.sum((jnp.cumsum(onehot, axis=0) - 1) * onehot, axis=1)
    counts = jnp.sum(onehot, axis=0)
    padded = ((counts + tm - 1) // tm) * tm
    ends = jnp.cumsum(padded)
    starts = ends - padded
    slot = starts[e_flat] + pos
    n_super = (TOP_K * n) // tm + n_experts
    tok = jnp.repeat(jnp.arange(n, dtype=jnp.int32) + row0, TOP_K)
    tok_of_slot = jnp.full((n_super * tm,), row0, jnp.int32).at[slot].set(tok)
    n_used = ends[-1] // tm
    k = jnp.minimum(jnp.arange(n_super, dtype=jnp.int32), n_used - 1)
    st_expert = jnp.minimum(jnp.searchsorted(ends, k * tm, side='right'), n_experts - 1).astype(jnp.int32)
    rows_left = starts[st_expert] + counts[st_expert] - k * tm
    st_valid = jnp.clip((rows_left + FFN_SUB - 1) // FFN_SUB, 0, tm // FFN_SUB)
    st_valid = jnp.where(jnp.arange(n_super) < n_used, st_valid, 0).astype(jnp.int32)
    slot2 = slot.reshape(n, TOP_K).astype(jnp.int32)
    pad = jnp.zeros((row0,), jnp.int32)
    slot_a = jnp.concatenate([pad, slot2[:, 0]])
    slot_b = jnp.concatenate([pad, slot2[:, 1]])
    return tok_of_slot, st_expert, k.astype(jnp.int32), st_valid, slot_a, slot_b


def kernel(x, c, ctx, c_ctx, w_mod, b_mod, w_in, w_out, ln1_g, ln1_b, ln2_g, ln2_b, pool_w, pool_scale,
           diff_lambda, diff_subln_g, conv_dw, conv_db, conv_ln_g, conv_ln_b, conv_pw, s5_a_re, s5_a_im,
           s5_log_dt, s5_b_re, s5_b_im, s5_c_re, s5_c_im, s5_d, s5_glu_w, s5_glu_b, ffn_w1, ffn_w3, ffn_w2,
           router_w, moe_w1, moe_w3, moe_w2):
    B, T, D = x.shape
    Tc = ctx.shape[1]
    depth = w_mod.shape[0]
    assert B == 1 and Tc % ROW_BLOCK == 0 and T % ROW_BLOCK == 0
    GW = D // N_GROUPS
    n_ctx_blocks = Tc // ROW_BLOCK
    alpha = (2.0 * depth) ** 0.25

    cc = jnp.zeros((SUBLANES, D), F32).at[0].set(c[0]).at[1].set(c_ctx)
    mod = _modulation(cc, w_mod, b_mod)
    cos, sin = _rope_tables(T, Tc, LANES)
    xs = jnp.concatenate([ctx[0], x[0]], axis=0)

    POOL_B, Q_B, K_B, V_B, CONV_B, S5_B = 0, 1, 2, 3, 2, 6

    for l in range(depth):
        last = l == depth - 1
        lam_init = 0.8 - 0.6 * math.exp(-0.3 * l)
        z = _in_projection(xs, mod, w_in, l, n_ctx_blocks)

        pa = _pool_mixer(z, pool_w, pool_scale, l, n_ctx_blocks)

        qt, kk, vt, kn2 = _qkv_prep(z, cos, sin, GW, Q_B, K_B, V_B)
        kn = jnp.sqrt(jnp.max(kn2[n_ctx_blocks:, :, 0], axis=0)) * (1.0 + 2.0 ** -6)
        lv = diff_lambda[l].astype(F32)
        lam = jnp.exp(jnp.sum(lv[0] * lv[1])) - jnp.exp(jnp.sum(lv[2] * lv[3])) + lam_init
        lam_row = jnp.full((1, ATTN_TQ), lam, F32)
        g_col = (diff_subln_g[l].astype(F32) * (1.0 - lam_init)).reshape(GW, 1)
        pb = _diff_attention(qt, kk, vt, lam_row, g_col, kn, Tc)

        pcv = _conv_mixer(z, CONV_B, conv_dw, conv_db, conv_ln_g, conv_ln_b, conv_pw, l, n_ctx_blocks)

        seg_len = xs.shape[0] // S5_SEGS
        pf = _s5_params(s5_a_re[l, 0], s5_a_im[l, 0], s5_log_dt[l, 0], s5_b_re[l, 0], s5_b_im[l, 0],
                        s5_c_re[l, 0], s5_c_im[l, 0], seg_len)
        pr = _s5_params(s5_a_re[l, 1], s5_a_im[l, 1], s5_log_dt[l, 1], s5_b_re[l, 1], s5_b_im[l, 1],
                        s5_c_re[l, 1], s5_c_im[l, 1], seg_len)
        yf, yr = _s5_scan(z, S5_B, pf, pr, Tc)
        pd = _s5_glu(yf, yr, z, S5_B, s5_d, s5_glu_w, s5_glu_b, l, n_ctx_blocks)

        w_out_bf = _cast_bf16(w_out, l)
        row_off = n_ctx_blocks if last else 0
        if l % 2 == 0:
            x1, h = _mix_out((pa, pb, pcv, pd), w_out_bf, xs, mod, ln1_g, ln1_b, l, alpha, n_ctx_blocks, None)
            S = xs.shape[0]
            n_super = -(-S // FFN_ST)
            per_st = FFN_ST // FFN_SUB
            st_valid = np.clip(-(-S // FFN_SUB) - per_st * np.arange(n_super), 0, per_st).astype(np.int32)
            h_pad = jnp.pad(h, ((0, n_super * FFN_ST - S), (0, 0)))
            f = _swiglu_supertiles(h_pad, ffn_w1[l // 2][None], ffn_w3[l // 2][None], ffn_w2[l // 2][None],
                                   jnp.zeros((n_super,), jnp.int32), jnp.arange(n_super, dtype=jnp.int32),
                                   jnp.asarray(st_valid))
            xs_new = _ln2(x1, f, mod, ln2_g, ln2_b, l, alpha, n_ctx_blocks, row_off)
        else:
            x1, h, idx, gates = _mix_out((pa, pb, pcv, pd), w_out_bf, xs, mod, ln1_g, ln1_b, l, alpha,
                                         n_ctx_blocks, router_w[l // 2])
            n_exp = router_w.shape[-1]
            row0 = row_off * ROW_BLOCK
            tok_of_slot, st_expert, st_block, st_valid, slot_a, slot_b = _route_slots(idx, row0, n_exp, FFN_ST)
            hs = _gather_rows(h, tok_of_slot, st_valid)
            y = _swiglu_supertiles(hs, moe_w1[l // 2], moe_w3[l // 2], moe_w2[l // 2], st_expert, st_block,
                                   st_valid)
            xs_new = _ln2(x1, y, mod, ln2_g, ln2_b, l, alpha, n_ctx_blocks, row_off, moe=(slot_a, slot_b, gates))
        xs = xs_new
    return xs[None]
```

```python
import functools
import math

import numpy as np
import jax
import jax.numpy as jnp
from jax import lax
from jax.experimental import pallas as pl
from jax.experimental.pallas import tpu as pltpu

F32 = jnp.float32
BF16 = jnp.bfloat16

GRID_W = 64
N_GROUPS = 4
POOL_WINDOWS = (2, 4, 8, 16)
DIFF_HEADS = 8
DIFF_QK = 32
CONV_WIDTH = 31
S5_P = 16
S5_N = 64
TOP_K = 2
ROPE_BASE = 10000.0
LN_EPS = 1e-5

LANES = 128
SUBLANES = 8
ROW_BLOCK = 256
VMEM_LIMIT = 56 * 1024 * 1024


def _cparams(n_axes, vmem=VMEM_LIMIT):
    return pltpu.CompilerParams(dimension_semantics=("arbitrary",) * n_axes, vmem_limit_bytes=vmem)


def _layer_norm(y, g, b):
    mu = jnp.mean(y, -1, keepdims=True)
    yc = y - mu
    var = jnp.mean(yc * yc, -1, keepdims=True)
    return yc * lax.rsqrt(var + LN_EPS) * g + b


def _silu(x):
    return x * jax.nn.sigmoid(x)


def _mod_kernel(cc_ref, w_ref, b_ref, o_ref):
    a = _silu(cc_ref[...])
    o_ref[...] = jnp.dot(a.astype(BF16), w_ref[...].astype(BF16), preferred_element_type=F32) + b_ref[...]


def _modulation(cc, w_mod, b_mod):
    L, D, N = w_mod.shape
    tn = 1536
    assert N % tn == 0
    return pl.pallas_call(
        _mod_kernel,
        grid=(L, N // tn),
        in_specs=[pl.BlockSpec((SUBLANES, D), lambda l, j: (0, 0)),
                  pl.BlockSpec((None, D, tn), lambda l, j: (l, 0, j)),
                  pl.BlockSpec((None, 1, tn), lambda l, j: (l, 0, j))],
        out_specs=pl.BlockSpec((None, SUBLANES, tn), lambda l, j: (l, 0, j)),
        out_shape=jax.ShapeDtypeStruct((L, SUBLANES, N), F32),
        compiler_params=_cparams(2),
    )(cc, w_mod, b_mod.reshape(L, 1, N))


def _mod_spec(layer, chunk, D, n_grid_axes):
    if n_grid_axes == 1:
        return pl.BlockSpec((None, SUBLANES, D), lambda i: (layer, 0, chunk))
    return pl.BlockSpec((None, SUBLANES, D), lambda j, i: (layer, 0, chunk))


def _pick(m, is_ctx):
    return jnp.where(is_ctx, m[1:2, :], m[0:1, :])


def _inproj_kernel(x_ref, sh_ref, sc_ref, w_ref, o_ref, wb_ref, *, n_ctx_blocks):
    i = pl.program_id(1)

    @pl.when(i == 0)
    def _():
        wb_ref[...] = w_ref[...].astype(BF16)

    is_ctx = i < n_ctx_blocks
    h = x_ref[...] * (1.0 + _pick(sc_ref[...], is_ctx)) + _pick(sh_ref[...], is_ctx)
    o_ref[...] = jnp.dot(h.astype(BF16), wb_ref[...], preferred_element_type=F32).astype(o_ref.dtype)


def _in_projection(xs, mod, w_in, layer, n_ctx_blocks):
    S, D = xs.shape
    N = w_in.shape[-1]
    tn = 1792
    assert N % tn == 0 and S % ROW_BLOCK == 0
    return pl.pallas_call(
        functools.partial(_inproj_kernel, n_ctx_blocks=n_ctx_blocks),
        grid=(N // tn, S // ROW_BLOCK),
        in_specs=[pl.BlockSpec((ROW_BLOCK, D), lambda j, i: (i, 0)),
                  _mod_spec(layer, 0, D, 2),
                  _mod_spec(layer, 1, D, 2),
                  pl.BlockSpec((None, D, tn), lambda j, i: (layer, 0, j))],
        out_specs=pl.BlockSpec((ROW_BLOCK, tn), lambda j, i: (i, j)),
        out_shape=jax.ShapeDtypeStruct((S, N), BF16),
        scratch_shapes=[pltpu.VMEM((D, tn), BF16)],
        compiler_params=_cparams(2),
    )(xs, mod, mod, w_in)


def _seq_edges(i, n_blocks, n_ctx_blocks):
    prev_ok = jnp.logical_and(i != 0, i != n_ctx_blocks)
    next_ok = jnp.logical_and(i != n_ctx_blocks - 1, i != n_blocks - 1)
    return prev_ok, next_ok


def _pool_kernel(p_ref, c_ref, n_ref, w_ref, scale_ref, o_ref, ext_ref, *, n_blocks, n_ctx_blocks):
    i = pl.program_id(0)
    R = ROW_BLOCK
    halo = SUBLANES
    prev_ok, next_ok = _seq_edges(i, n_blocks, n_ctx_blocks)
    cur = c_ref[...].astype(F32)
    pack = 2 * SUBLANES
    ext_ref[0:halo, :] = jnp.where(prev_ok, p_ref[R - pack:R, :].astype(F32)[pack - halo:], 0.0)
    ext_ref[halo:halo + R, :] = cur
    ext_ref[halo + R:halo + R + halo, :] = jnp.where(next_ok, n_ref[0:pack, :].astype(F32)[:halo], 0.0)
    rloc = lax.broadcasted_iota(jnp.int32, (R, 1), 0)
    gw = cur.shape[1] // len(POOL_WINDOWS)
    ext = ext_ref[...]
    n_ext = ext.shape[0]
    accs = [jnp.zeros((R, gw), F32) for _ in POOL_WINDOWS]
    for r in range(SUBLANES):
        rolled = ext if r == 0 else pltpu.roll(ext, n_ext - r, 0)
        for g, w in enumerate(POOL_WINDOWS):
            for d in range(-(w // 2), w - w // 2):
                m, rr = divmod(halo + d, SUBLANES)
                if rr == r:
                    accs[g] = accs[g] + rolled[m * SUBLANES:m * SUBLANES + R, g * gw:(g + 1) * gw]
    outs = []
    for g, w in enumerate(POOL_WINDOWS):
        cnt = jnp.zeros((R, 1), F32)
        for d in range(-(w // 2), w - w // 2):
            valid = jnp.logical_and(jnp.logical_or(rloc + d >= 0, prev_ok),
                                    jnp.logical_or(rloc + d < R, next_ok))
            cnt = cnt + valid.astype(F32)
        diff = accs[g] / cnt - cur[:, g * gw:(g + 1) * gw]
        outs.append(jnp.dot(diff.astype(BF16), w_ref[g].astype(BF16), preferred_element_type=F32))
    o_ref[...] = (jnp.concatenate(outs, axis=-1) * scale_ref[...]).astype(o_ref.dtype)


def _pool_mixer(z, pool_w, pool_scale, layer, n_ctx_blocks):
    S = z.shape[0]
    nb = S // ROW_BLOCK
    GW = pool_scale.shape[-1]
    G, gw = pool_w.shape[1], pool_w.shape[2]
    return pl.pallas_call(
        functools.partial(_pool_kernel, n_blocks=nb, n_ctx_blocks=n_ctx_blocks),
        grid=(nb,),
        in_specs=[pl.BlockSpec((ROW_BLOCK, GW), lambda i: (jnp.maximum(i - 1, 0), 0)),
                  pl.BlockSpec((ROW_BLOCK, GW), lambda i: (i, 0)),
                  pl.BlockSpec((ROW_BLOCK, GW), lambda i: (jnp.minimum(i + 1, nb - 1), 0)),
                  pl.BlockSpec((None, G, gw, gw), lambda i: (layer, 0, 0, 0)),
                  pl.BlockSpec((None, 1, GW), lambda i: (layer, 0, 0))],
        out_specs=pl.BlockSpec((ROW_BLOCK, GW), lambda i: (i, 0)),
        out_shape=jax.ShapeDtypeStruct((S, GW), BF16),
        scratch_shapes=[pltpu.VMEM((ROW_BLOCK + 2 * SUBLANES, GW), F32)],
        compiler_params=_cparams(1),
    )(z, z, z, pool_w, pool_scale.reshape(pool_scale.shape[0], 1, GW))


CONV_HALO = 16


def _conv_kernel(p_ref, c_ref, n_ref, dw_ref, db_ref, g_ref, b_ref, pw_ref, o_ref, ext_ref,
                 *, n_blocks, n_ctx_blocks):
    i = pl.program_id(0)
    R = ROW_BLOCK
    H = CONV_HALO
    GW = o_ref.shape[1]
    prev_ok, next_ok = _seq_edges(i, n_blocks, n_ctx_blocks)

    def glu(u):
        u = u.astype(F32)
        return u[:, :GW] * jax.nn.sigmoid(u[:, GW:])

    ext_ref[0:H, :] = jnp.where(prev_ok, glu(p_ref[R - H:R, :]), 0.0)
    ext_ref[H:H + R, :] = glu(c_ref[...])
    ext_ref[H + R:H + R + H, :] = jnp.where(next_ok, glu(n_ref[0:H, :]), 0.0)
    off = H - CONV_WIDTH // 2
    ext = ext_ref[...]
    n_ext = ext.shape[0]
    acc = jnp.zeros((R, GW), F32)
    for r in range(SUBLANES):
        rolled = ext if r == 0 else pltpu.roll(ext, n_ext - r, 0)
        for j in range(CONV_WIDTH):
            m, rr = divmod(off + j, SUBLANES)
            if rr == r:
                acc = acc + rolled[m * SUBLANES:m * SUBLANES + R, :] * dw_ref[j:j + 1, :]
    y = _layer_norm(acc + db_ref[...], g_ref[...], b_ref[...])
    o_ref[...] = jnp.dot(_silu(y).astype(BF16), pw_ref[...].astype(BF16),
                         preferred_element_type=F32).astype(o_ref.dtype)


def _conv_mixer(z, col_block, conv_dw, conv_db, conv_ln_g, conv_ln_b, conv_pw, layer, n_ctx_blocks):
    S = z.shape[0]
    nb = S // ROW_BLOCK
    GW = conv_db.shape[-1]
    L = conv_db.shape[0]
    vec = lambda a: a.reshape(L, 1, GW)
    vspec = pl.BlockSpec((None, 1, GW), lambda i: (layer, 0, 0))
    return pl.pallas_call(
        functools.partial(_conv_kernel, n_blocks=nb, n_ctx_blocks=n_ctx_blocks),
        grid=(nb,),
        in_specs=[pl.BlockSpec((ROW_BLOCK, 2 * GW), lambda i: (jnp.maximum(i - 1, 0), col_block)),
                  pl.BlockSpec((ROW_BLOCK, 2 * GW), lambda i: (i, col_block)),
                  pl.BlockSpec((ROW_BLOCK, 2 * GW), lambda i: (jnp.minimum(i + 1, nb - 1), col_block)),
                  pl.BlockSpec((None, CONV_WIDTH, GW), lambda i: (layer, 0, 0)),
                  vspec, vspec, vspec,
                  pl.BlockSpec((None, GW, GW), lambda i: (layer, 0, 0))],
        out_specs=pl.BlockSpec((ROW_BLOCK, GW), lambda i: (i, 0)),
        out_shape=jax.ShapeDtypeStruct((S, GW), BF16),
        scratch_shapes=[pltpu.VMEM((ROW_BLOCK + 2 * CONV_HALO, GW), F32)],
        compiler_params=_cparams(1),
    )(z, z, z, conv_dw, vec(conv_db), vec(conv_ln_g), vec(conv_ln_b), conv_pw)


def _rope_tables(T, Tc, width):
    ax = DIFF_QK // 2
    inv = ROPE_BASE ** (-jnp.arange(0, ax, 2, dtype=F32) / ax)
    t = jnp.arange(T)
    row = (t // GRID_W).astype(F32)
    col = (t % GRID_W).astype(F32)
    ang = jnp.stack([row[:, None] * inv, col[:, None] * inv], axis=1)
    cos = jnp.cos(ang)[:, :, None, :]
    sin = jnp.sin(ang)[:, :, None, :]
    cos = jnp.broadcast_to(cos, (T, 2, 2, ax // 2)).reshape(T, DIFF_QK)
    sin = jnp.concatenate([-sin, sin], axis=2).reshape(T, DIFF_QK)
    reps = width // DIFF_QK
    cos = jnp.concatenate([jnp.ones((Tc, DIFF_QK), F32), cos], axis=0)
    sin = jnp.concatenate([jnp.zeros((Tc, DIFF_QK), F32), sin], axis=0)
    return jnp.tile(cos, (1, reps)), jnp.tile(sin, (1, reps))


def _qkv_prep_kernel(q_ref, k_ref, v_ref, cos_ref, sin_ref, qo_ref, ko_ref, vo_ref, kn_ref):
    W = q_ref.shape[1]
    half = DIFF_QK // 4
    lane = lax.broadcasted_iota(jnp.int32, (1, W), 1)
    first = (lane % (2 * half)) < half
    cos = jnp.concatenate([cos_ref[...]] * (W // LANES), axis=1)
    sin = jnp.concatenate([sin_ref[...]] * (W // LANES), axis=1)

    def rope(x):
        partner = jnp.where(first, pltpu.roll(x, W - half, 1), pltpu.roll(x, half, 1))
        return x * cos + partner * sin

    qo_ref[...] = (rope(q_ref[...].astype(F32)) * (DIFF_QK ** -0.5 * math.log2(math.e))).T.astype(BF16)
    kb = rope(k_ref[...].astype(F32)).astype(BF16)
    ko_ref[...] = kb
    ksq = kb.astype(F32).T
    ksq = ksq * ksq
    for grp in range(W // DIFF_QK):
        n2 = jnp.sum(ksq[grp * DIFF_QK:(grp + 1) * DIFF_QK, :], axis=0, keepdims=True)
        kn_ref[grp:grp + 1, :] = jnp.broadcast_to(jnp.max(n2, axis=1, keepdims=True), (1, LANES))
    vt = v_ref[...].astype(F32).T.astype(BF16)
    dv = LANES // 2
    ones = jnp.ones((ATTN_VROWS - dv, vt.shape[1]), BF16)
    for h in range(W // dv):
        vo_ref[h * ATTN_VROWS:h * ATTN_VROWS + dv, :] = vt[h * dv:(h + 1) * dv, :]
        vo_ref[h * ATTN_VROWS + dv:(h + 1) * ATTN_VROWS, :] = ones


def _qkv_prep(z, cos, sin, W, q_blk, k_blk, v_blk):
    S = z.shape[0]
    nb = S // ROW_BLOCK
    row = lambda c: pl.BlockSpec((ROW_BLOCK, W), lambda i: (i, c))
    tab = pl.BlockSpec((ROW_BLOCK, LANES), lambda i: (i, 0))
    return pl.pallas_call(
        _qkv_prep_kernel,
        grid=(nb,),
        in_specs=[row(q_blk), row(k_blk), row(v_blk), tab, tab],
        out_specs=[pl.BlockSpec((W, ROW_BLOCK), lambda i: (0, i)), row(0),
                   pl.BlockSpec((DIFF_HEADS * ATTN_VROWS, ROW_BLOCK), lambda i: (0, i)),
                   pl.BlockSpec((None, W // DIFF_QK, LANES), lambda i: (i, 0, 0))],
        out_shape=[jax.ShapeDtypeStruct((W, S), BF16), jax.ShapeDtypeStruct((S, W), BF16),
                   jax.ShapeDtypeStruct((DIFF_HEADS * ATTN_VROWS, S), BF16),
                   jax.ShapeDtypeStruct((nb, W // DIFF_QK, LANES), F32)],
        compiler_params=_cparams(1),
    )(z, z, z, cos, sin)


ATTN_TQ = 256
ATTN_VROWS = LANES // 2 + 2 * SUBLANES
ATTN_MARGIN = 64.0
ATTN_TK = (4096, 2048, 1024)


def _attn_kernel(qt_ref, k_ref, vt_ref, lam_ref, g_ref, kn_ref, o_ref, qq_ref, m_ref, acc_ref,
                 *, n_ctx, n_ctx_blocks, n_lat_chunks, tk):
    i = pl.program_id(1)
    tq = ATTN_TQ
    dv = LANES // 2
    qt = qt_ref[...]
    feat = lax.broadcasted_iota(jnp.int32, (LANES, 1), 0)
    zero = jnp.zeros_like(qt)
    for hh in range(2):
        for comp in range(2):
            lo = hh * dv + comp * DIFF_QK
            keep = jnp.logical_and(feat >= lo, feat < lo + DIFF_QK)
            qq_ref[hh, :, comp * tq:(comp + 1) * tq] = jnp.where(keep, qt, zero)

    def attend(start, size, mode):
        kk = k_ref[pl.ds(start, size), :]
        for hh in range(2):
            s = jnp.dot(kk, qq_ref[hh], preferred_element_type=F32)
            vv = vt_ref[hh * ATTN_VROWS:(hh + 1) * ATTN_VROWS, pl.ds(start, size)]
            mx = jnp.max(s, axis=0, keepdims=True)
            if mode == "first":
                m_ref[hh] = mx
                p = jnp.exp2(s - mx)
                acc_ref[hh] = jnp.dot(vv, p.astype(BF16), preferred_element_type=F32)
            elif mode == "exact":
                m_old = m_ref[hh]
                m_new = jnp.maximum(m_old, mx)
                m_ref[hh] = m_new
                p = jnp.exp2(s - m_new)
                acc_ref[hh] = (jnp.exp2(m_old - m_new) * acc_ref[hh]
                               + jnp.dot(vv, p.astype(BF16), preferred_element_type=F32))
            else:
                m_old = m_ref[hh]
                p = jnp.exp2(s - m_old)
                m_new = jnp.maximum(m_old, mx)
                m_ref[hh] = m_new
                acc_ref[hh] = (jnp.exp2(m_old - m_new)
                               * (acc_ref[hh] + jnp.dot(vv, p.astype(BF16), preferred_element_type=F32)))

    attend(0, n_ctx, "first")
    n_steps = jnp.where(i < n_ctx_blocks, 0, n_lat_chunks)

    pair = pl.program_id(0)
    col = lax.broadcasted_iota(jnp.int32, (1, 2 * tq), 1)
    excess = jnp.full((1, 2 * tq), -jnp.inf, F32)
    for hh in range(2):
        qf = qq_ref[hh].astype(F32)
        qn = jnp.sqrt(jnp.sum(qf * qf, axis=0, keepdims=True))
        grp = (2 * pair + hh) * 2
        kn = jnp.where(col < tq, kn_ref[grp], kn_ref[grp + 1])
        excess = jnp.maximum(excess, qn * kn - m_ref[hh])
    safe = jnp.max(excess) < ATTN_MARGIN

    def loop(mode):
        def body(c, carry):
            attend(pl.multiple_of(n_ctx + c * tk, LANES), tk, mode)
            return carry
        lax.fori_loop(0, n_steps, body, 0)

    @pl.when(safe)
    def _():
        loop("deferred")

    @pl.when(jnp.logical_not(safe))
    def _():
        loop("exact")

    lam = lam_ref[...]
    outs = []
    for hh in range(2):
        acc = acc_ref[hh]
        ratio = acc[:dv] / acc[dv:dv + 1]
        o = ratio[:, :tq] - lam * ratio[:, tq:]
        r = lax.rsqrt(jnp.sum(o * o, axis=0, keepdims=True) / dv + LN_EPS)
        outs.append(o * r)
    o_ref[...] = (jnp.concatenate(outs, axis=0) * g_ref[...]).T.astype(o_ref.dtype)


def _diff_attention(qt, k, vt, lam, g, kn, n_ctx):
    W, S = qt.shape
    assert S % ATTN_TQ == 0 and n_ctx % ATTN_TQ == 0
    tk = next(t for t in ATTN_TK if (S - n_ctx) % t == 0)
    nq = S // ATTN_TQ
    return pl.pallas_call(
        functools.partial(_attn_kernel, n_ctx=n_ctx, n_ctx_blocks=n_ctx // ATTN_TQ,
                          n_lat_chunks=(S - n_ctx) // tk, tk=tk),
        grid=(W // LANES, nq),
        in_specs=[pl.BlockSpec((LANES, ATTN_TQ), lambda p, i: (p, i)),
                  pl.BlockSpec((S, LANES), lambda p, i: (0, p)),
                  pl.BlockSpec((2 * ATTN_VROWS, S), lambda p, i: (p, 0)),
                  pl.BlockSpec((1, ATTN_TQ), lambda p, i: (0, 0)),
                  pl.BlockSpec((LANES, 1), lambda p, i: (p, 0)),
                  pl.BlockSpec(memory_space=pltpu.SMEM)],
        out_specs=pl.BlockSpec((ATTN_TQ, LANES), lambda p, i: (i, p)),
        out_shape=jax.ShapeDtypeStruct((S, W), BF16),
        scratch_shapes=[pltpu.VMEM((2, LANES, 2 * ATTN_TQ), BF16),
                        pltpu.VMEM((2, 1, 2 * ATTN_TQ), F32),
                        pltpu.VMEM((2, ATTN_VROWS, 2 * ATTN_TQ), F32)],
        compiler_params=_cparams(2),
    )(qt, k, vt, lam, g, kn)


S5_SEGS = SUBLANES
S5_KB = 32
S5_GB = 8


def _s5_params(a_re, a_im, log_dt, b_re, b_im, c_re, c_im, seg_len):
    G, N = a_re.shape
    P = b_re.shape[-1]
    nblk = G // S5_GB
    a_re, a_im = a_re.astype(F32), a_im.astype(F32)
    dt = jnp.exp(log_dt.astype(F32))[:, None]
    lr, li = dt * a_re, dt * a_im
    mag = jnp.exp(lr)
    ar, ai = mag * jnp.cos(li), mag * jnp.sin(li)
    den = a_re * a_re + a_im * a_im
    qr = ((ar - 1.0) * a_re + ai * a_im) / den
    qi = (ai * a_re - (ar - 1.0) * a_im) / den
    b_re, b_im = b_re.astype(F32), b_im.astype(F32)
    br = qr[..., None] * b_re - qi[..., None] * b_im
    bi = qr[..., None] * b_im + qi[..., None] * b_re
    mag_l = jnp.exp(seg_len * lr)
    alr, ali = mag_l * jnp.cos(seg_len * li), mag_l * jnp.sin(seg_len * li)
    eye = jnp.eye(S5_GB, dtype=F32)
    wb = lambda m: jnp.einsum('gh,bgnp->bgphn', eye, m.reshape(nblk, S5_GB, N, P)).reshape(
        nblk, S5_GB * P, S5_GB * N)
    w_in = jnp.concatenate([wb(br), wb(bi)], axis=2)
    cm = lambda m: jnp.einsum('gh,bgpn->bhngp', eye, m.astype(F32).reshape(nblk, S5_GB, P, N)).reshape(
        nblk, S5_GB * N, S5_GB * P)
    w_out = jnp.concatenate([cm(c_re), -cm(c_im)], axis=1)
    row = lambda r, i: jnp.concatenate([r.reshape(1, G * N), i.reshape(1, G * N)], axis=1)
    coef = jnp.broadcast_to(row(ar, ai), (S5_SEGS, 2 * G * N))
    return w_in.astype(BF16), coef, row(alr, ali), w_out.astype(BF16)


def _s5_kernel(*refs, emit_out):
    n_seg = S5_SEGS
    uf_ref = refs[0]
    ur_refs = refs[1:1 + n_seg]
    rest = refs[1 + n_seg:]
    if emit_out:
        (wbf_ref, wbr_ref, af_ref, ar_ref, ef_ref, er_ref, alf_ref, alr_ref, cf_ref, cr_ref,
         yf_ref, yr_ref, stage_ref, bf_ref, br_ref, hf_ref, hr_ref) = rest
    else:
        (wbf_ref, wbr_ref, af_ref, ar_ref, ef_out_ref, er_out_ref,
         stage_ref, bf_ref, br_ref, hf_ref, hr_ref) = rest
    g = pl.program_id(0)
    KB = S5_KB
    R = n_seg * KB
    NS = af_ref.shape[1] // 2
    nblk = wbf_ref.shape[0]
    wi = wbf_ref.shape[1]
    ws = wbf_ref.shape[2] // 2

    def cmul_add(a_row, h, add):
        are, aim = a_row[:, :NS], a_row[:, NS:]
        hre, him = h[:, :NS], h[:, NS:]
        return jnp.concatenate([are * hre - aim * him + add[:, :NS], are * him + aim * hre + add[:, NS:]], axis=1)

    @pl.when(g == 0)
    def _():
        if emit_out:
            def chain(e_ref, al_ref, order):
                al = al_ref[...]
                c = jnp.zeros((1, 2 * NS), F32)
                rows = [None] * n_seg
                for s in order:
                    rows[s] = c
                    c = cmul_add(al, c, e_ref[s:s + 1, :])
                return jnp.concatenate(rows, axis=0)
            hf_ref[...] = chain(ef_ref, alf_ref, range(n_seg))
            hr_ref[...] = chain(er_ref, alr_ref, range(n_seg - 1, -1, -1))
        else:
            hf_ref[...] = jnp.zeros_like(hf_ref)
            hr_ref[...] = jnp.zeros_like(hr_ref)

    def interleaved(load_seg):
        for s in range(n_seg):
            blk = load_seg(s).astype(F32)
            for c in range(nblk):
                stage_ref[c, s * KB:(s + 1) * KB, :] = blk[:, c * wi:(c + 1) * wi]
        rows = [jnp.concatenate([stage_ref[c, pl.ds(kk, n_seg, stride=KB), :] for c in range(nblk)], axis=1)
                for kk in range(KB)]
        return jnp.concatenate(rows, axis=0).astype(BF16)

    def project_in(u, w_ref, buf_ref):
        for b in range(nblk):
            res = jnp.dot(u[:, b * wi:(b + 1) * wi], w_ref[b], preferred_element_type=F32)
            buf_ref[:, b * ws:(b + 1) * ws] = res[:, :ws]
            buf_ref[:, NS + b * ws:NS + (b + 1) * ws] = res[:, ws:]

    def scan(buf_ref, a_ref, h_ref, reverse):
        half = NS // 2
        for c in range(2):
            cre = slice(c * half, (c + 1) * half)
            cim = slice(NS + c * half, NS + (c + 1) * half)
            are, aim = a_ref[:, cre], a_ref[:, cim]

            def step(t, carry, cre=cre, cim=cim, are=are, aim=aim):
                hre, him = carry
                kk = (KB - 1 - t) if reverse else t
                r0 = pl.multiple_of(kk * n_seg, n_seg)
                nre = are * hre - aim * him + buf_ref[pl.ds(r0, n_seg), cre]
                nim = are * him + aim * hre + buf_ref[pl.ds(r0, n_seg), cim]
                if emit_out:
                    buf_ref[pl.ds(r0, n_seg), cre] = nre
                    buf_ref[pl.ds(r0, n_seg), cim] = nim
                return nre, nim

            hre, him = lax.fori_loop(0, KB, step, (h_ref[:, cre], h_ref[:, cim]))
            h_ref[:, cre] = hre
            h_ref[:, cim] = him

    def project_out(buf_ref, c_ref, y_ref):
        for b in range(nblk):
            hcat = jnp.concatenate([buf_ref[:, b * ws:(b + 1) * ws], buf_ref[:, NS + b * ws:NS + (b + 1) * ws]],
                                   axis=1).astype(BF16)
            stage_ref[b] = jnp.dot(hcat, c_ref[b], preferred_element_type=F32)
        for s in range(n_seg):
            y_ref[s] = jnp.concatenate([stage_ref[c, pl.ds(s, KB, stride=n_seg), :] for c in range(nblk)], axis=1)

    project_in(interleaved(lambda s: uf_ref[s]), wbf_ref, bf_ref)
    scan(bf_ref, af_ref, hf_ref, False)
    if emit_out:
        project_out(bf_ref, cf_ref, yf_ref)
    project_in(interleaved(lambda s: ur_refs[s][...]), wbr_ref, br_ref)
    scan(br_ref, ar_ref, hr_ref, True)
    if emit_out:
        project_out(br_ref, cr_ref, yr_ref)
    else:
        ef_out_ref[...] = hf_ref[...]
        er_out_ref[...] = hr_ref[...]


def _s5_pass(z, col_block, pf, pr, ends, n_ctx):
    S, NZ = z.shape
    GW = pf[0].shape[1] * pf[0].shape[0]
    NS2 = pf[1].shape[1]
    seg_len = S // S5_SEGS
    steps = seg_len // S5_KB
    nblocks = S // S5_KB
    assert S % (S5_SEGS * S5_KB) == 0 and n_ctx % S5_KB == 0
    ctx_blocks = n_ctx // S5_KB
    emit_out = ends is not None
    z4 = z.reshape(S5_SEGS, steps, S5_KB, NZ)
    z3 = z.reshape(nblocks, S5_KB, NZ)

    def rev_spec(s):
        return pl.BlockSpec((None, S5_KB, GW),
                            lambda g: ((s * steps + steps - 1 - g + ctx_blocks) % nblocks, 0, col_block))

    const = lambda a: pl.BlockSpec(a.shape, lambda g: (0,) * a.ndim)
    in_specs = [pl.BlockSpec((S5_SEGS, None, S5_KB, GW), lambda g: (0, g, 0, col_block))]
    in_specs += [rev_spec(s) for s in range(S5_SEGS)]
    args = [z4] + [z3] * S5_SEGS
    weights = [pf[0], pr[0], pf[1], pr[1]]
    if emit_out:
        weights += [ends[0], ends[1], pf[2], pr[2], pf[3], pr[3]]
    in_specs += [const(a) for a in weights]
    args += weights
    scratch = [pltpu.VMEM((pf[0].shape[0], S5_SEGS * S5_KB, pf[0].shape[1]), F32),
               pltpu.VMEM((S5_SEGS * S5_KB, NS2), F32), pltpu.VMEM((S5_SEGS * S5_KB, NS2), F32),
               pltpu.VMEM((S5_SEGS, NS2), F32), pltpu.VMEM((S5_SEGS, NS2), F32)]
    if emit_out:
        yshape = jax.ShapeDtypeStruct((S5_SEGS, steps, S5_KB, GW), F32)
        out_shape = [yshape, yshape]
        out_specs = [pl.BlockSpec((S5_SEGS, None, S5_KB, GW), lambda g: (0, g, 0, 0)),
                     pl.BlockSpec((S5_SEGS, None, S5_KB, GW), lambda g: (0, steps - 1 - g, 0, 0))]
    else:
        eshape = jax.ShapeDtypeStruct((S5_SEGS, NS2), F32)
        out_shape = [eshape, eshape]
        out_specs = [pl.BlockSpec((S5_SEGS, NS2), lambda g: (0, 0))] * 2
    return pl.pallas_call(
        functools.partial(_s5_kernel, emit_out=emit_out),
        grid=(steps,),
        in_specs=in_specs,
        out_specs=out_specs,
        out_shape=out_shape,
        scratch_shapes=scratch,
        compiler_params=_cparams(1),
    )(*args)


def _s5_scan(z, col_block, pf, pr, n_ctx):
    S = z.shape[0]
    GW = pf[0].shape[1] * pf[0].shape[0]
    ends = _s5_pass(z, col_block, pf, pr, None, n_ctx)
    yf, yr = _s5_pass(z, col_block, pf, pr, ends, n_ctx)
    return yf.reshape(S, GW), yr.reshape(S, GW)


def _s5_glu_kernel(yf_ref, yr_ref, u_ref, d_ref, w_ref, b_ref, o_ref):
    y = yf_ref[...] + yr_ref[...] + d_ref[...] * u_ref[...].astype(F32)
    zz = jax.nn.gelu(y)
    gate = jnp.dot(zz.astype(BF16), w_ref[...].astype(BF16), preferred_element_type=F32) + b_ref[...]
    o_ref[...] = (zz * jax.nn.sigmoid(gate)).astype(o_ref.dtype)


def _s5_glu(yf, yr, z, col_block, s5_d, glu_w, glu_b, layer, n_ctx_blocks):
    S, GW = yf.shape
    L = s5_d.shape[0]
    nb = S // ROW_BLOCK
    row = lambda c: pl.BlockSpec((ROW_BLOCK, GW), lambda i: (i, c))
    rot = pl.BlockSpec((ROW_BLOCK, GW), lambda i: ((i + nb - n_ctx_blocks) % nb, 0))
    vspec = pl.BlockSpec((None, 1, GW), lambda i: (layer, 0, 0))
    return pl.pallas_call(
        _s5_glu_kernel,
        grid=(nb,),
        in_specs=[row(0), rot, row(col_block), vspec,
                  pl.BlockSpec((None, GW, GW), lambda i: (layer, 0, 0)), vspec],
        out_specs=row(0),
        out_shape=jax.ShapeDtypeStruct((S, GW), BF16),
        compiler_params=_cparams(1),
    )(yf, yr, z, s5_d.reshape(L, 1, GW), glu_w, glu_b.reshape(L, 1, GW))


def _cast_kernel(x_ref, o_ref):
    o_ref[...] = x_ref[...].astype(o_ref.dtype)


def _cast_bf16(w, layer):
    _, K, N = w.shape
    tk = 512
    return pl.pallas_call(
        _cast_kernel,
        grid=(K // tk,),
        in_specs=[pl.BlockSpec((None, tk, N), lambda i: (layer, i, 0))],
        out_specs=pl.BlockSpec((tk, N), lambda i: (i, 0)),
        out_shape=jax.ShapeDtypeStruct((K, N), BF16),
        compiler_params=_cparams(1),
    )(w)


def _mixout_kernel(*refs, alpha, n_ctx_blocks, route, h_dtype):
    (pa_ref, pb_ref, pc_ref, pd_ref, w_ref, x_ref, g1_ref, lg_ref, lb_ref, sh_ref, sc_ref) = refs[:11]
    if route:
        rw_ref, x1_ref, h_ref, idx_ref, gate_ref = refs[11:]
    else:
        x1_ref, h_ref = refs[11:]
    i = pl.program_id(0)
    is_ctx = i < n_ctx_blocks
    GW = pa_ref.shape[1]
    mix = jnp.zeros(x_ref.shape, F32)
    for k, p_ref in enumerate((pa_ref, pb_ref, pc_ref, pd_ref)):
        mix = mix + jnp.dot(p_ref[...].astype(BF16), w_ref[k * GW:(k + 1) * GW, :],
                            preferred_element_type=F32)
    y = alpha * x_ref[...] + _pick(g1_ref[...], is_ctx) * mix
    x1 = _layer_norm(y, lg_ref[...], lb_ref[...])
    x1_ref[...] = x1
    h = x1 * (1.0 + _pick(sc_ref[...], is_ctx)) + _pick(sh_ref[...], is_ctx)
    h_ref[...] = h.astype(h_dtype)
    if route:
        rw = rw_ref[...]
        h_hi = h.astype(BF16)
        h_lo = (h - h_hi.astype(F32)).astype(BF16)
        w_hi = rw.astype(BF16)
        w_lo = (rw - w_hi.astype(F32)).astype(BF16)
        logits = (jnp.dot(h_hi, w_hi, preferred_element_type=F32)
                  + (jnp.dot(h_lo, w_hi, preferred_element_type=F32)
                     + jnp.dot(h_hi, w_lo, preferred_element_type=F32)))
        n_exp = rw_ref.shape[1]
        lane = lax.broadcasted_iota(jnp.int32, logits.shape, 1)
        m1 = jnp.max(logits, axis=-1, keepdims=True)
        i1 = jnp.min(jnp.where(logits == m1, lane, n_exp), axis=-1, keepdims=True)
        rest = jnp.where(lane == i1, -jnp.inf, logits)
        m2 = jnp.max(rest, axis=-1, keepdims=True)
        i2 = jnp.min(jnp.where(rest == m2, lane, n_exp), axis=-1, keepdims=True)
        e2 = jnp.exp(m2 - m1)
        idx_ref[...] = jnp.concatenate([i1, i2], axis=1)
        gate_ref[...] = jnp.concatenate([1.0 / (1.0 + e2), e2 / (1.0 + e2)], axis=1)


def _mix_out(parts, w_out_bf, xs, mod, ln_g, ln_b, layer, alpha, n_ctx_blocks, router_w):
    S, D = xs.shape
    GW = parts[0].shape[1]
    L = ln_g.shape[0]
    route = router_w is not None
    h_dtype = F32 if route else BF16
    part = pl.BlockSpec((ROW_BLOCK, GW), lambda i: (i, 0))
    rows = pl.BlockSpec((ROW_BLOCK, D), lambda i: (i, 0))
    vspec = pl.BlockSpec((None, 1, D), lambda i: (layer, 0, 0))
    in_specs = [part, part, part, part,
                pl.BlockSpec((D, D), lambda i: (0, 0)), rows,
                _mod_spec(layer, 2, D, 1), vspec, vspec, _mod_spec(layer, 3, D, 1), _mod_spec(layer, 4, D, 1)]
    args = list(parts) + [w_out_bf, xs, mod, ln_g.reshape(L, 1, D), ln_b.reshape(L, 1, D), mod, mod]
    out_specs = [rows, rows]
    out_shape = [jax.ShapeDtypeStruct((S, D), F32), jax.ShapeDtypeStruct((S, D), h_dtype)]
    if route:
        E = router_w.shape[-1]
        in_specs.append(pl.BlockSpec((D, E), lambda i: (0, 0)))
        args.append(router_w)
        out_specs += [pl.BlockSpec((ROW_BLOCK, TOP_K), lambda i: (i, 0))] * 2
        out_shape += [jax.ShapeDtypeStruct((S, TOP_K), jnp.int32), jax.ShapeDtypeStruct((S, TOP_K), F32)]
    return pl.pallas_call(
        functools.partial(_mixout_kernel, alpha=alpha, n_ctx_blocks=n_ctx_blocks, route=route, h_dtype=h_dtype),
        grid=(S // ROW_BLOCK,),
        in_specs=in_specs,
        out_specs=out_specs,
        out_shape=out_shape,
        compiler_params=_cparams(1),
    )(*args)


FFN_TM = 512
FFN_TF = 512
FFN_TN = 512


def _expert_changed(te_ref, i):
    prev = te_ref[jnp.maximum(i - 1, 0)]
    return jnp.logical_or(i == 0, te_ref[i] != prev)


def _ffn_up_kernel(te_ref, nu_ref, h_ref, w1_ref, w3_ref, o_ref, w1b_ref, w3b_ref):
    i = pl.program_id(1)

    @pl.when(_expert_changed(te_ref, i))
    def _():
        w1b_ref[...] = w1_ref[...].astype(BF16)
        w3b_ref[...] = w3_ref[...].astype(BF16)

    @pl.when(i < nu_ref[0])
    def _():
        h = h_ref[...]
        a = jnp.dot(h, w1b_ref[...], preferred_element_type=F32)
        b = jnp.dot(h, w3b_ref[...], preferred_element_type=F32)
        o_ref[...] = (_silu(a) * b).astype(o_ref.dtype)

    @pl.when(i >= nu_ref[0])
    def _():
        o_ref[...] = jnp.zeros_like(o_ref)


def _ffn_down_kernel(te_ref, nu_ref, g_ref, w2_ref, o_ref, w2b_ref):
    i = pl.program_id(1)

    @pl.when(_expert_changed(te_ref, i))
    def _():
        w2b_ref[...] = w2_ref[...].astype(BF16)

    @pl.when(i < nu_ref[0])
    def _():
        o_ref[...] = jnp.dot(g_ref[...], w2b_ref[...], preferred_element_type=F32)

    @pl.when(i >= nu_ref[0])
    def _():
        o_ref[...] = jnp.zeros_like(o_ref)


def _swiglu_tiles(hs, w1, w3, w2, tile_expert, n_used, tm):
    R, D = hs.shape
    _, _, F = w1.shape
    n_tiles = R // tm
    tf = FFN_TF if F % FFN_TF == 0 else F
    tn = FFN_TN
    assert R % tm == 0 and F % tf == 0 and D % tn == 0
    used = lambda i, nu: jnp.minimum(i, nu[0] - 1)
    g = pl.pallas_call(
        _ffn_up_kernel,
        grid_spec=pltpu.PrefetchScalarGridSpec(
            num_scalar_prefetch=2,
            grid=(F // tf, n_tiles),
            in_specs=[pl.BlockSpec((tm, D), lambda j, i, te, nu: (used(i, nu), 0)),
                      pl.BlockSpec((None, D, tf), lambda j, i, te, nu: (te[used(i, nu)], 0, j)),
                      pl.BlockSpec((None, D, tf), lambda j, i, te, nu: (te[used(i, nu)], 0, j))],
            out_specs=pl.BlockSpec((tm, tf), lambda j, i, te, nu: (i, j)),
            scratch_shapes=[pltpu.VMEM((D, tf), BF16), pltpu.VMEM((D, tf), BF16)]),
        out_shape=jax.ShapeDtypeStruct((R, F), BF16),
        compiler_params=_cparams(2),
    )(tile_expert, n_used, hs, w1, w3)
    return pl.pallas_call(
        _ffn_down_kernel,
        grid_spec=pltpu.PrefetchScalarGridSpec(
            num_scalar_prefetch=2,
            grid=(D // tn, n_tiles),
            in_specs=[pl.BlockSpec((tm, F), lambda j, i, te, nu: (used(i, nu), 0)),
                      pl.BlockSpec((None, F, tn), lambda j, i, te, nu: (te[used(i, nu)], 0, j))],
            out_specs=pl.BlockSpec((tm, tn), lambda j, i, te, nu: (i, j)),
            scratch_shapes=[pltpu.VMEM((F, tn), BF16)]),
        out_shape=jax.ShapeDtypeStruct((R, D), F32),
        compiler_params=_cparams(2),
    )(tile_expert, n_used, g, w2)


def _row_copy(src_ref, dst_ref, src_row, dst_row, sem):
    return pltpu.make_async_copy(src_ref.at[pl.ds(src_row, 1)], dst_ref.at[pl.ds(dst_row, 1)], sem)


DMA_UNROLL = 8


def _gather_kernel(tok_ref, nu_ref, src_ref, o_ref, buf_ref, sem):
    i = pl.program_id(0)
    tm = buf_ref.shape[0]
    live = i < nu_ref[0]

    def issue(r, c):
        _row_copy(src_ref, buf_ref, tok_ref[i * tm + r], r, sem).start()
        return c

    def drain(r, c):
        _row_copy(src_ref, buf_ref, 0, r, sem).wait()
        return c

    @pl.when(live)
    def _():
        lax.fori_loop(0, tm, issue, 0, unroll=DMA_UNROLL)
        lax.fori_loop(0, tm, drain, 0, unroll=DMA_UNROLL)
        o_ref[...] = buf_ref[...].astype(o_ref.dtype)

    @pl.when(jnp.logical_not(live))
    def _():
        o_ref[...] = jnp.zeros_like(o_ref)


def _gather_rows(src, tok_of_slot, n_used, tm):
    R = tok_of_slot.shape[0]
    D = src.shape[1]
    return pl.pallas_call(
        _gather_kernel,
        grid_spec=pltpu.PrefetchScalarGridSpec(
            num_scalar_prefetch=2,
            grid=(R // tm,),
            in_specs=[pl.BlockSpec(memory_space=pl.ANY)],
            out_specs=pl.BlockSpec((tm, D), lambda i, tok, nu: (i, 0)),
            scratch_shapes=[pltpu.VMEM((tm, D), src.dtype), pltpu.SemaphoreType.DMA(())]),
        out_shape=jax.ShapeDtypeStruct((R, D), BF16),
        compiler_params=_cparams(1),
    )(tok_of_slot, n_used, src)


def _ln2_dense_kernel(x_ref, f_ref, g2_ref, lg_ref, lb_ref, o_ref, *, alpha, n_ctx_blocks, row_off):
    is_ctx = (pl.program_id(0) + row_off) < n_ctx_blocks
    y = alpha * x_ref[...] + _pick(g2_ref[...], is_ctx) * f_ref[...]
    o_ref[...] = _layer_norm(y, lg_ref[...], lb_ref[...])


def _ln2_moe_kernel(sa_ref, sb_ref, x_ref, y_ref, gate_ref, g2_ref, lg_ref, lb_ref, o_ref, bufa_ref, bufb_ref,
                    sem, *, alpha, n_ctx_blocks, row_off):
    i = pl.program_id(0)
    R = ROW_BLOCK
    base = (i + row_off) * R

    def issue(r, c):
        _row_copy(y_ref, bufa_ref, sa_ref[base + r], r, sem).start()
        _row_copy(y_ref, bufb_ref, sb_ref[base + r], r, sem).start()
        return c

    def drain(r, c):
        _row_copy(y_ref, bufa_ref, 0, r, sem).wait()
        _row_copy(y_ref, bufb_ref, 0, r, sem).wait()
        return c

    lax.fori_loop(0, R, issue, 0, unroll=DMA_UNROLL // 2)
    lax.fori_loop(0, R, drain, 0, unroll=DMA_UNROLL // 2)
    gate = gate_ref[...]
    f = gate[:, 0:1] * bufa_ref[...] + gate[:, 1:2] * bufb_ref[...]
    is_ctx = (i + row_off) < n_ctx_blocks
    y = alpha * x_ref[...] + _pick(g2_ref[...], is_ctx) * f
    o_ref[...] = _layer_norm(y, lg_ref[...], lb_ref[...])


def _ln2(x1, f, mod, ln_g, ln_b, layer, alpha, n_ctx_blocks, row_off, moe=None):
    S, D = x1.shape
    L = ln_g.shape[0]
    nb = S // ROW_BLOCK - row_off
    n_pre = 0 if moe is None else 2
    wrap = (lambda f_: (lambda i, *_: f_(i)))
    rows_in = pl.BlockSpec((ROW_BLOCK, D), wrap(lambda i: (i + row_off, 0)))
    rows_out = pl.BlockSpec((ROW_BLOCK, D), wrap(lambda i: (i, 0)))
    vspec = pl.BlockSpec((None, 1, D), wrap(lambda i: (layer, 0, 0)))
    mspec = pl.BlockSpec((None, SUBLANES, D), wrap(lambda i: (layer, 0, 5)))
    common = dict(alpha=alpha, n_ctx_blocks=n_ctx_blocks, row_off=row_off)
    lg, lb = ln_g.reshape(L, 1, D), ln_b.reshape(L, 1, D)
    out_shape = jax.ShapeDtypeStruct((nb * ROW_BLOCK, D), F32)
    if moe is None:
        return pl.pallas_call(
            functools.partial(_ln2_dense_kernel, **common),
            grid=(nb,),
            in_specs=[rows_in, rows_in, mspec, vspec, vspec],
            out_specs=rows_out,
            out_shape=out_shape,
            compiler_params=_cparams(1),
        )(x1, f, mod, lg, lb)
    slot_a, slot_b, gates = moe
    return pl.pallas_call(
        functools.partial(_ln2_moe_kernel, **common),
        grid_spec=pltpu.PrefetchScalarGridSpec(
            num_scalar_prefetch=n_pre,
            grid=(nb,),
            in_specs=[rows_in, pl.BlockSpec(memory_space=pl.ANY),
                      pl.BlockSpec((ROW_BLOCK, TOP_K), wrap(lambda i: (i + row_off, 0))),
                      mspec, vspec, vspec],
            out_specs=rows_out,
            scratch_shapes=[pltpu.VMEM((ROW_BLOCK, D), F32), pltpu.VMEM((ROW_BLOCK, D), F32),
                            pltpu.SemaphoreType.DMA(())]),
        out_shape=out_shape,
        compiler_params=_cparams(1),
    )(slot_a, slot_b, x1, f, gates, mod, lg, lb)


def _route_slots(idx, row0, n_experts, tm):
    S = idx.shape[0]
    n = S - row0
    e_flat = idx[row0:].reshape(-1)
    onehot = (e_flat[:, None] == jnp.arange(n_experts, dtype=jnp.int32)[None, :]).astype(jnp.int32)
    pos = jnp.sum((jnp.cumsum(onehot, axis=0) - 1) * onehot, axis=1)
    counts = jnp.sum(onehot, axis=0)
    padded = ((counts + tm - 1) // tm) * tm
    ends = jnp.cumsum(padded)
    starts = ends - padded
    slot = starts[e_flat] + pos
    n_tiles = -(-(TOP_K * n) // tm) + n_experts
    tok = jnp.repeat(jnp.arange(n, dtype=jnp.int32) + row0, TOP_K)
    tok_of_slot = jnp.full((n_tiles * tm,), row0, jnp.int32).at[slot].set(tok)
    tile_start = jnp.arange(n_tiles, dtype=jnp.int32) * tm
    tile_expert = jnp.minimum(jnp.searchsorted(ends, tile_start, side='right'), n_experts - 1).astype(jnp.int32)
    n_used = (ends[-1] // tm).astype(jnp.int32).reshape(1)
    slot2 = slot.reshape(n, TOP_K).astype(jnp.int32)
    pad = jnp.zeros((row0,), jnp.int32)
    slot_a = jnp.concatenate([pad, slot2[:, 0]])
    slot_b = jnp.concatenate([pad, slot2[:, 1]])
    return tok_of_slot, tile_expert, n_used, slot_a, slot_b


def kernel(x, c, ctx, c_ctx, w_mod, b_mod, w_in, w_out, ln1_g, ln1_b, ln2_g, ln2_b, pool_w, pool_scale,
           diff_lambda, diff_subln_g, conv_dw, conv_db, conv_ln_g, conv_ln_b, conv_pw, s5_a_re, s5_a_im,
           s5_log_dt, s5_b_re, s5_b_im, s5_c_re, s5_c_im, s5_d, s5_glu_w, s5_glu_b, ffn_w1, ffn_w3, ffn_w2,
           router_w, moe_w1, moe_w3, moe_w2):
    B, T, D = x.shape
    Tc = ctx.shape[1]
    depth = w_mod.shape[0]
    assert B == 1 and Tc % ROW_BLOCK == 0 and T % ROW_BLOCK == 0
    GW = D // N_GROUPS
    n_ctx_blocks = Tc // ROW_BLOCK
    alpha = (2.0 * depth) ** 0.25

    cc = jnp.zeros((SUBLANES, D), F32).at[0].set(c[0]).at[1].set(c_ctx)
    mod = _modulation(cc, w_mod, b_mod)
    cos, sin = _rope_tables(T, Tc, LANES)
    xs = jnp.concatenate([ctx[0], x[0]], axis=0)

    POOL_B, Q_B, K_B, V_B, CONV_B, S5_B = 0, 1, 2, 3, 2, 6

    for l in range(depth):
        last = l == depth - 1
        lam_init = 0.8 - 0.6 * math.exp(-0.3 * l)
        z = _in_projection(xs, mod, w_in, l, n_ctx_blocks)

        pa = _pool_mixer(z, pool_w, pool_scale, l, n_ctx_blocks)

        qt, kk, vt, kn2 = _qkv_prep(z, cos, sin, GW, Q_B, K_B, V_B)
        kn = jnp.sqrt(jnp.max(kn2[n_ctx_blocks:, :, 0], axis=0)) * (1.0 + 2.0 ** -6)
        lv = diff_lambda[l].astype(F32)
        lam = jnp.exp(jnp.sum(lv[0] * lv[1])) - jnp.exp(jnp.sum(lv[2] * lv[3])) + lam_init
        lam_row = jnp.full((1, ATTN_TQ), lam, F32)
        g_col = (diff_subln_g[l].astype(F32) * (1.0 - lam_init)).reshape(GW, 1)
        pb = _diff_attention(qt, kk, vt, lam_row, g_col, kn, Tc)

        pcv = _conv_mixer(z, CONV_B, conv_dw, conv_db, conv_ln_g, conv_ln_b, conv_pw, l, n_ctx_blocks)

        seg_len = xs.shape[0] // S5_SEGS
        pf = _s5_params(s5_a_re[l, 0], s5_a_im[l, 0], s5_log_dt[l, 0], s5_b_re[l, 0], s5_b_im[l, 0],
                        s5_c_re[l, 0], s5_c_im[l, 0], seg_len)
        pr = _s5_params(s5_a_re[l, 1], s5_a_im[l, 1], s5_log_dt[l, 1], s5_b_re[l, 1], s5_b_im[l, 1],
                        s5_c_re[l, 1], s5_c_im[l, 1], seg_len)
        yf, yr = _s5_scan(z, S5_B, pf, pr, Tc)
        pd = _s5_glu(yf, yr, z, S5_B, s5_d, s5_glu_w, s5_glu_b, l, n_ctx_blocks)

        w_out_bf = _cast_bf16(w_out, l)
        row_off = n_ctx_blocks if last else 0
        if l % 2 == 0:
            x1, h = _mix_out((pa, pb, pcv, pd), w_out_bf, xs, mod, ln1_g, ln1_b, l, alpha, n_ctx_blocks, None)
            S = xs.shape[0]
            tm = next(t for t in (768, 512, ROW_BLOCK) if S % t == 0)
            n_tiles = S // tm
            f = _swiglu_tiles(h, ffn_w1[l // 2][None], ffn_w3[l // 2][None], ffn_w2[l // 2][None],
                              jnp.zeros((n_tiles,), jnp.int32), jnp.full((1,), n_tiles, jnp.int32), tm)
            xs_new = _ln2(x1, f, mod, ln2_g, ln2_b, l, alpha, n_ctx_blocks, row_off)
        else:
            x1, h, idx, gates = _mix_out((pa, pb, pcv, pd), w_out_bf, xs, mod, ln1_g, ln1_b, l, alpha,
                                         n_ctx_blocks, router_w[l // 2])
            n_exp = router_w.shape[-1]
            row0 = row_off * ROW_BLOCK
            tok_of_slot, tile_expert, n_used, slot_a, slot_b = _route_slots(idx, row0, n_exp, FFN_TM)
            hs = _gather_rows(h, tok_of_slot, n_used, FFN_TM)
            y = _swiglu_tiles(hs, moe_w1[l // 2], moe_w3[l // 2], moe_w2[l // 2], tile_expert, n_used, FFN_TM)
            xs_new = _ln2(x1, y, mod, ln2_g, ln2_b, l, alpha, n_ctx_blocks, row_off, moe=(slot_a, slot_b, gates))
        xs = xs_new
    return xs[None]
```

```python
import functools
import math

import numpy as np
import jax
import jax.numpy as jnp
from jax import lax
from jax.experimental import pallas as pl
from jax.experimental.pallas import tpu as pltpu

F32 = jnp.float32
BF16 = jnp.bfloat16

GRID_W = 64
N_GROUPS = 4
POOL_WINDOWS = (2, 4, 8, 16)
DIFF_HEADS = 8
DIFF_QK = 32
CONV_WIDTH = 31
S5_P = 16
S5_N = 64
TOP_K = 2
ROPE_BASE = 10000.0
LN_EPS = 1e-5

LANES = 128
SUBLANES = 8
ROW_BLOCK = 256
VMEM_LIMIT = 56 * 1024 * 1024


def _cparams(n_axes, vmem=VMEM_LIMIT):
    return pltpu.CompilerParams(dimension_semantics=("arbitrary",) * n_axes, vmem_limit_bytes=vmem)


def _layer_norm(y, g, b):
    mu = jnp.mean(y, -1, keepdims=True)
    yc = y - mu
    var = jnp.mean(yc * yc, -1, keepdims=True)
    return yc * lax.rsqrt(var + LN_EPS) * g + b


def _silu(x):
    return x * jax.nn.sigmoid(x)


def _mod_kernel(cc_ref, w_ref, b_ref, o_ref):
    a = _silu(cc_ref[...])
    o_ref[...] = jnp.dot(a.astype(BF16), w_ref[...].astype(BF16), preferred_element_type=F32) + b_ref[...]


def _modulation(cc, w_mod, b_mod):
    L, D, N = w_mod.shape
    tn = 1536
    assert N % tn == 0
    return pl.pallas_call(
        _mod_kernel,
        grid=(L, N // tn),
        in_specs=[pl.BlockSpec((SUBLANES, D), lambda l, j: (0, 0)),
                  pl.BlockSpec((None, D, tn), lambda l, j: (l, 0, j)),
                  pl.BlockSpec((None, 1, tn), lambda l, j: (l, 0, j))],
        out_specs=pl.BlockSpec((None, SUBLANES, tn), lambda l, j: (l, 0, j)),
        out_shape=jax.ShapeDtypeStruct((L, SUBLANES, N), F32),
        compiler_params=_cparams(2),
    )(cc, w_mod, b_mod.reshape(L, 1, N))


def _mod_spec(layer, chunk, D, n_grid_axes):
    if n_grid_axes == 1:
        return pl.BlockSpec((None, SUBLANES, D), lambda i: (layer, 0, chunk))
    return pl.BlockSpec((None, SUBLANES, D), lambda j, i: (layer, 0, chunk))


def _pick(m, is_ctx):
    return jnp.where(is_ctx, m[1:2, :], m[0:1, :])


def _inproj_kernel(x_ref, sh_ref, sc_ref, w_ref, o_ref, wb_ref, *, n_ctx_blocks):
    i = pl.program_id(1)

    @pl.when(i == 0)
    def _():
        wb_ref[...] = w_ref[...].astype(BF16)

    is_ctx = i < n_ctx_blocks
    h = x_ref[...] * (1.0 + _pick(sc_ref[...], is_ctx)) + _pick(sh_ref[...], is_ctx)
    o_ref[...] = jnp.dot(h.astype(BF16), wb_ref[...], preferred_element_type=F32).astype(o_ref.dtype)


def _in_projection(xs, mod, w_in, layer, n_ctx_blocks):
    S, D = xs.shape
    N = w_in.shape[-1]
    tn = 1792
    assert N % tn == 0 and S % ROW_BLOCK == 0
    return pl.pallas_call(
        functools.partial(_inproj_kernel, n_ctx_blocks=n_ctx_blocks),
        grid=(N // tn, S // ROW_BLOCK),
        in_specs=[pl.BlockSpec((ROW_BLOCK, D), lambda j, i: (i, 0)),
                  _mod_spec(layer, 0, D, 2),
                  _mod_spec(layer, 1, D, 2),
                  pl.BlockSpec((None, D, tn), lambda j, i: (layer, 0, j))],
        out_specs=pl.BlockSpec((ROW_BLOCK, tn), lambda j, i: (i, j)),
        out_shape=jax.ShapeDtypeStruct((S, N), BF16),
        scratch_shapes=[pltpu.VMEM((D, tn), BF16)],
        compiler_params=_cparams(2),
    )(xs, mod, mod, w_in)


def _seq_edges(i, n_blocks, n_ctx_blocks):
    prev_ok = jnp.logical_and(i != 0, i != n_ctx_blocks)
    next_ok = jnp.logical_and(i != n_ctx_blocks - 1, i != n_blocks - 1)
    return prev_ok, next_ok


def _pool_kernel(p_ref, c_ref, n_ref, w_ref, scale_ref, o_ref, ext_ref, *, n_blocks, n_ctx_blocks):
    i = pl.program_id(0)
    R = ROW_BLOCK
    halo = SUBLANES
    prev_ok, next_ok = _seq_edges(i, n_blocks, n_ctx_blocks)
    cur = c_ref[...].astype(F32)
    pack = 2 * SUBLANES
    ext_ref[0:halo, :] = jnp.where(prev_ok, p_ref[R - pack:R, :].astype(F32)[pack - halo:], 0.0)
    ext_ref[halo:halo + R, :] = cur
    ext_ref[halo + R:halo + R + halo, :] = jnp.where(next_ok, n_ref[0:pack, :].astype(F32)[:halo], 0.0)
    rloc = lax.broadcasted_iota(jnp.int32, (R, 1), 0)
    gw = cur.shape[1] // len(POOL_WINDOWS)
    ext = ext_ref[...]
    n_ext = ext.shape[0]
    accs = [jnp.zeros((R, gw), F32) for _ in POOL_WINDOWS]
    for r in range(SUBLANES):
        rolled = ext if r == 0 else pltpu.roll(ext, n_ext - r, 0)
        for g, w in enumerate(POOL_WINDOWS):
            for d in range(-(w // 2), w - w // 2):
                m, rr = divmod(halo + d, SUBLANES)
                if rr == r:
                    accs[g] = accs[g] + rolled[m * SUBLANES:m * SUBLANES + R, g * gw:(g + 1) * gw]
    outs = []
    for g, w in enumerate(POOL_WINDOWS):
        cnt = jnp.zeros((R, 1), F32)
        for d in range(-(w // 2), w - w // 2):
            valid = jnp.logical_and(jnp.logical_or(rloc + d >= 0, prev_ok),
                                    jnp.logical_or(rloc + d < R, next_ok))
            cnt = cnt + valid.astype(F32)
        diff = accs[g] / cnt - cur[:, g * gw:(g + 1) * gw]
        outs.append(jnp.dot(diff.astype(BF16), w_ref[g].astype(BF16), preferred_element_type=F32))
    o_ref[...] = (jnp.concatenate(outs, axis=-1) * scale_ref[...]).astype(o_ref.dtype)


def _pool_mixer(z, pool_w, pool_scale, layer, n_ctx_blocks):
    S = z.shape[0]
    nb = S // ROW_BLOCK
    GW = pool_scale.shape[-1]
    G, gw = pool_w.shape[1], pool_w.shape[2]
    return pl.pallas_call(
        functools.partial(_pool_kernel, n_blocks=nb, n_ctx_blocks=n_ctx_blocks),
        grid=(nb,),
        in_specs=[pl.BlockSpec((ROW_BLOCK, GW), lambda i: (jnp.maximum(i - 1, 0), 0)),
                  pl.BlockSpec((ROW_BLOCK, GW), lambda i: (i, 0)),
                  pl.BlockSpec((ROW_BLOCK, GW), lambda i: (jnp.minimum(i + 1, nb - 1), 0)),
                  pl.BlockSpec((None, G, gw, gw), lambda i: (layer, 0, 0, 0)),
                  pl.BlockSpec((None, 1, GW), lambda i: (layer, 0, 0))],
        out_specs=pl.BlockSpec((ROW_BLOCK, GW), lambda i: (i, 0)),
        out_shape=jax.ShapeDtypeStruct((S, GW), BF16),
        scratch_shapes=[pltpu.VMEM((ROW_BLOCK + 2 * SUBLANES, GW), F32)],
        compiler_params=_cparams(1),
    )(z, z, z, pool_w, pool_scale.reshape(pool_scale.shape[0], 1, GW))


CONV_HALO = 16


def _conv_kernel(p_ref, c_ref, n_ref, dw_ref, db_ref, g_ref, b_ref, pw_ref, o_ref, ext_ref,
                 *, n_blocks, n_ctx_blocks):
    i = pl.program_id(0)
    R = ROW_BLOCK
    H = CONV_HALO
    GW = o_ref.shape[1]
    prev_ok, next_ok = _seq_edges(i, n_blocks, n_ctx_blocks)

    def glu(u):
        u = u.astype(F32)
        return u[:, :GW] * jax.nn.sigmoid(u[:, GW:])

    ext_ref[0:H, :] = jnp.where(prev_ok, glu(p_ref[R - H:R, :]), 0.0)
    ext_ref[H:H + R, :] = glu(c_ref[...])
    ext_ref[H + R:H + R + H, :] = jnp.where(next_ok, glu(n_ref[0:H, :]), 0.0)
    off = H - CONV_WIDTH // 2
    ext = ext_ref[...]
    n_ext = ext.shape[0]
    acc = jnp.zeros((R, GW), F32)
    for r in range(SUBLANES):
        rolled = ext if r == 0 else pltpu.roll(ext, n_ext - r, 0)
        for j in range(CONV_WIDTH):
            m, rr = divmod(off + j, SUBLANES)
            if rr == r:
                acc = acc + rolled[m * SUBLANES:m * SUBLANES + R, :] * dw_ref[j:j + 1, :]
    y = _layer_norm(acc + db_ref[...], g_ref[...], b_ref[...])
    o_ref[...] = jnp.dot(_silu(y).astype(BF16), pw_ref[...].astype(BF16),
                         preferred_element_type=F32).astype(o_ref.dtype)


def _conv_mixer(z, col_block, conv_dw, conv_db, conv_ln_g, conv_ln_b, conv_pw, layer, n_ctx_blocks):
    S = z.shape[0]
    nb = S // ROW_BLOCK
    GW = conv_db.shape[-1]
    L = conv_db.shape[0]
    vec = lambda a: a.reshape(L, 1, GW)
    vspec = pl.BlockSpec((None, 1, GW), lambda i: (layer, 0, 0))
    return pl.pallas_call(
        functools.partial(_conv_kernel, n_blocks=nb, n_ctx_blocks=n_ctx_blocks),
        grid=(nb,),
        in_specs=[pl.BlockSpec((ROW_BLOCK, 2 * GW), lambda i: (jnp.maximum(i - 1, 0), col_block)),
                  pl.BlockSpec((ROW_BLOCK, 2 * GW), lambda i: (i, col_block)),
                  pl.BlockSpec((ROW_BLOCK, 2 * GW), lambda i: (jnp.minimum(i + 1, nb - 1), col_block)),
                  pl.BlockSpec((None, CONV_WIDTH, GW), lambda i: (layer, 0, 0)),
                  vspec, vspec, vspec,
                  pl.BlockSpec((None, GW, GW), lambda i: (layer, 0, 0))],
        out_specs=pl.BlockSpec((ROW_BLOCK, GW), lambda i: (i, 0)),
        out_shape=jax.ShapeDtypeStruct((S, GW), BF16),
        scratch_shapes=[pltpu.VMEM((ROW_BLOCK + 2 * CONV_HALO, GW), F32)],
        compiler_params=_cparams(1),
    )(z, z, z, conv_dw, vec(conv_db), vec(conv_ln_g), vec(conv_ln_b), conv_pw)


def _rope_tables(T, Tc, width):
    ax = DIFF_QK // 2
    inv = ROPE_BASE ** (-jnp.arange(0, ax, 2, dtype=F32) / ax)
    t = jnp.arange(T)
    row = (t // GRID_W).astype(F32)
    col = (t % GRID_W).astype(F32)
    ang = jnp.stack([row[:, None] * inv, col[:, None] * inv], axis=1)
    cos = jnp.cos(ang)[:, :, None, :]
    sin = jnp.sin(ang)[:, :, None, :]
    cos = jnp.broadcast_to(cos, (T, 2, 2, ax // 2)).reshape(T, DIFF_QK)
    sin = jnp.concatenate([-sin, sin], axis=2).reshape(T, DIFF_QK)
    reps = width // DIFF_QK
    cos = jnp.concatenate([jnp.ones((Tc, DIFF_QK), F32), cos], axis=0)
    sin = jnp.concatenate([jnp.zeros((Tc, DIFF_QK), F32), sin], axis=0)
    return jnp.tile(cos, (1, reps)), jnp.tile(sin, (1, reps))


def _qkv_prep_kernel(q_ref, k_ref, v_ref, cos_ref, sin_ref, qo_ref, ko_ref, vo_ref, kn_ref):
    W = q_ref.shape[1]
    half = DIFF_QK // 4
    lane = lax.broadcasted_iota(jnp.int32, (1, W), 1)
    first = (lane % (2 * half)) < half
    cos = jnp.concatenate([cos_ref[...]] * (W // LANES), axis=1)
    sin = jnp.concatenate([sin_ref[...]] * (W // LANES), axis=1)

    def rope(x):
        partner = jnp.where(first, pltpu.roll(x, W - half, 1), pltpu.roll(x, half, 1))
        return x * cos + partner * sin

    qo_ref[...] = (rope(q_ref[...].astype(F32)) * (DIFF_QK ** -0.5 * math.log2(math.e))).T.astype(BF16)
    kb = rope(k_ref[...].astype(F32)).astype(BF16)
    ko_ref[...] = kb
    ksq = kb.astype(F32).T
    ksq = ksq * ksq
    for grp in range(W // DIFF_QK):
        n2 = jnp.sum(ksq[grp * DIFF_QK:(grp + 1) * DIFF_QK, :], axis=0, keepdims=True)
        kn_ref[grp:grp + 1, :] = jnp.broadcast_to(jnp.max(n2, axis=1, keepdims=True), (1, LANES))
    vt = v_ref[...].astype(F32).T.astype(BF16)
    dv = LANES // 2
    ones = jnp.ones((ATTN_VROWS - dv, vt.shape[1]), BF16)
    for h in range(W // dv):
        vo_ref[h * ATTN_VROWS:h * ATTN_VROWS + dv, :] = vt[h * dv:(h + 1) * dv, :]
        vo_ref[h * ATTN_VROWS + dv:(h + 1) * ATTN_VROWS, :] = ones


def _qkv_prep(z, cos, sin, W, q_blk, k_blk, v_blk):
    S = z.shape[0]
    nb = S // ROW_BLOCK
    row = lambda c: pl.BlockSpec((ROW_BLOCK, W), lambda i: (i, c))
    tab = pl.BlockSpec((ROW_BLOCK, LANES), lambda i: (i, 0))
    return pl.pallas_call(
        _qkv_prep_kernel,
        grid=(nb,),
        in_specs=[row(q_blk), row(k_blk), row(v_blk), tab, tab],
        out_specs=[pl.BlockSpec((W, ROW_BLOCK), lambda i: (0, i)), row(0),
                   pl.BlockSpec((DIFF_HEADS * ATTN_VROWS, ROW_BLOCK), lambda i: (0, i)),
                   pl.BlockSpec((None, W // DIFF_QK, LANES), lambda i: (i, 0, 0))],
        out_shape=[jax.ShapeDtypeStruct((W, S), BF16), jax.ShapeDtypeStruct((S, W), BF16),
                   jax.ShapeDtypeStruct((DIFF_HEADS * ATTN_VROWS, S), BF16),
                   jax.ShapeDtypeStruct((nb, W // DIFF_QK, LANES), F32)],
        compiler_params=_cparams(1),
    )(z, z, z, cos, sin)


ATTN_TQ = 256
ATTN_VROWS = LANES // 2 + 2 * SUBLANES
ATTN_MARGIN = 64.0
ATTN_TK = (4096, 2048, 1024)


def _attn_kernel(qt_ref, k_ref, vt_ref, lam_ref, g_ref, kn_ref, o_ref, qq_ref, m_ref, acc_ref,
                 *, n_ctx, n_ctx_blocks, n_lat_chunks, tk):
    i = pl.program_id(1)
    tq = ATTN_TQ
    dv = LANES // 2
    qt = qt_ref[...]
    feat = lax.broadcasted_iota(jnp.int32, (LANES, 1), 0)
    zero = jnp.zeros_like(qt)
    for hh in range(2):
        for comp in range(2):
            lo = hh * dv + comp * DIFF_QK
            keep = jnp.logical_and(feat >= lo, feat < lo + DIFF_QK)
            qq_ref[hh, :, comp * tq:(comp + 1) * tq] = jnp.where(keep, qt, zero)

    def attend(start, size, mode):
        kk = k_ref[pl.ds(start, size), :]
        for hh in range(2):
            s = jnp.dot(kk, qq_ref[hh], preferred_element_type=F32)
            vv = vt_ref[hh * ATTN_VROWS:(hh + 1) * ATTN_VROWS, pl.ds(start, size)]
            mx = jnp.max(s, axis=0, keepdims=True)
            if mode == "first":
                m_ref[hh] = mx
                p = jnp.exp2(s - mx)
                acc_ref[hh] = jnp.dot(vv, p.astype(BF16), preferred_element_type=F32)
            elif mode == "exact":
                m_old = m_ref[hh]
                m_new = jnp.maximum(m_old, mx)
                m_ref[hh] = m_new
                p = jnp.exp2(s - m_new)
                acc_ref[hh] = (jnp.exp2(m_old - m_new) * acc_ref[hh]
                               + jnp.dot(vv, p.astype(BF16), preferred_element_type=F32))
            else:
                m_old = m_ref[hh]
                p = jnp.exp2(s - m_old)
                m_new = jnp.maximum(m_old, mx)
                m_ref[hh] = m_new
                acc_ref[hh] = (jnp.exp2(m_old - m_new)
                               * (acc_ref[hh] + jnp.dot(vv, p.astype(BF16), preferred_element_type=F32)))

    attend(0, n_ctx, "first")
    n_steps = jnp.where(i < n_ctx_blocks, 0, n_lat_chunks)

    pair = pl.program_id(0)
    col = lax.broadcasted_iota(jnp.int32, (1, 2 * tq), 1)
    excess = jnp.full((1, 2 * tq), -jnp.inf, F32)
    for hh in range(2):
        qf = qq_ref[hh].astype(F32)
        qn = jnp.sqrt(jnp.sum(qf * qf, axis=0, keepdims=True))
        grp = (2 * pair + hh) * 2
        kn = jnp.where(col < tq, kn_ref[grp], kn_ref[grp + 1])
        excess = jnp.maximum(excess, qn * kn - m_ref[hh])
    safe = jnp.max(excess) < ATTN_MARGIN

    def loop(mode):
        def body(c, carry):
            attend(pl.multiple_of(n_ctx + c * tk, LANES), tk, mode)
            return carry
        lax.fori_loop(0, n_steps, body, 0)

    @pl.when(safe)
    def _():
        loop("deferred")

    @pl.when(jnp.logical_not(safe))
    def _():
        loop("exact")

    lam = lam_ref[...]
    outs = []
    for hh in range(2):
        acc = acc_ref[hh]
        ratio = acc[:dv] / acc[dv:dv + 1]
        o = ratio[:, :tq] - lam * ratio[:, tq:]
        r = lax.rsqrt(jnp.sum(o * o, axis=0, keepdims=True) / dv + LN_EPS)
        outs.append(o * r)
    o_ref[...] = (jnp.concatenate(outs, axis=0) * g_ref[...]).T.astype(o_ref.dtype)


def _diff_attention(qt, k, vt, lam, g, kn, n_ctx):
    W, S = qt.shape
    assert S % ATTN_TQ == 0 and n_ctx % ATTN_TQ == 0
    tk = next(t for t in ATTN_TK if (S - n_ctx) % t == 0)
    nq = S // ATTN_TQ
    return pl.pallas_call(
        functools.partial(_attn_kernel, n_ctx=n_ctx, n_ctx_blocks=n_ctx // ATTN_TQ,
                          n_lat_chunks=(S - n_ctx) // tk, tk=tk),
        grid=(W // LANES, nq),
        in_specs=[pl.BlockSpec((LANES, ATTN_TQ), lambda p, i: (p, i)),
                  pl.BlockSpec((S, LANES), lambda p, i: (0, p)),
                  pl.BlockSpec((2 * ATTN_VROWS, S), lambda p, i: (p, 0)),
                  pl.BlockSpec((1, ATTN_TQ), lambda p, i: (0, 0)),
                  pl.BlockSpec((LANES, 1), lambda p, i: (p, 0)),
                  pl.BlockSpec(memory_space=pltpu.SMEM)],
        out_specs=pl.BlockSpec((ATTN_TQ, LANES), lambda p, i: (i, p)),
        out_shape=jax.ShapeDtypeStruct((S, W), BF16),
        scratch_shapes=[pltpu.VMEM((2, LANES, 2 * ATTN_TQ), BF16),
                        pltpu.VMEM((2, 1, 2 * ATTN_TQ), F32),
                        pltpu.VMEM((2, ATTN_VROWS, 2 * ATTN_TQ), F32)],
        compiler_params=_cparams(2),
    )(qt, k, vt, lam, g, kn)


S5_SEGS = SUBLANES
S5_KB = 32
S5_GB = 8


def _s5_params(a_re, a_im, log_dt, b_re, b_im, c_re, c_im, seg_len):
    G, N = a_re.shape
    P = b_re.shape[-1]
    nblk = G // S5_GB
    a_re, a_im = a_re.astype(F32), a_im.astype(F32)
    dt = jnp.exp(log_dt.astype(F32))[:, None]
    lr, li = dt * a_re, dt * a_im
    mag = jnp.exp(lr)
    ar, ai = mag * jnp.cos(li), mag * jnp.sin(li)
    den = a_re * a_re + a_im * a_im
    qr = ((ar - 1.0) * a_re + ai * a_im) / den
    qi = (ai * a_re - (ar - 1.0) * a_im) / den
    b_re, b_im = b_re.astype(F32), b_im.astype(F32)
    br = qr[..., None] * b_re - qi[..., None] * b_im
    bi = qr[..., None] * b_im + qi[..., None] * b_re
    mag_l = jnp.exp(seg_len * lr)
    alr, ali = mag_l * jnp.cos(seg_len * li), mag_l * jnp.sin(seg_len * li)
    eye = jnp.eye(S5_GB, dtype=F32)
    wb = lambda m: jnp.einsum('gh,bgnp->bgphn', eye, m.reshape(nblk, S5_GB, N, P)).reshape(
        nblk, S5_GB * P, S5_GB * N)
    w_in = jnp.concatenate([wb(br), wb(bi)], axis=2)
    cm = lambda m: jnp.einsum('gh,bgpn->bhngp', eye, m.astype(F32).reshape(nblk, S5_GB, P, N)).reshape(
        nblk, S5_GB * N, S5_GB * P)
    w_out = jnp.concatenate([cm(c_re), -cm(c_im)], axis=1)
    row = lambda r, i: jnp.concatenate([r.reshape(1, G * N), i.reshape(1, G * N)], axis=1)
    coef = jnp.broadcast_to(row(ar, ai), (S5_SEGS, 2 * G * N))
    return w_in.astype(BF16), coef, row(alr, ali), w_out.astype(BF16)


def _s5_kernel(*refs, emit_out):
    n_seg = S5_SEGS
    uf_ref = refs[0]
    ur_refs = refs[1:1 + n_seg]
    rest = refs[1 + n_seg:]
    if emit_out:
        (wbf_ref, wbr_ref, af_ref, ar_ref, ef_ref, er_ref, alf_ref, alr_ref, cf_ref, cr_ref,
         yf_ref, yr_ref, stage_ref, bf_ref, br_ref, hf_ref, hr_ref) = rest
    else:
        (wbf_ref, wbr_ref, af_ref, ar_ref, ef_out_ref, er_out_ref,
         stage_ref, bf_ref, br_ref, hf_ref, hr_ref) = rest
    g = pl.program_id(0)
    KB = S5_KB
    R = n_seg * KB
    NS = af_ref.shape[1] // 2
    nblk = wbf_ref.shape[0]
    wi = wbf_ref.shape[1]
    ws = wbf_ref.shape[2] // 2

    def cmul_add(a_row, h, add):
        are, aim = a_row[:, :NS], a_row[:, NS:]
        hre, him = h[:, :NS], h[:, NS:]
        return jnp.concatenate([are * hre - aim * him + add[:, :NS], are * him + aim * hre + add[:, NS:]], axis=1)

    @pl.when(g == 0)
    def _():
        if emit_out:
            def chain(e_ref, al_ref, order):
                al = al_ref[...]
                c = jnp.zeros((1, 2 * NS), F32)
                rows = [None] * n_seg
                for s in order:
                    rows[s] = c
                    c = cmul_add(al, c, e_ref[s:s + 1, :])
                return jnp.concatenate(rows, axis=0)
            hf_ref[...] = chain(ef_ref, alf_ref, range(n_seg))
            hr_ref[...] = chain(er_ref, alr_ref, range(n_seg - 1, -1, -1))
        else:
            hf_ref[...] = jnp.zeros_like(hf_ref)
            hr_ref[...] = jnp.zeros_like(hr_ref)

    def interleaved(load_seg):
        for s in range(n_seg):
            blk = load_seg(s).astype(F32)
            for c in range(nblk):
                stage_ref[c, s * KB:(s + 1) * KB, :] = blk[:, c * wi:(c + 1) * wi]
        rows = [jnp.concatenate([stage_ref[c, pl.ds(kk, n_seg, stride=KB), :] for c in range(nblk)], axis=1)
                for kk in range(KB)]
        return jnp.concatenate(rows, axis=0).astype(BF16)

    def project_in(u, w_ref, buf_ref):
        for b in range(nblk):
            res = jnp.dot(u[:, b * wi:(b + 1) * wi], w_ref[b], preferred_element_type=F32)
            buf_ref[:, b * ws:(b + 1) * ws] = res[:, :ws]
            buf_ref[:, NS + b * ws:NS + (b + 1) * ws] = res[:, ws:]

    def scan(buf_ref, a_ref, h_ref, reverse):
        half = NS // 2
        for c in range(2):
            cre = slice(c * half, (c + 1) * half)
            cim = slice(NS + c * half, NS + (c + 1) * half)
            are, aim = a_ref[:, cre], a_ref[:, cim]

            def step(t, carry, cre=cre, cim=cim, are=are, aim=aim):
                hre, him = carry
                kk = (KB - 1 - t) if reverse else t
                r0 = pl.multiple_of(kk * n_seg, n_seg)
                nre = are * hre - aim * him + buf_ref[pl.ds(r0, n_seg), cre]
                nim = are * him + aim * hre + buf_ref[pl.ds(r0, n_seg), cim]
                if emit_out:
                    buf_ref[pl.ds(r0, n_seg), cre] = nre
                    buf_ref[pl.ds(r0, n_seg), cim] = nim
                return nre, nim

            hre, him = lax.fori_loop(0, KB, step, (h_ref[:, cre], h_ref[:, cim]))
            h_ref[:, cre] = hre
            h_ref[:, cim] = him

    def project_out(buf_ref, c_ref, y_ref):
        for b in range(nblk):
            hcat = jnp.concatenate([buf_ref[:, b * ws:(b + 1) * ws], buf_ref[:, NS + b * ws:NS + (b + 1) * ws]],
                                   axis=1).astype(BF16)
            stage_ref[b] = jnp.dot(hcat, c_ref[b], preferred_element_type=F32)
        for s in range(n_seg):
            y_ref[s] = jnp.concatenate([stage_ref[c, pl.ds(s, KB, stride=n_seg), :] for c in range(nblk)], axis=1)

    project_in(interleaved(lambda s: uf_ref[s]), wbf_ref, bf_ref)
    scan(bf_ref, af_ref, hf_ref, False)
    if emit_out:
        project_out(bf_ref, cf_ref, yf_ref)
    project_in(interleaved(lambda s: ur_refs[s][...]), wbr_ref, br_ref)
    scan(br_ref, ar_ref, hr_ref, True)
    if emit_out:
        project_out(br_ref, cr_ref, yr_ref)
    else:
        ef_out_ref[...] = hf_ref[...]
        er_out_ref[...] = hr_ref[...]


def _s5_pass(z, col_block, pf, pr, ends, n_ctx):
    S, NZ = z.shape
    GW = pf[0].shape[1] * pf[0].shape[0]
    NS2 = pf[1].shape[1]
    seg_len = S // S5_SEGS
    steps = seg_len // S5_KB
    nblocks = S // S5_KB
    assert S % (S5_SEGS * S5_KB) == 0 and n_ctx % S5_KB == 0
    ctx_blocks = n_ctx // S5_KB
    emit_out = ends is not None
    z4 = z.reshape(S5_SEGS, steps, S5_KB, NZ)
    z3 = z.reshape(nblocks, S5_KB, NZ)

    def rev_spec(s):
        return pl.BlockSpec((None, S5_KB, GW),
                            lambda g: ((s * steps + steps - 1 - g + ctx_blocks) % nblocks, 0, col_block))

    const = lambda a: pl.BlockSpec(a.shape, lambda g: (0,) * a.ndim)
    in_specs = [pl.BlockSpec((S5_SEGS, None, S5_KB, GW), lambda g: (0, g, 0, col_block))]
    in_specs += [rev_spec(s) for s in range(S5_SEGS)]
    args = [z4] + [z3] * S5_SEGS
    weights = [pf[0], pr[0], pf[1], pr[1]]
    if emit_out:
        weights += [ends[0], ends[1], pf[2], pr[2], pf[3], pr[3]]
    in_specs += [const(a) for a in weights]
    args += weights
    scratch = [pltpu.VMEM((pf[0].shape[0], S5_SEGS * S5_KB, pf[0].shape[1]), F32),
               pltpu.VMEM((S5_SEGS * S5_KB, NS2), F32), pltpu.VMEM((S5_SEGS * S5_KB, NS2), F32),
               pltpu.VMEM((S5_SEGS, NS2), F32), pltpu.VMEM((S5_SEGS, NS2), F32)]
    if emit_out:
        yshape = jax.ShapeDtypeStruct((S5_SEGS, steps, S5_KB, GW), F32)
        out_shape = [yshape, yshape]
        out_specs = [pl.BlockSpec((S5_SEGS, None, S5_KB, GW), lambda g: (0, g, 0, 0)),
                     pl.BlockSpec((S5_SEGS, None, S5_KB, GW), lambda g: (0, steps - 1 - g, 0, 0))]
    else:
        eshape = jax.ShapeDtypeStruct((S5_SEGS, NS2), F32)
        out_shape = [eshape, eshape]
        out_specs = [pl.BlockSpec((S5_SEGS, NS2), lambda g: (0, 0))] * 2
    return pl.pallas_call(
        functools.partial(_s5_kernel, emit_out=emit_out),
        grid=(steps,),
        in_specs=in_specs,
        out_specs=out_specs,
        out_shape=out_shape,
        scratch_shapes=scratch,
        compiler_params=_cparams(1),
    )(*args)


def _s5_scan(z, col_block, pf, pr, n_ctx):
    S = z.shape[0]
    GW = pf[0].shape[1] * pf[0].shape[0]
    ends = _s5_pass(z, col_block, pf, pr, None, n_ctx)
    yf, yr = _s5_pass(z, col_block, pf, pr, ends, n_ctx)
    return yf.reshape(S, GW), yr.reshape(S, GW)


def _s5_glu_kernel(yf_ref, yr_ref, u_ref, d_ref, w_ref, b_ref, o_ref):
    y = yf_ref[...] + yr_ref[...] + d_ref[...] * u_ref[...].astype(F32)
    zz = jax.nn.gelu(y)
    gate = jnp.dot(zz.astype(BF16), w_ref[...].astype(BF16), preferred_element_type=F32) + b_ref[...]
    o_ref[...] = (zz * jax.nn.sigmoid(gate)).astype(o_ref.dtype)


def _s5_glu(yf, yr, z, col_block, s5_d, glu_w, glu_b, layer, n_ctx_blocks):
    S, GW = yf.shape
    L = s5_d.shape[0]
    nb = S // ROW_BLOCK
    row = lambda c: pl.BlockSpec((ROW_BLOCK, GW), lambda i: (i, c))
    rot = pl.BlockSpec((ROW_BLOCK, GW), lambda i: ((i + nb - n_ctx_blocks) % nb, 0))
    vspec = pl.BlockSpec((None, 1, GW), lambda i: (layer, 0, 0))
    return pl.pallas_call(
        _s5_glu_kernel,
        grid=(nb,),
        in_specs=[row(0), rot, row(col_block), vspec,
                  pl.BlockSpec((None, GW, GW), lambda i: (layer, 0, 0)), vspec],
        out_specs=row(0),
        out_shape=jax.ShapeDtypeStruct((S, GW), BF16),
        compiler_params=_cparams(1),
    )(yf, yr, z, s5_d.reshape(L, 1, GW), glu_w, glu_b.reshape(L, 1, GW))


def _cast_kernel(x_ref, o_ref):
    o_ref[...] = x_ref[...].astype(o_ref.dtype)


def _cast_bf16(w, layer):
    _, K, N = w.shape
    tk = 512
    return pl.pallas_call(
        _cast_kernel,
        grid=(K // tk,),
        in_specs=[pl.BlockSpec((None, tk, N), lambda i: (layer, i, 0))],
        out_specs=pl.BlockSpec((tk, N), lambda i: (i, 0)),
        out_shape=jax.ShapeDtypeStruct((K, N), BF16),
        compiler_params=_cparams(1),
    )(w)


def _mixout_kernel(*refs, alpha, n_ctx_blocks, route, h_dtype):
    (pa_ref, pb_ref, pc_ref, pd_ref, w_ref, x_ref, g1_ref, lg_ref, lb_ref, sh_ref, sc_ref) = refs[:11]
    if route:
        rw_ref, x1_ref, h_ref, idx_ref, gate_ref = refs[11:]
    else:
        x1_ref, h_ref = refs[11:]
    i = pl.program_id(0)
    is_ctx = i < n_ctx_blocks
    GW = pa_ref.shape[1]
    mix = jnp.zeros(x_ref.shape, F32)
    for k, p_ref in enumerate((pa_ref, pb_ref, pc_ref, pd_ref)):
        mix = mix + jnp.dot(p_ref[...].astype(BF16), w_ref[k * GW:(k + 1) * GW, :],
                            preferred_element_type=F32)
    y = alpha * x_ref[...] + _pick(g1_ref[...], is_ctx) * mix
    x1 = _layer_norm(y, lg_ref[...], lb_ref[...])
    x1_ref[...] = x1
    h = x1 * (1.0 + _pick(sc_ref[...], is_ctx)) + _pick(sh_ref[...], is_ctx)
    h_ref[...] = h.astype(h_dtype)
    if route:
        rw = rw_ref[...]
        h_hi = h.astype(BF16)
        h_lo = (h - h_hi.astype(F32)).astype(BF16)
        w_hi = rw.astype(BF16)
        w_lo = (rw - w_hi.astype(F32)).astype(BF16)
        logits = (jnp.dot(h_hi, w_hi, preferred_element_type=F32)
                  + (jnp.dot(h_lo, w_hi, preferred_element_type=F32)
                     + jnp.dot(h_hi, w_lo, preferred_element_type=F32)))
        n_exp = rw_ref.shape[1]
        lane = lax.broadcasted_iota(jnp.int32, logits.shape, 1)
        m1 = jnp.max(logits, axis=-1, keepdims=True)
        i1 = jnp.min(jnp.where(logits == m1, lane, n_exp), axis=-1, keepdims=True)
        rest = jnp.where(lane == i1, -jnp.inf, logits)
        m2 = jnp.max(rest, axis=-1, keepdims=True)
        i2 = jnp.min(jnp.where(rest == m2, lane, n_exp), axis=-1, keepdims=True)
        e2 = jnp.exp(m2 - m1)
        idx_ref[...] = jnp.concatenate([i1, i2], axis=1)
        gate_ref[...] = jnp.concatenate([1.0 / (1.0 + e2), e2 / (1.0 + e2)], axis=1)


def _mix_out(parts, w_out_bf, xs, mod, ln_g, ln_b, layer, alpha, n_ctx_blocks, router_w):
    S, D = xs.shape
    GW = parts[0].shape[1]
    L = ln_g.shape[0]
    route = router_w is not None
    h_dtype = F32 if route else BF16
    part = pl.BlockSpec((ROW_BLOCK, GW), lambda i: (i, 0))
    rows = pl.BlockSpec((ROW_BLOCK, D), lambda i: (i, 0))
    vspec = pl.BlockSpec((None, 1, D), lambda i: (layer, 0, 0))
    in_specs = [part, part, part, part,
                pl.BlockSpec((D, D), lambda i: (0, 0)), rows,
                _mod_spec(layer, 2, D, 1), vspec, vspec, _mod_spec(layer, 3, D, 1), _mod_spec(layer, 4, D, 1)]
    args = list(parts) + [w_out_bf, xs, mod, ln_g.reshape(L, 1, D), ln_b.reshape(L, 1, D), mod, mod]
    out_specs = [rows, rows]
    out_shape = [jax.ShapeDtypeStruct((S, D), F32), jax.ShapeDtypeStruct((S, D), h_dtype)]
    if route:
        E = router_w.shape[-1]
        in_specs.append(pl.BlockSpec((D, E), lambda i: (0, 0)))
        args.append(router_w)
        out_specs += [pl.BlockSpec((ROW_BLOCK, TOP_K), lambda i: (i, 0))] * 2
        out_shape += [jax.ShapeDtypeStruct((S, TOP_K), jnp.int32), jax.ShapeDtypeStruct((S, TOP_K), F32)]
    return pl.pallas_call(
        functools.partial(_mixout_kernel, alpha=alpha, n_ctx_blocks=n_ctx_blocks, route=route, h_dtype=h_dtype),
        grid=(S // ROW_BLOCK,),
        in_specs=in_specs,
        out_specs=out_specs,
        out_shape=out_shape,
        compiler_params=_cparams(1),
    )(*args)


FFN_TM = 512
FFN_TF = 512
FFN_TN = 512


def _expert_changed(te_ref, i):
    prev = te_ref[jnp.maximum(i - 1, 0)]
    return jnp.logical_or(i == 0, te_ref[i] != prev)


def _ffn_up_kernel(te_ref, src_ref, live_ref, h_ref, w1_ref, w3_ref, o_ref, w1b_ref, w3b_ref):
    i = pl.program_id(1)
    half = h_ref.shape[0] // 2

    @pl.when(_expert_changed(te_ref, i))
    def _():
        w1b_ref[...] = w1_ref[...].astype(BF16)
        w3b_ref[...] = w3_ref[...].astype(BF16)

    def act(h):
        a = jnp.dot(h, w1b_ref[...], preferred_element_type=F32)
        b = jnp.dot(h, w3b_ref[...], preferred_element_type=F32)
        return (_silu(a) * b).astype(o_ref.dtype)

    @pl.when(live_ref[i] == 2)
    def _():
        o_ref[...] = act(h_ref[...])

    @pl.when(live_ref[i] == 1)
    def _():
        o_ref[0:half, :] = act(h_ref[0:half, :])
        o_ref[half:, :] = jnp.zeros((half, o_ref.shape[1]), o_ref.dtype)

    @pl.when(live_ref[i] == 0)
    def _():
        o_ref[...] = jnp.zeros_like(o_ref)


def _ffn_down_kernel(te_ref, src_ref, live_ref, g_ref, w2_ref, o_ref, w2b_ref):
    i = pl.program_id(1)

    @pl.when(_expert_changed(te_ref, i))
    def _():
        w2b_ref[...] = w2_ref[...].astype(BF16)

    @pl.when(live_ref[i] > 0)
    def _():
        o_ref[...] = jnp.dot(g_ref[...], w2b_ref[...], preferred_element_type=F32)

    @pl.when(live_ref[i] == 0)
    def _():
        o_ref[...] = jnp.zeros_like(o_ref)


def _swiglu_tiles(hs, w1, w3, w2, up, down):
    R, D = hs.shape
    _, _, F = w1.shape
    tf = FFN_TF if F % FFN_TF == 0 else F
    tn = FFN_TN
    tm_u, te_u, src_u, live_u = up
    tm_d, te_d, src_d, live_d = down
    assert R % tm_u == 0 and R % tm_d == 0 and F % tf == 0 and D % tn == 0
    g = pl.pallas_call(
        _ffn_up_kernel,
        grid_spec=pltpu.PrefetchScalarGridSpec(
            num_scalar_prefetch=3,
            grid=(F // tf, R // tm_u),
            in_specs=[pl.BlockSpec((tm_u, D), lambda j, i, te, src, lv: (src[i], 0)),
                      pl.BlockSpec((None, D, tf), lambda j, i, te, src, lv: (te[i], 0, j)),
                      pl.BlockSpec((None, D, tf), lambda j, i, te, src, lv: (te[i], 0, j))],
            out_specs=pl.BlockSpec((tm_u, tf), lambda j, i, te, src, lv: (i, j)),
            scratch_shapes=[pltpu.VMEM((D, tf), BF16), pltpu.VMEM((D, tf), BF16)]),
        out_shape=jax.ShapeDtypeStruct((R, F), BF16),
        compiler_params=_cparams(2),
    )(te_u, src_u, live_u, hs, w1, w3)
    return pl.pallas_call(
        _ffn_down_kernel,
        grid_spec=pltpu.PrefetchScalarGridSpec(
            num_scalar_prefetch=3,
            grid=(D // tn, R // tm_d),
            in_specs=[pl.BlockSpec((tm_d, F), lambda j, i, te, src, lv: (src[i], 0)),
                      pl.BlockSpec((None, F, tn), lambda j, i, te, src, lv: (te[i], 0, j))],
            out_specs=pl.BlockSpec((tm_d, tn), lambda j, i, te, src, lv: (i, j)),
            scratch_shapes=[pltpu.VMEM((F, tn), BF16)]),
        out_shape=jax.ShapeDtypeStruct((R, D), F32),
        compiler_params=_cparams(2),
    )(te_d, src_d, live_d, g, w2)


def _row_copy(src_ref, dst_ref, src_row, dst_row, sem):
    return pltpu.make_async_copy(src_ref.at[pl.ds(src_row, 1)], dst_ref.at[pl.ds(dst_row, 1)], sem)


DMA_UNROLL = 8


def _gather_kernel(tok_ref, live_ref, src_ref, o_ref, buf_ref, sem):
    i = pl.program_id(0)
    tm = buf_ref.shape[0]
    live = live_ref[i] > 0

    def issue(r, c):
        _row_copy(src_ref, buf_ref, tok_ref[i * tm + r], r, sem).start()
        return c

    def drain(r, c):
        _row_copy(src_ref, buf_ref, 0, r, sem).wait()
        return c

    @pl.when(live)
    def _():
        lax.fori_loop(0, tm, issue, 0, unroll=DMA_UNROLL)
        lax.fori_loop(0, tm, drain, 0, unroll=DMA_UNROLL)
        o_ref[...] = buf_ref[...].astype(o_ref.dtype)

    @pl.when(jnp.logical_not(live))
    def _():
        o_ref[...] = jnp.zeros_like(o_ref)


def _gather_rows(src, tok_of_slot, live, tm):
    R = tok_of_slot.shape[0]
    D = src.shape[1]
    return pl.pallas_call(
        _gather_kernel,
        grid_spec=pltpu.PrefetchScalarGridSpec(
            num_scalar_prefetch=2,
            grid=(R // tm,),
            in_specs=[pl.BlockSpec(memory_space=pl.ANY)],
            out_specs=pl.BlockSpec((tm, D), lambda i, tok, nu: (i, 0)),
            scratch_shapes=[pltpu.VMEM((tm, D), src.dtype), pltpu.SemaphoreType.DMA(())]),
        out_shape=jax.ShapeDtypeStruct((R, D), BF16),
        compiler_params=_cparams(1),
    )(tok_of_slot, live, src)


def _ln2_dense_kernel(x_ref, f_ref, g2_ref, lg_ref, lb_ref, o_ref, *, alpha, n_ctx_blocks, row_off):
    is_ctx = (pl.program_id(0) + row_off) < n_ctx_blocks
    y = alpha * x_ref[...] + _pick(g2_ref[...], is_ctx) * f_ref[...]
    o_ref[...] = _layer_norm(y, lg_ref[...], lb_ref[...])


def _ln2_moe_kernel(sa_ref, sb_ref, x_ref, y_ref, gate_ref, g2_ref, lg_ref, lb_ref, o_ref, bufa_ref, bufb_ref,
                    sem, *, alpha, n_ctx_blocks, row_off):
    i = pl.program_id(0)
    R = ROW_BLOCK
    base = (i + row_off) * R

    def issue(r, c):
        _row_copy(y_ref, bufa_ref, sa_ref[base + r], r, sem).start()
        _row_copy(y_ref, bufb_ref, sb_ref[base + r], r, sem).start()
        return c

    def drain(r, c):
        _row_copy(y_ref, bufa_ref, 0, r, sem).wait()
        _row_copy(y_ref, bufb_ref, 0, r, sem).wait()
        return c

    lax.fori_loop(0, R, issue, 0, unroll=DMA_UNROLL // 2)
    lax.fori_loop(0, R, drain, 0, unroll=DMA_UNROLL // 2)
    gate = gate_ref[...]
    f = gate[:, 0:1] * bufa_ref[...] + gate[:, 1:2] * bufb_ref[...]
    is_ctx = (i + row_off) < n_ctx_blocks
    y = alpha * x_ref[...] + _pick(g2_ref[...], is_ctx) * f
    o_ref[...] = _layer_norm(y, lg_ref[...], lb_ref[...])


def _ln2(x1, f, mod, ln_g, ln_b, layer, alpha, n_ctx_blocks, row_off, moe=None):
    S, D = x1.shape
    L = ln_g.shape[0]
    nb = S // ROW_BLOCK - row_off
    n_pre = 0 if moe is None else 2
    wrap = (lambda f_: (lambda i, *_: f_(i)))
    rows_in = pl.BlockSpec((ROW_BLOCK, D), wrap(lambda i: (i + row_off, 0)))
    rows_out = pl.BlockSpec((ROW_BLOCK, D), wrap(lambda i: (i, 0)))
    vspec = pl.BlockSpec((None, 1, D), wrap(lambda i: (layer, 0, 0)))
    mspec = pl.BlockSpec((None, SUBLANES, D), wrap(lambda i: (layer, 0, 5)))
    common = dict(alpha=alpha, n_ctx_blocks=n_ctx_blocks, row_off=row_off)
    lg, lb = ln_g.reshape(L, 1, D), ln_b.reshape(L, 1, D)
    out_shape = jax.ShapeDtypeStruct((nb * ROW_BLOCK, D), F32)
    if moe is None:
        return pl.pallas_call(
            functools.partial(_ln2_dense_kernel, **common),
            grid=(nb,),
            in_specs=[rows_in, rows_in, mspec, vspec, vspec],
            out_specs=rows_out,
            out_shape=out_shape,
            compiler_params=_cparams(1),
        )(x1, f, mod, lg, lb)
    slot_a, slot_b, gates = moe
    return pl.pallas_call(
        functools.partial(_ln2_moe_kernel, **common),
        grid_spec=pltpu.PrefetchScalarGridSpec(
            num_scalar_prefetch=n_pre,
            grid=(nb,),
            in_specs=[rows_in, pl.BlockSpec(memory_space=pl.ANY),
                      pl.BlockSpec((ROW_BLOCK, TOP_K), wrap(lambda i: (i + row_off, 0))),
                      mspec, vspec, vspec],
            out_specs=rows_out,
            scratch_shapes=[pltpu.VMEM((ROW_BLOCK, D), F32), pltpu.VMEM((ROW_BLOCK, D), F32),
                            pltpu.SemaphoreType.DMA(())]),
        out_shape=out_shape,
        compiler_params=_cparams(1),
    )(slot_a, slot_b, x1, f, gates, mod, lg, lb)


def _route_slots(idx, row0, n_experts, tm):
    S = idx.shape[0]
    n = S - row0
    e_flat = idx[row0:].reshape(-1)
    onehot = (e_flat[:, None] == jnp.arange(n_experts, dtype=jnp.int32)[None, :]).astype(jnp.int32)
    pos = jnp.sum((jnp.cumsum(onehot, axis=0) - 1) * onehot, axis=1)
    counts = jnp.sum(onehot, axis=0)
    big = 2 * tm
    padded = ((counts + big - 1) // big) * big
    ends = jnp.cumsum(padded)
    starts = ends - padded
    slot = starts[e_flat] + pos
    n_big = (TOP_K * n) // big + n_experts
    tok = jnp.repeat(jnp.arange(n, dtype=jnp.int32) + row0, TOP_K)
    tok_of_slot = jnp.full((n_big * big,), row0, jnp.int32).at[slot].set(tok)
    t = jnp.arange(2 * n_big, dtype=jnp.int32)
    te = jnp.minimum(jnp.searchsorted(ends, t * tm, side='right'), n_experts - 1).astype(jnp.int32)
    live = (t * tm < starts[te] + counts[te]).astype(jnp.int32)
    src = jnp.maximum(lax.cummax(jnp.where(live > 0, t, -1)), 0)
    down = (tm, te[src], src, live)
    live_big = live[0::2] + live[1::2]
    src_big = src[0::2] // 2
    up = (big, te[0::2][src_big], src_big, live_big)
    slot2 = slot.reshape(n, TOP_K).astype(jnp.int32)
    pad = jnp.zeros((row0,), jnp.int32)
    slot_a = jnp.concatenate([pad, slot2[:, 0]])
    slot_b = jnp.concatenate([pad, slot2[:, 1]])
    return tok_of_slot, up, down, slot_a, slot_b


def kernel(x, c, ctx, c_ctx, w_mod, b_mod, w_in, w_out, ln1_g, ln1_b, ln2_g, ln2_b, pool_w, pool_scale,
           diff_lambda, diff_subln_g, conv_dw, conv_db, conv_ln_g, conv_ln_b, conv_pw, s5_a_re, s5_a_im,
           s5_log_dt, s5_b_re, s5_b_im, s5_c_re, s5_c_im, s5_d, s5_glu_w, s5_glu_b, ffn_w1, ffn_w3, ffn_w2,
           router_w, moe_w1, moe_w3, moe_w2):
    B, T, D = x.shape
    Tc = ctx.shape[1]
    depth = w_mod.shape[0]
    assert B == 1 and Tc % ROW_BLOCK == 0 and T % ROW_BLOCK == 0
    GW = D // N_GROUPS
    n_ctx_blocks = Tc // ROW_BLOCK
    alpha = (2.0 * depth) ** 0.25

    cc = jnp.zeros((SUBLANES, D), F32).at[0].set(c[0]).at[1].set(c_ctx)
    mod = _modulation(cc, w_mod, b_mod)
    cos, sin = _rope_tables(T, Tc, LANES)
    xs = jnp.concatenate([ctx[0], x[0]], axis=0)

    POOL_B, Q_B, K_B, V_B, CONV_B, S5_B = 0, 1, 2, 3, 2, 6

    for l in range(depth):
        last = l == depth - 1
        lam_init = 0.8 - 0.6 * math.exp(-0.3 * l)
        z = _in_projection(xs, mod, w_in, l, n_ctx_blocks)

        pa = _pool_mixer(z, pool_w, pool_scale, l, n_ctx_blocks)

        qt, kk, vt, kn2 = _qkv_prep(z, cos, sin, GW, Q_B, K_B, V_B)
        kn = jnp.sqrt(jnp.max(kn2[n_ctx_blocks:, :, 0], axis=0)) * (1.0 + 2.0 ** -6)
        lv = diff_lambda[l].astype(F32)
        lam = jnp.exp(jnp.sum(lv[0] * lv[1])) - jnp.exp(jnp.sum(lv[2] * lv[3])) + lam_init
        lam_row = jnp.full((1, ATTN_TQ), lam, F32)
        g_col = (diff_subln_g[l].astype(F32) * (1.0 - lam_init)).reshape(GW, 1)
        pb = _diff_attention(qt, kk, vt, lam_row, g_col, kn, Tc)

        pcv = _conv_mixer(z, CONV_B, conv_dw, conv_db, conv_ln_g, conv_ln_b, conv_pw, l, n_ctx_blocks)

        seg_len = xs.shape[0] // S5_SEGS
        pf = _s5_params(s5_a_re[l, 0], s5_a_im[l, 0], s5_log_dt[l, 0], s5_b_re[l, 0], s5_b_im[l, 0],
                        s5_c_re[l, 0], s5_c_im[l, 0], seg_len)
        pr = _s5_params(s5_a_re[l, 1], s5_a_im[l, 1], s5_log_dt[l, 1], s5_b_re[l, 1], s5_b_im[l, 1],
                        s5_c_re[l, 1], s5_c_im[l, 1], seg_len)
        yf, yr = _s5_scan(z, S5_B, pf, pr, Tc)
        pd = _s5_glu(yf, yr, z, S5_B, s5_d, s5_glu_w, s5_glu_b, l, n_ctx_blocks)

        w_out_bf = _cast_bf16(w_out, l)
        row_off = n_ctx_blocks if last else 0
        if l % 2 == 0:
            x1, h = _mix_out((pa, pb, pcv, pd), w_out_bf, xs, mod, ln1_g, ln1_b, l, alpha, n_ctx_blocks, None)
            S = xs.shape[0]
            def dense_tiles(sizes, live):
                tm = next(t for t in sizes if S % t == 0)
                nt = S // tm
                return (tm, jnp.zeros((nt,), jnp.int32), jnp.arange(nt, dtype=jnp.int32),
                        jnp.full((nt,), live, jnp.int32))
            f = _swiglu_tiles(h, ffn_w1[l // 2][None], ffn_w3[l // 2][None], ffn_w2[l // 2][None],
                              dense_tiles((1408, 768, 512, ROW_BLOCK), 2), dense_tiles((768, 512, ROW_BLOCK), 1))
            xs_new = _ln2(x1, f, mod, ln2_g, ln2_b, l, alpha, n_ctx_blocks, row_off)
        else:
            x1, h, idx, gates = _mix_out((pa, pb, pcv, pd), w_out_bf, xs, mod, ln1_g, ln1_b, l, alpha,
                                         n_ctx_blocks, router_w[l // 2])
            n_exp = router_w.shape[-1]
            row0 = row_off * ROW_BLOCK
            tok_of_slot, up, down, slot_a, slot_b = _route_slots(idx, row0, n_exp, FFN_TM)
            hs = _gather_rows(h, tok_of_slot, down[3], FFN_TM)
            y = _swiglu_tiles(hs, moe_w1[l // 2], moe_w3[l // 2], moe_w2[l // 2], up, down)
            xs_new = _ln2(x1, y, mod, ln2_g, ln2_b, l, alpha, n_ctx_blocks, row_off, moe=(slot_a, slot_b, gates))
        xs = xs_new
    return xs[None]
```

```python
import functools
import math

import numpy as np
import jax
import jax.numpy as jnp
from jax import lax
from jax.experimental import pallas as pl
from jax.experimental.pallas import tpu as pltpu

F32 = jnp.float32
BF16 = jnp.bfloat16

GRID_W = 64
N_GROUPS = 4
POOL_WINDOWS = (2, 4, 8, 16)
DIFF_HEADS = 8
DIFF_QK = 32
CONV_WIDTH = 31
S5_P = 16
S5_N = 64
TOP_K = 2
ROPE_BASE = 10000.0
LN_EPS = 1e-5

LANES = 128
SUBLANES = 8
ROW_BLOCK = 256
VMEM_LIMIT = 56 * 1024 * 1024


def _cparams(n_axes, vmem=VMEM_LIMIT):
    return pltpu.CompilerParams(dimension_semantics=("arbitrary",) * n_axes, vmem_limit_bytes=vmem)


def _layer_norm(y, g, b):
    mu = jnp.mean(y, -1, keepdims=True)
    yc = y - mu
    var = jnp.mean(yc * yc, -1, keepdims=True)
    return yc * lax.rsqrt(var + LN_EPS) * g + b


def _silu(x):
    return x * jax.nn.sigmoid(x)


def _mod_kernel(cc_ref, w_ref, b_ref, o_ref):
    a = _silu(cc_ref[...])
    o_ref[...] = jnp.dot(a.astype(BF16), w_ref[...].astype(BF16), preferred_element_type=F32) + b_ref[...]


def _modulation(cc, w_mod, b_mod):
    L, D, N = w_mod.shape
    tn = 1536
    assert N % tn == 0
    return pl.pallas_call(
        _mod_kernel,
        grid=(L, N // tn),
        in_specs=[pl.BlockSpec((SUBLANES, D), lambda l, j: (0, 0)),
                  pl.BlockSpec((None, D, tn), lambda l, j: (l, 0, j)),
                  pl.BlockSpec((None, 1, tn), lambda l, j: (l, 0, j))],
        out_specs=pl.BlockSpec((None, SUBLANES, tn), lambda l, j: (l, 0, j)),
        out_shape=jax.ShapeDtypeStruct((L, SUBLANES, N), F32),
        compiler_params=_cparams(2),
    )(cc, w_mod, b_mod.reshape(L, 1, N))


def _mod_spec(layer, chunk, D, n_grid_axes):
    if n_grid_axes == 1:
        return pl.BlockSpec((None, SUBLANES, D), lambda i: (layer, 0, chunk))
    return pl.BlockSpec((None, SUBLANES, D), lambda j, i: (layer, 0, chunk))


def _pick(m, is_ctx):
    return jnp.where(is_ctx, m[1:2, :], m[0:1, :])


def _inproj_kernel(x_ref, sh_ref, sc_ref, w_ref, o_ref, wb_ref, *, n_ctx_blocks):
    i = pl.program_id(1)

    @pl.when(i == 0)
    def _():
        wb_ref[...] = w_ref[...].astype(BF16)

    is_ctx = i < n_ctx_blocks
    h = x_ref[...] * (1.0 + _pick(sc_ref[...], is_ctx)) + _pick(sh_ref[...], is_ctx)
    o_ref[...] = jnp.dot(h.astype(BF16), wb_ref[...], preferred_element_type=F32).astype(o_ref.dtype)


def _in_projection(xs, mod, w_in, layer, n_ctx_blocks):
    S, D = xs.shape
    N = w_in.shape[-1]
    tn = 1792
    assert N % tn == 0 and S % ROW_BLOCK == 0
    return pl.pallas_call(
        functools.partial(_inproj_kernel, n_ctx_blocks=n_ctx_blocks),
        grid=(N // tn, S // ROW_BLOCK),
        in_specs=[pl.BlockSpec((ROW_BLOCK, D), lambda j, i: (i, 0)),
                  _mod_spec(layer, 0, D, 2),
                  _mod_spec(layer, 1, D, 2),
                  pl.BlockSpec((None, D, tn), lambda j, i: (layer, 0, j))],
        out_specs=pl.BlockSpec((ROW_BLOCK, tn), lambda j, i: (i, j)),
        out_shape=jax.ShapeDtypeStruct((S, N), BF16),
        scratch_shapes=[pltpu.VMEM((D, tn), BF16)],
        compiler_params=_cparams(2),
    )(xs, mod, mod, w_in)


def _seq_edges(i, n_blocks, n_ctx_blocks):
    prev_ok = jnp.logical_and(i != 0, i != n_ctx_blocks)
    next_ok = jnp.logical_and(i != n_ctx_blocks - 1, i != n_blocks - 1)
    return prev_ok, next_ok


def _pool_kernel(p_ref, c_ref, n_ref, w_ref, scale_ref, o_ref, ext_ref, *, n_blocks, n_ctx_blocks):
    i = pl.program_id(0)
    R = ROW_BLOCK
    halo = SUBLANES
    prev_ok, next_ok = _seq_edges(i, n_blocks, n_ctx_blocks)
    cur = c_ref[...].astype(F32)
    pack = 2 * SUBLANES
    ext_ref[0:halo, :] = jnp.where(prev_ok, p_ref[R - pack:R, :].astype(F32)[pack - halo:], 0.0)
    ext_ref[halo:halo + R, :] = cur
    ext_ref[halo + R:halo + R + halo, :] = jnp.where(next_ok, n_ref[0:pack, :].astype(F32)[:halo], 0.0)
    rloc = lax.broadcasted_iota(jnp.int32, (R, 1), 0)
    gw = cur.shape[1] // len(POOL_WINDOWS)
    ext = ext_ref[...]
    n_ext = ext.shape[0]
    accs = [jnp.zeros((R, gw), F32) for _ in POOL_WINDOWS]
    for r in range(SUBLANES):
        rolled = ext if r == 0 else pltpu.roll(ext, n_ext - r, 0)
        for g, w in enumerate(POOL_WINDOWS):
            for d in range(-(w // 2), w - w // 2):
                m, rr = divmod(halo + d, SUBLANES)
                if rr == r:
                    accs[g] = accs[g] + rolled[m * SUBLANES:m * SUBLANES + R, g * gw:(g + 1) * gw]
    outs = []
    for g, w in enumerate(POOL_WINDOWS):
        below = jnp.where(prev_ok, 0, jnp.maximum(w // 2 - rloc, 0))
        above = jnp.where(next_ok, 0, jnp.maximum(rloc + (w - w // 2) - R, 0))
        cnt = (w - below - above).astype(F32)
        diff = accs[g] / cnt - cur[:, g * gw:(g + 1) * gw]
        outs.append(jnp.dot(diff.astype(BF16), w_ref[g].astype(BF16), preferred_element_type=F32))
    o_ref[...] = (jnp.concatenate(outs, axis=-1) * scale_ref[...]).astype(o_ref.dtype)


def _pool_mixer(z, pool_w, pool_scale, layer, n_ctx_blocks):
    S = z.shape[0]
    nb = S // ROW_BLOCK
    GW = pool_scale.shape[-1]
    G, gw = pool_w.shape[1], pool_w.shape[2]
    return pl.pallas_call(
        functools.partial(_pool_kernel, n_blocks=nb, n_ctx_blocks=n_ctx_blocks),
        grid=(nb,),
        in_specs=[pl.BlockSpec((ROW_BLOCK, GW), lambda i: (jnp.maximum(i - 1, 0), 0)),
                  pl.BlockSpec((ROW_BLOCK, GW), lambda i: (i, 0)),
                  pl.BlockSpec((ROW_BLOCK, GW), lambda i: (jnp.minimum(i + 1, nb - 1), 0)),
                  pl.BlockSpec((None, G, gw, gw), lambda i: (layer, 0, 0, 0)),
                  pl.BlockSpec((None, 1, GW), lambda i: (layer, 0, 0))],
        out_specs=pl.BlockSpec((ROW_BLOCK, GW), lambda i: (i, 0)),
        out_shape=jax.ShapeDtypeStruct((S, GW), BF16),
        scratch_shapes=[pltpu.VMEM((ROW_BLOCK + 2 * SUBLANES, GW), F32)],
        compiler_params=_cparams(1),
    )(z, z, z, pool_w, pool_scale.reshape(pool_scale.shape[0], 1, GW))


CONV_HALO = 16


def _conv_kernel(p_ref, c_ref, n_ref, dw_ref, db_ref, g_ref, b_ref, pw_ref, o_ref, ext_ref,
                 *, n_blocks, n_ctx_blocks):
    i = pl.program_id(0)
    R = ROW_BLOCK
    H = CONV_HALO
    GW = o_ref.shape[1]
    prev_ok, next_ok = _seq_edges(i, n_blocks, n_ctx_blocks)

    def glu(u):
        u = u.astype(F32)
        return u[:, :GW] * jax.nn.sigmoid(u[:, GW:])

    ext_ref[0:H, :] = jnp.where(prev_ok, glu(p_ref[R - H:R, :]), 0.0)
    ext_ref[H:H + R, :] = glu(c_ref[...])
    ext_ref[H + R:H + R + H, :] = jnp.where(next_ok, glu(n_ref[0:H, :]), 0.0)
    off = H - CONV_WIDTH // 2
    ext = ext_ref[...]
    n_ext = ext.shape[0]
    acc = jnp.zeros((R, GW), F32)
    for r in range(SUBLANES):
        rolled = ext if r == 0 else pltpu.roll(ext, n_ext - r, 0)
        for j in range(CONV_WIDTH):
            m, rr = divmod(off + j, SUBLANES)
            if rr == r:
                acc = acc + rolled[m * SUBLANES:m * SUBLANES + R, :] * dw_ref[j:j + 1, :]
    y = _layer_norm(acc + db_ref[...], g_ref[...], b_ref[...])
    o_ref[...] = jnp.dot(_silu(y).astype(BF16), pw_ref[...].astype(BF16),
                         preferred_element_type=F32).astype(o_ref.dtype)


def _conv_mixer(z, col_block, conv_dw, conv_db, conv_ln_g, conv_ln_b, conv_pw, layer, n_ctx_blocks):
    S = z.shape[0]
    nb = S // ROW_BLOCK
    GW = conv_db.shape[-1]
    L = conv_db.shape[0]
    vec = lambda a: a.reshape(L, 1, GW)
    vspec = pl.BlockSpec((None, 1, GW), lambda i: (layer, 0, 0))
    return pl.pallas_call(
        functools.partial(_conv_kernel, n_blocks=nb, n_ctx_blocks=n_ctx_blocks),
        grid=(nb,),
        in_specs=[pl.BlockSpec((ROW_BLOCK, 2 * GW), lambda i: (jnp.maximum(i - 1, 0), col_block)),
                  pl.BlockSpec((ROW_BLOCK, 2 * GW), lambda i: (i, col_block)),
                  pl.BlockSpec((ROW_BLOCK, 2 * GW), lambda i: (jnp.minimum(i + 1, nb - 1), col_block)),
                  pl.BlockSpec((None, CONV_WIDTH, GW), lambda i: (layer, 0, 0)),
                  vspec, vspec, vspec,
                  pl.BlockSpec((None, GW, GW), lambda i: (layer, 0, 0))],
        out_specs=pl.BlockSpec((ROW_BLOCK, GW), lambda i: (i, 0)),
        out_shape=jax.ShapeDtypeStruct((S, GW), BF16),
        scratch_shapes=[pltpu.VMEM((ROW_BLOCK + 2 * CONV_HALO, GW), F32)],
        compiler_params=_cparams(1),
    )(z, z, z, conv_dw, vec(conv_db), vec(conv_ln_g), vec(conv_ln_b), conv_pw)


def _rope_tables(T, Tc, width):
    ax = DIFF_QK // 2
    inv = ROPE_BASE ** (-jnp.arange(0, ax, 2, dtype=F32) / ax)
    t = jnp.arange(T)
    row = (t // GRID_W).astype(F32)
    col = (t % GRID_W).astype(F32)
    ang = jnp.stack([row[:, None] * inv, col[:, None] * inv], axis=1)
    cos = jnp.cos(ang)[:, :, None, :]
    sin = jnp.sin(ang)[:, :, None, :]
    cos = jnp.broadcast_to(cos, (T, 2, 2, ax // 2)).reshape(T, DIFF_QK)
    sin = jnp.concatenate([-sin, sin], axis=2).reshape(T, DIFF_QK)
    reps = width // DIFF_QK
    cos = jnp.concatenate([jnp.ones((Tc, DIFF_QK), F32), cos], axis=0)
    sin = jnp.concatenate([jnp.zeros((Tc, DIFF_QK), F32), sin], axis=0)
    return jnp.tile(cos, (1, reps)), jnp.tile(sin, (1, reps))


def _qkv_prep_kernel(q_ref, k_ref, v_ref, cos_ref, sin_ref, qo_ref, ko_ref, vo_ref, kn_ref):
    W = q_ref.shape[1]
    half = DIFF_QK // 4
    lane = lax.broadcasted_iota(jnp.int32, (1, W), 1)
    first = (lane % (2 * half)) < half
    cos = jnp.concatenate([cos_ref[...]] * (W // LANES), axis=1)
    sin = jnp.concatenate([sin_ref[...]] * (W // LANES), axis=1)

    def rope(x):
        partner = jnp.where(first, pltpu.roll(x, W - half, 1), pltpu.roll(x, half, 1))
        return x * cos + partner * sin

    qo_ref[...] = (rope(q_ref[...].astype(F32)) * (DIFF_QK ** -0.5 * math.log2(math.e))).T.astype(BF16)
    kb = rope(k_ref[...].astype(F32)).astype(BF16)
    ko_ref[...] = kb
    ksq = kb.astype(F32).T
    ksq = ksq * ksq
    for grp in range(W // DIFF_QK):
        n2 = jnp.sum(ksq[grp * DIFF_QK:(grp + 1) * DIFF_QK, :], axis=0, keepdims=True)
        kn_ref[grp:grp + 1, :] = jnp.broadcast_to(jnp.max(n2, axis=1, keepdims=True), (1, LANES))
    vt = v_ref[...].astype(F32).T.astype(BF16)
    dv = LANES // 2
    ones = jnp.ones((ATTN_VROWS - dv, vt.shape[1]), BF16)
    for h in range(W // dv):
        vo_ref[h * ATTN_VROWS:h * ATTN_VROWS + dv, :] = vt[h * dv:(h + 1) * dv, :]
        vo_ref[h * ATTN_VROWS + dv:(h + 1) * ATTN_VROWS, :] = ones


def _qkv_prep(z, cos, sin, W, q_blk, k_blk, v_blk):
    S = z.shape[0]
    nb = S // ROW_BLOCK
    row = lambda c: pl.BlockSpec((ROW_BLOCK, W), lambda i: (i, c))
    tab = pl.BlockSpec((ROW_BLOCK, LANES), lambda i: (i, 0))
    return pl.pallas_call(
        _qkv_prep_kernel,
        grid=(nb,),
        in_specs=[row(q_blk), row(k_blk), row(v_blk), tab, tab],
        out_specs=[pl.BlockSpec((W, ROW_BLOCK), lambda i: (0, i)), row(0),
                   pl.BlockSpec((DIFF_HEADS * ATTN_VROWS, ROW_BLOCK), lambda i: (0, i)),
                   pl.BlockSpec((None, W // DIFF_QK, LANES), lambda i: (i, 0, 0))],
        out_shape=[jax.ShapeDtypeStruct((W, S), BF16), jax.ShapeDtypeStruct((S, W), BF16),
                   jax.ShapeDtypeStruct((DIFF_HEADS * ATTN_VROWS, S), BF16),
                   jax.ShapeDtypeStruct((nb, W // DIFF_QK, LANES), F32)],
        compiler_params=_cparams(1),
    )(z, z, z, cos, sin)


ATTN_TQ = 256
ATTN_VROWS = LANES // 2 + 2 * SUBLANES
ATTN_MARGIN = 64.0
ATTN_TK = (4096, 2048, 1024)


def _attn_kernel(qt_ref, k_ref, vt_ref, lam_ref, g_ref, kn_ref, o_ref, qq_ref, m_ref, acc_ref,
                 *, n_ctx, n_ctx_blocks, n_lat_chunks, tk):
    i = pl.program_id(1)
    tq = ATTN_TQ
    dv = LANES // 2
    qt = qt_ref[...]
    feat = lax.broadcasted_iota(jnp.int32, (LANES, 1), 0)
    zero = jnp.zeros_like(qt)
    for hh in range(2):
        for comp in range(2):
            lo = hh * dv + comp * DIFF_QK
            keep = jnp.logical_and(feat >= lo, feat < lo + DIFF_QK)
            qq_ref[hh, :, comp * tq:(comp + 1) * tq] = jnp.where(keep, qt, zero)

    def attend(start, size, mode):
        kk = k_ref[pl.ds(start, size), :]
        for hh in range(2):
            s = jnp.dot(kk, qq_ref[hh], preferred_element_type=F32)
            vv = vt_ref[hh * ATTN_VROWS:(hh + 1) * ATTN_VROWS, pl.ds(start, size)]
            mx = jnp.max(s, axis=0, keepdims=True)
            if mode == "first":
                m_ref[hh] = mx
                p = jnp.exp2(s - mx)
                acc_ref[hh] = jnp.dot(vv, p.astype(BF16), preferred_element_type=F32)
            elif mode == "exact":
                m_old = m_ref[hh]
                m_new = jnp.maximum(m_old, mx)
                m_ref[hh] = m_new
                p = jnp.exp2(s - m_new)
                acc_ref[hh] = (jnp.exp2(m_old - m_new) * acc_ref[hh]
                               + jnp.dot(vv, p.astype(BF16), preferred_element_type=F32))
            else:
                m_old = m_ref[hh]
                p = jnp.exp2(s - m_old)
                m_new = jnp.maximum(m_old, mx)
                m_ref[hh] = m_new
                acc_ref[hh] = (jnp.exp2(m_old - m_new)
                               * (acc_ref[hh] + jnp.dot(vv, p.astype(BF16), preferred_element_type=F32)))

    attend(0, n_ctx, "first")
    n_steps = jnp.where(i < n_ctx_blocks, 0, n_lat_chunks)

    pair = pl.program_id(0)
    col = lax.broadcasted_iota(jnp.int32, (1, 2 * tq), 1)
    excess = jnp.full((1, 2 * tq), -jnp.inf, F32)
    for hh in range(2):
        qf = qq_ref[hh].astype(F32)
        qn = jnp.sqrt(jnp.sum(qf * qf, axis=0, keepdims=True))
        grp = (2 * pair + hh) * 2
        kn = jnp.where(col < tq, kn_ref[grp], kn_ref[grp + 1])
        excess = jnp.maximum(excess, qn * kn - m_ref[hh])
    safe = jnp.max(excess) < ATTN_MARGIN

    def loop(mode):
        def body(c, carry):
            attend(pl.multiple_of(n_ctx + c * tk, LANES), tk, mode)
            return carry
        lax.fori_loop(0, n_steps, body, 0)

    @pl.when(safe)
    def _():
        loop("deferred")

    @pl.when(jnp.logical_not(safe))
    def _():
        loop("exact")

    lam = lam_ref[...]
    outs = []
    for hh in range(2):
        acc = acc_ref[hh]
        ratio = acc[:dv] / acc[dv:dv + 1]
        o = ratio[:, :tq] - lam * ratio[:, tq:]
        r = lax.rsqrt(jnp.sum(o * o, axis=0, keepdims=True) / dv + LN_EPS)
        outs.append(o * r)
    o_ref[...] = (jnp.concatenate(outs, axis=0) * g_ref[...]).T.astype(o_ref.dtype)


def _diff_attention(qt, k, vt, lam, g, kn, n_ctx):
    W, S = qt.shape
    assert S % ATTN_TQ == 0 and n_ctx % ATTN_TQ == 0
    tk = next(t for t in ATTN_TK if (S - n_ctx) % t == 0)
    nq = S // ATTN_TQ
    return pl.pallas_call(
        functools.partial(_attn_kernel, n_ctx=n_ctx, n_ctx_blocks=n_ctx // ATTN_TQ,
                          n_lat_chunks=(S - n_ctx) // tk, tk=tk),
        grid=(W // LANES, nq),
        in_specs=[pl.BlockSpec((LANES, ATTN_TQ), lambda p, i: (p, i)),
                  pl.BlockSpec((S, LANES), lambda p, i: (0, p)),
                  pl.BlockSpec((2 * ATTN_VROWS, S), lambda p, i: (p, 0)),
                  pl.BlockSpec((1, ATTN_TQ), lambda p, i: (0, 0)),
                  pl.BlockSpec((LANES, 1), lambda p, i: (p, 0)),
                  pl.BlockSpec(memory_space=pltpu.SMEM)],
        out_specs=pl.BlockSpec((ATTN_TQ, LANES), lambda p, i: (i, p)),
        out_shape=jax.ShapeDtypeStruct((S, W), BF16),
        scratch_shapes=[pltpu.VMEM((2, LANES, 2 * ATTN_TQ), BF16),
                        pltpu.VMEM((2, 1, 2 * ATTN_TQ), F32),
                        pltpu.VMEM((2, ATTN_VROWS, 2 * ATTN_TQ), F32)],
        compiler_params=_cparams(2),
    )(qt, k, vt, lam, g, kn)


S5_SEGS = SUBLANES
S5_KB = 32
S5_GB = 8


def _s5_params(a_re, a_im, log_dt, b_re, b_im, c_re, c_im, seg_len):
    G, N = a_re.shape
    P = b_re.shape[-1]
    nblk = G // S5_GB
    a_re, a_im = a_re.astype(F32), a_im.astype(F32)
    dt = jnp.exp(log_dt.astype(F32))[:, None]
    lr, li = dt * a_re, dt * a_im
    mag = jnp.exp(lr)
    ar, ai = mag * jnp.cos(li), mag * jnp.sin(li)
    den = a_re * a_re + a_im * a_im
    qr = ((ar - 1.0) * a_re + ai * a_im) / den
    qi = (ai * a_re - (ar - 1.0) * a_im) / den
    b_re, b_im = b_re.astype(F32), b_im.astype(F32)
    br = qr[..., None] * b_re - qi[..., None] * b_im
    bi = qr[..., None] * b_im + qi[..., None] * b_re
    mag_l = jnp.exp(seg_len * lr)
    alr, ali = mag_l * jnp.cos(seg_len * li), mag_l * jnp.sin(seg_len * li)
    eye = jnp.eye(S5_GB, dtype=F32)
    wb = lambda m: jnp.einsum('gh,bgnp->bgphn', eye, m.reshape(nblk, S5_GB, N, P)).reshape(
        nblk, S5_GB * P, S5_GB * N)
    w_in = jnp.concatenate([wb(br), wb(bi)], axis=2)
    cm = lambda m: jnp.einsum('gh,bgpn->bhngp', eye, m.astype(F32).reshape(nblk, S5_GB, P, N)).reshape(
        nblk, S5_GB * N, S5_GB * P)
    w_out = jnp.concatenate([cm(c_re), -cm(c_im)], axis=1)
    row = lambda r, i: jnp.concatenate([r.reshape(1, G * N), i.reshape(1, G * N)], axis=1)
    coef = jnp.broadcast_to(row(ar, ai), (S5_SEGS, 2 * G * N))
    return w_in.astype(BF16), coef, row(alr, ali), w_out.astype(BF16)


def _s5_kernel(*refs, emit_out):
    n_seg = S5_SEGS
    uf_ref = refs[0]
    ur_refs = refs[1:1 + n_seg]
    rest = refs[1 + n_seg:]
    if emit_out:
        (wbf_ref, wbr_ref, af_ref, ar_ref, ef_ref, er_ref, alf_ref, alr_ref, cf_ref, cr_ref,
         yf_ref, yr_ref, stage_ref, bf_ref, br_ref, hf_ref, hr_ref) = rest
    else:
        (wbf_ref, wbr_ref, af_ref, ar_ref, ef_out_ref, er_out_ref,
         stage_ref, bf_ref, br_ref, hf_ref, hr_ref) = rest
    g = pl.program_id(0)
    KB = S5_KB
    R = n_seg * KB
    NS = af_ref.shape[1] // 2
    nblk = wbf_ref.shape[0]
    wi = wbf_ref.shape[1]
    ws = wbf_ref.shape[2] // 2

    def cmul_add(a_row, h, add):
        are, aim = a_row[:, :NS], a_row[:, NS:]
        hre, him = h[:, :NS], h[:, NS:]
        return jnp.concatenate([are * hre - aim * him + add[:, :NS], are * him + aim * hre + add[:, NS:]], axis=1)

    @pl.when(g == 0)
    def _():
        if emit_out:
            def chain(e_ref, al_ref, order):
                al = al_ref[...]
                c = jnp.zeros((1, 2 * NS), F32)
                rows = [None] * n_seg
                for s in order:
                    rows[s] = c
                    c = cmul_add(al, c, e_ref[s:s + 1, :])
                return jnp.concatenate(rows, axis=0)
            hf_ref[...] = chain(ef_ref, alf_ref, range(n_seg))
            hr_ref[...] = chain(er_ref, alr_ref, range(n_seg - 1, -1, -1))
        else:
            hf_ref[...] = jnp.zeros_like(hf_ref)
            hr_ref[...] = jnp.zeros_like(hr_ref)

    def interleaved(load_seg):
        for s in range(n_seg):
            blk = load_seg(s).astype(F32)
            for c in range(nblk):
                stage_ref[c, s * KB:(s + 1) * KB, :] = blk[:, c * wi:(c + 1) * wi]
        rows = [jnp.concatenate([stage_ref[c, pl.ds(kk, n_seg, stride=KB), :] for c in range(nblk)], axis=1)
                for kk in range(KB)]
        return jnp.concatenate(rows, axis=0).astype(BF16)

    def project_in(u, w_ref, buf_ref):
        for b in range(nblk):
            res = jnp.dot(u[:, b * wi:(b + 1) * wi], w_ref[b], preferred_element_type=F32)
            buf_ref[:, b * ws:(b + 1) * ws] = res[:, :ws]
            buf_ref[:, NS + b * ws:NS + (b + 1) * ws] = res[:, ws:]

    def scan(buf_ref, a_ref, h_ref, reverse):
        half = NS // 2
        for c in range(2):
            cre = slice(c * half, (c + 1) * half)
            cim = slice(NS + c * half, NS + (c + 1) * half)
            are, aim = a_ref[:, cre], a_ref[:, cim]

            def step(t, carry, cre=cre, cim=cim, are=are, aim=aim):
                hre, him = carry
                kk = (KB - 1 - t) if reverse else t
                r0 = pl.multiple_of(kk * n_seg, n_seg)
                nre = are * hre - aim * him + buf_ref[pl.ds(r0, n_seg), cre]
                nim = are * him + aim * hre + buf_ref[pl.ds(r0, n_seg), cim]
                if emit_out:
                    buf_ref[pl.ds(r0, n_seg), cre] = nre
                    buf_ref[pl.ds(r0, n_seg), cim] = nim
                return nre, nim

            hre, him = lax.fori_loop(0, KB, step, (h_ref[:, cre], h_ref[:, cim]))
            h_ref[:, cre] = hre
            h_ref[:, cim] = him

    def project_out(buf_ref, c_ref, y_ref):
        for b in range(nblk):
            hcat = jnp.concatenate([buf_ref[:, b * ws:(b + 1) * ws], buf_ref[:, NS + b * ws:NS + (b + 1) * ws]],
                                   axis=1).astype(BF16)
            stage_ref[b] = jnp.dot(hcat, c_ref[b], preferred_element_type=F32)
        for s in range(n_seg):
            y_ref[s] = jnp.concatenate([stage_ref[c, pl.ds(s, KB, stride=n_seg), :] for c in range(nblk)], axis=1)

    project_in(interleaved(lambda s: uf_ref[s]), wbf_ref, bf_ref)
    scan(bf_ref, af_ref, hf_ref, False)
    if emit_out:
        project_out(bf_ref, cf_ref, yf_ref)
    project_in(interleaved(lambda s: ur_refs[s][...]), wbr_ref, br_ref)
    scan(br_ref, ar_ref, hr_ref, True)
    if emit_out:
        project_out(br_ref, cr_ref, yr_ref)
    else:
        ef_out_ref[...] = hf_ref[...]
        er_out_ref[...] = hr_ref[...]


def _s5_pass(z, col_block, pf, pr, ends, n_ctx):
    S, NZ = z.shape
    GW = pf[0].shape[1] * pf[0].shape[0]
    NS2 = pf[1].shape[1]
    seg_len = S // S5_SEGS
    steps = seg_len // S5_KB
    nblocks = S // S5_KB
    assert S % (S5_SEGS * S5_KB) == 0 and n_ctx % S5_KB == 0
    ctx_blocks = n_ctx // S5_KB
    emit_out = ends is not None
    z4 = z.reshape(S5_SEGS, steps, S5_KB, NZ)
    z3 = z.reshape(nblocks, S5_KB, NZ)

    def rev_spec(s):
        return pl.BlockSpec((None, S5_KB, GW),
                            lambda g: ((s * steps + steps - 1 - g + ctx_blocks) % nblocks, 0, col_block))

    const = lambda a: pl.BlockSpec(a.shape, lambda g: (0,) * a.ndim)
    in_specs = [pl.BlockSpec((S5_SEGS, None, S5_KB, GW), lambda g: (0, g, 0, col_block))]
    in_specs += [rev_spec(s) for s in range(S5_SEGS)]
    args = [z4] + [z3] * S5_SEGS
    weights = [pf[0], pr[0], pf[1], pr[1]]
    if emit_out:
        weights += [ends[0], ends[1], pf[2], pr[2], pf[3], pr[3]]
    in_specs += [const(a) for a in weights]
    args += weights
    scratch = [pltpu.VMEM((pf[0].shape[0], S5_SEGS * S5_KB, pf[0].shape[1]), F32),
               pltpu.VMEM((S5_SEGS * S5_KB, NS2), F32), pltpu.VMEM((S5_SEGS * S5_KB, NS2), F32),
               pltpu.VMEM((S5_SEGS, NS2), F32), pltpu.VMEM((S5_SEGS, NS2), F32)]
    if emit_out:
        yshape = jax.ShapeDtypeStruct((S5_SEGS, steps, S5_KB, GW), F32)
        out_shape = [yshape, yshape]
        out_specs = [pl.BlockSpec((S5_SEGS, None, S5_KB, GW), lambda g: (0, g, 0, 0)),
                     pl.BlockSpec((S5_SEGS, None, S5_KB, GW), lambda g: (0, steps - 1 - g, 0, 0))]
    else:
        eshape = jax.ShapeDtypeStruct((S5_SEGS, NS2), F32)
        out_shape = [eshape, eshape]
        out_specs = [pl.BlockSpec((S5_SEGS, NS2), lambda g: (0, 0))] * 2
    return pl.pallas_call(
        functools.partial(_s5_kernel, emit_out=emit_out),
        grid=(steps,),
        in_specs=in_specs,
        out_specs=out_specs,
        out_shape=out_shape,
        scratch_shapes=scratch,
        compiler_params=_cparams(1),
    )(*args)


def _s5_scan(z, col_block, pf, pr, n_ctx):
    S = z.shape[0]
    GW = pf[0].shape[1] * pf[0].shape[0]
    ends = _s5_pass(z, col_block, pf, pr, None, n_ctx)
    yf, yr = _s5_pass(z, col_block, pf, pr, ends, n_ctx)
    return yf.reshape(S, GW), yr.reshape(S, GW)


def _cast_kernel(x_ref, o_ref):
    o_ref[...] = x_ref[...].astype(o_ref.dtype)


def _cast_bf16(w, layer):
    _, K, N = w.shape
    tk = 512
    return pl.pallas_call(
        _cast_kernel,
        grid=(K // tk,),
        in_specs=[pl.BlockSpec((None, tk, N), lambda i: (layer, i, 0))],
        out_specs=pl.BlockSpec((tk, N), lambda i: (i, 0)),
        out_shape=jax.ShapeDtypeStruct((K, N), BF16),
        compiler_params=_cparams(1),
    )(w)


def _mixout_kernel(*refs, alpha, n_ctx_blocks, route, h_dtype):
    (pa_ref, pb_ref, pc_ref, yf_ref, yr_ref, u_ref, d_ref, gw_ref, gb_ref,
     w_ref, x_ref, g1_ref, lg_ref, lb_ref, sh_ref, sc_ref) = refs[:16]
    if route:
        rw_ref, x1_ref, h_ref, idx_ref, gate_ref = refs[16:]
    else:
        x1_ref, h_ref = refs[16:]
    i = pl.program_id(0)
    is_ctx = i < n_ctx_blocks
    sy = yf_ref[...] + yr_ref[...] + d_ref[...] * u_ref[...].astype(F32)
    zz = jax.nn.gelu(sy)
    sgate = jnp.dot(zz.astype(BF16), gw_ref[...].astype(BF16), preferred_element_type=F32) + gb_ref[...]
    pd = (zz * jax.nn.sigmoid(sgate)).astype(BF16)
    GW = pa_ref.shape[1]
    mix = jnp.zeros(x_ref.shape, F32)
    for k, part in enumerate((pa_ref[...], pb_ref[...], pc_ref[...], pd)):
        mix = mix + jnp.dot(part.astype(BF16), w_ref[k * GW:(k + 1) * GW, :], preferred_element_type=F32)
    y = alpha * x_ref[...] + _pick(g1_ref[...], is_ctx) * mix
    x1 = _layer_norm(y, lg_ref[...], lb_ref[...])
    x1_ref[...] = x1
    h = x1 * (1.0 + _pick(sc_ref[...], is_ctx)) + _pick(sh_ref[...], is_ctx)
    h_ref[...] = h.astype(h_dtype)
    if route:
        rw = rw_ref[...]
        h_hi = h.astype(BF16)
        h_lo = (h - h_hi.astype(F32)).astype(BF16)
        w_hi = rw.astype(BF16)
        w_lo = (rw - w_hi.astype(F32)).astype(BF16)
        logits = (jnp.dot(h_hi, w_hi, preferred_element_type=F32)
                  + (jnp.dot(h_lo, w_hi, preferred_element_type=F32)
                     + jnp.dot(h_hi, w_lo, preferred_element_type=F32)))
        n_exp = rw_ref.shape[1]
        lane = lax.broadcasted_iota(jnp.int32, logits.shape, 1)
        m1 = jnp.max(logits, axis=-1, keepdims=True)
        i1 = jnp.min(jnp.where(logits == m1, lane, n_exp), axis=-1, keepdims=True)
        rest = jnp.where(lane == i1, -jnp.inf, logits)
        m2 = jnp.max(rest, axis=-1, keepdims=True)
        i2 = jnp.min(jnp.where(rest == m2, lane, n_exp), axis=-1, keepdims=True)
        e2 = jnp.exp(m2 - m1)
        idx_ref[...] = jnp.concatenate([i1, i2], axis=1)
        gate_ref[...] = jnp.concatenate([1.0 / (1.0 + e2), e2 / (1.0 + e2)], axis=1)


def _mix_out(parts, s5, w_out_bf, xs, mod, ln_g, ln_b, layer, alpha, n_ctx_blocks, router_w):
    S, D = xs.shape
    GW = parts[0].shape[1]
    L = ln_g.shape[0]
    nb = S // ROW_BLOCK
    route = router_w is not None
    h_dtype = F32 if route else BF16
    yf, yr, z, col_block, s5_d, glu_w, glu_b = s5
    part = pl.BlockSpec((ROW_BLOCK, GW), lambda i: (i, 0))
    rot = pl.BlockSpec((ROW_BLOCK, GW), lambda i: ((i + nb - n_ctx_blocks) % nb, 0))
    gvec = pl.BlockSpec((None, 1, GW), lambda i: (layer, 0, 0))
    rows = pl.BlockSpec((ROW_BLOCK, D), lambda i: (i, 0))
    vspec = pl.BlockSpec((None, 1, D), lambda i: (layer, 0, 0))
    in_specs = [part, part, part,
                part, rot, pl.BlockSpec((ROW_BLOCK, GW), lambda i: (i, col_block)), gvec,
                pl.BlockSpec((None, GW, GW), lambda i: (layer, 0, 0)), gvec,
                pl.BlockSpec((D, D), lambda i: (0, 0)), rows,
                _mod_spec(layer, 2, D, 1), vspec, vspec, _mod_spec(layer, 3, D, 1), _mod_spec(layer, 4, D, 1)]
    args = list(parts) + [yf, yr, z, s5_d.reshape(L, 1, GW), glu_w, glu_b.reshape(L, 1, GW),
                          w_out_bf, xs, mod, ln_g.reshape(L, 1, D), ln_b.reshape(L, 1, D), mod, mod]
    out_specs = [rows, rows]
    out_shape = [jax.ShapeDtypeStruct((S, D), F32), jax.ShapeDtypeStruct((S, D), h_dtype)]
    if route:
        E = router_w.shape[-1]
        in_specs.append(pl.BlockSpec((D, E), lambda i: (0, 0)))
        args.append(router_w)
        out_specs += [pl.BlockSpec((ROW_BLOCK, TOP_K), lambda i: (i, 0))] * 2
        out_shape += [jax.ShapeDtypeStruct((S, TOP_K), jnp.int32), jax.ShapeDtypeStruct((S, TOP_K), F32)]
    return pl.pallas_call(
        functools.partial(_mixout_kernel, alpha=alpha, n_ctx_blocks=n_ctx_blocks, route=route, h_dtype=h_dtype),
        grid=(S // ROW_BLOCK,),
        in_specs=in_specs,
        out_specs=out_specs,
        out_shape=out_shape,
        compiler_params=_cparams(1),
    )(*args)


FFN_TM = 512
FFN_TF = 512
FFN_TN = 512


def _expert_changed(te_ref, i):
    prev = te_ref[jnp.maximum(i - 1, 0)]
    return jnp.logical_or(i == 0, te_ref[i] != prev)


def _ffn_up_kernel(te_ref, src_ref, live_ref, h_ref, w1_ref, w3_ref, o_ref, w1b_ref, w3b_ref):
    i = pl.program_id(1)
    half = h_ref.shape[0] // 2

    @pl.when(_expert_changed(te_ref, i))
    def _():
        w1b_ref[...] = w1_ref[...].astype(BF16)
        w3b_ref[...] = w3_ref[...].astype(BF16)

    def act(h):
        a = jnp.dot(h, w1b_ref[...], preferred_element_type=F32)
        b = jnp.dot(h, w3b_ref[...], preferred_element_type=F32)
        return (_silu(a) * b).astype(o_ref.dtype)

    @pl.when(live_ref[i] == 2)
    def _():
        o_ref[...] = act(h_ref[...])

    @pl.when(live_ref[i] == 1)
    def _():
        o_ref[0:half, :] = act(h_ref[0:half, :])
        o_ref[half:, :] = jnp.zeros((half, o_ref.shape[1]), o_ref.dtype)

    @pl.when(live_ref[i] == 0)
    def _():
        o_ref[...] = jnp.zeros_like(o_ref)


def _ffn_down_kernel(te_ref, src_ref, live_ref, g_ref, w2_ref, o_ref, w2b_ref):
    i = pl.program_id(1)

    @pl.when(_expert_changed(te_ref, i))
    def _():
        w2b_ref[...] = w2_ref[...].astype(BF16)

    @pl.when(live_ref[i] > 0)
    def _():
        o_ref[...] = jnp.dot(g_ref[...], w2b_ref[...], preferred_element_type=F32)

    @pl.when(live_ref[i] == 0)
    def _():
        o_ref[...] = jnp.zeros_like(o_ref)


def _swiglu_tiles(hs, w1, w3, w2, up, down):
    R, D = hs.shape
    _, _, F = w1.shape
    tf = FFN_TF if F % FFN_TF == 0 else F
    tn = FFN_TN
    tm_u, te_u, src_u, live_u = up
    tm_d, te_d, src_d, live_d = down
    assert R % tm_u == 0 and R % tm_d == 0 and F % tf == 0 and D % tn == 0
    g = pl.pallas_call(
        _ffn_up_kernel,
        grid_spec=pltpu.PrefetchScalarGridSpec(
            num_scalar_prefetch=3,
            grid=(F // tf, R // tm_u),
            in_specs=[pl.BlockSpec((tm_u, D), lambda j, i, te, src, lv: (src[i], 0)),
                      pl.BlockSpec((None, D, tf), lambda j, i, te, src, lv: (te[i], 0, j)),
                      pl.BlockSpec((None, D, tf), lambda j, i, te, src, lv: (te[i], 0, j))],
            out_specs=pl.BlockSpec((tm_u, tf), lambda j, i, te, src, lv: (i, j)),
            scratch_shapes=[pltpu.VMEM((D, tf), BF16), pltpu.VMEM((D, tf), BF16)]),
        out_shape=jax.ShapeDtypeStruct((R, F), BF16),
        compiler_params=_cparams(2),
    )(te_u, src_u, live_u, hs, w1, w3)
    return pl.pallas_call(
        _ffn_down_kernel,
        grid_spec=pltpu.PrefetchScalarGridSpec(
            num_scalar_prefetch=3,
            grid=(D // tn, R // tm_d),
            in_specs=[pl.BlockSpec((tm_d, F), lambda j, i, te, src, lv: (src[i], 0)),
                      pl.BlockSpec((None, F, tn), lambda j, i, te, src, lv: (te[i], 0, j))],
            out_specs=pl.BlockSpec((tm_d, tn), lambda j, i, te, src, lv: (i, j)),
            scratch_shapes=[pltpu.VMEM((F, tn), BF16)]),
        out_shape=jax.ShapeDtypeStruct((R, D), F32),
        compiler_params=_cparams(2),
    )(te_d, src_d, live_d, g, w2)


def _row_copy(src_ref, dst_ref, src_row, dst_row, sem):
    return pltpu.make_async_copy(src_ref.at[pl.ds(src_row, 1)], dst_ref.at[pl.ds(dst_row, 1)], sem)


DMA_UNROLL = 8


def _gather_kernel(tok_ref, live_ref, src_ref, o_ref, buf_ref, sem):
    i = pl.program_id(0)
    tm = buf_ref.shape[0]
    live = live_ref[i] > 0

    def issue(r, c):
        _row_copy(src_ref, buf_ref, tok_ref[i * tm + r], r, sem).start()
        return c

    def drain(r, c):
        _row_copy(src_ref, buf_ref, 0, r, sem).wait()
        return c

    @pl.when(live)
    def _():
        lax.fori_loop(0, tm, issue, 0, unroll=DMA_UNROLL)
        lax.fori_loop(0, tm, drain, 0, unroll=DMA_UNROLL)
        o_ref[...] = buf_ref[...].astype(o_ref.dtype)

    @pl.when(jnp.logical_not(live))
    def _():
        o_ref[...] = jnp.zeros_like(o_ref)


def _gather_rows(src, tok_of_slot, live, tm):
    R = tok_of_slot.shape[0]
    D = src.shape[1]
    return pl.pallas_call(
        _gather_kernel,
        grid_spec=pltpu.PrefetchScalarGridSpec(
            num_scalar_prefetch=2,
            grid=(R // tm,),
            in_specs=[pl.BlockSpec(memory_space=pl.ANY)],
            out_specs=pl.BlockSpec((tm, D), lambda i, tok, nu: (i, 0)),
            scratch_shapes=[pltpu.VMEM((tm, D), src.dtype), pltpu.SemaphoreType.DMA(())]),
        out_shape=jax.ShapeDtypeStruct((R, D), BF16),
        compiler_params=_cparams(1),
    )(tok_of_slot, live, src)


def _ln2_dense_kernel(x_ref, f_ref, g2_ref, lg_ref, lb_ref, o_ref, *, alpha, n_ctx_blocks, row_off):
    is_ctx = (pl.program_id(0) + row_off) < n_ctx_blocks
    y = alpha * x_ref[...] + _pick(g2_ref[...], is_ctx) * f_ref[...]
    o_ref[...] = _layer_norm(y, lg_ref[...], lb_ref[...])


def _ln2_moe_kernel(sa_ref, sb_ref, x_ref, y_ref, gate_ref, g2_ref, lg_ref, lb_ref, o_ref, bufa_ref, bufb_ref,
                    sem, *, alpha, n_ctx_blocks, row_off):
    i = pl.program_id(0)
    R = ROW_BLOCK
    base = (i + row_off) * R

    def issue(r, c):
        _row_copy(y_ref, bufa_ref, sa_ref[base + r], r, sem).start()
        _row_copy(y_ref, bufb_ref, sb_ref[base + r], r, sem).start()
        return c

    def drain(r, c):
        _row_copy(y_ref, bufa_ref, 0, r, sem).wait()
        _row_copy(y_ref, bufb_ref, 0, r, sem).wait()
        return c

    lax.fori_loop(0, R, issue, 0, unroll=DMA_UNROLL // 2)
    lax.fori_loop(0, R, drain, 0, unroll=DMA_UNROLL // 2)
    gate = gate_ref[...]
    f = gate[:, 0:1] * bufa_ref[...] + gate[:, 1:2] * bufb_ref[...]
    is_ctx = (i + row_off) < n_ctx_blocks
    y = alpha * x_ref[...] + _pick(g2_ref[...], is_ctx) * f
    o_ref[...] = _layer_norm(y, lg_ref[...], lb_ref[...])


def _ln2(x1, f, mod, ln_g, ln_b, layer, alpha, n_ctx_blocks, row_off, moe=None):
    S, D = x1.shape
    L = ln_g.shape[0]
    nb = S // ROW_BLOCK - row_off
    n_pre = 0 if moe is None else 2
    wrap = (lambda f_: (lambda i, *_: f_(i)))
    rows_in = pl.BlockSpec((ROW_BLOCK, D), wrap(lambda i: (i + row_off, 0)))
    rows_out = pl.BlockSpec((ROW_BLOCK, D), wrap(lambda i: (i, 0)))
    vspec = pl.BlockSpec((None, 1, D), wrap(lambda i: (layer, 0, 0)))
    mspec = pl.BlockSpec((None, SUBLANES, D), wrap(lambda i: (layer, 0, 5)))
    common = dict(alpha=alpha, n_ctx_blocks=n_ctx_blocks, row_off=row_off)
    lg, lb = ln_g.reshape(L, 1, D), ln_b.reshape(L, 1, D)
    out_shape = jax.ShapeDtypeStruct((nb * ROW_BLOCK, D), F32)
    if moe is None:
        return pl.pallas_call(
            functools.partial(_ln2_dense_kernel, **common),
            grid=(nb,),
            in_specs=[rows_in, rows_in, mspec, vspec, vspec],
            out_specs=rows_out,
            out_shape=out_shape,
            compiler_params=_cparams(1),
        )(x1, f, mod, lg, lb)
    slot_a, slot_b, gates = moe
    return pl.pallas_call(
        functools.partial(_ln2_moe_kernel, **common),
        grid_spec=pltpu.PrefetchScalarGridSpec(
            num_scalar_prefetch=n_pre,
            grid=(nb,),
            in_specs=[rows_in, pl.BlockSpec(memory_space=pl.ANY),
                      pl.BlockSpec((ROW_BLOCK, TOP_K), wrap(lambda i: (i + row_off, 0))),
                      mspec, vspec, vspec],
            out_specs=rows_out,
            scratch_shapes=[pltpu.VMEM((ROW_BLOCK, D), F32), pltpu.VMEM((ROW_BLOCK, D), F32),
                            pltpu.SemaphoreType.DMA(())]),
        out_shape=out_shape,
        compiler_params=_cparams(1),
    )(slot_a, slot_b, x1, f, gates, mod, lg, lb)


def _route_slots(idx, row0, n_experts, tm):
    S = idx.shape[0]
    n = S - row0
    e_flat = idx[row0:].reshape(-1)
    onehot = (e_flat[:, None] == jnp.arange(n_experts, dtype=jnp.int32)[None, :]).astype(jnp.int32)
    pos = jnp.sum((jnp.cumsum(onehot, axis=0) - 1) * onehot, axis=1)
    counts = jnp.sum(onehot, axis=0)
    big = 2 * tm
    padded = ((counts + big - 1) // big) * big
    ends = jnp.cumsum(padded)
    starts = ends - padded
    slot = starts[e_flat] + pos
    n_big = (TOP_K * n) // big + n_experts
    tok = jnp.repeat(jnp.arange(n, dtype=jnp.int32) + row0, TOP_K)
    tok_of_slot = jnp.full((n_big * big,), row0, jnp.int32).at[slot].set(tok)
    t = jnp.arange(2 * n_big, dtype=jnp.int32)
    te = jnp.minimum(jnp.searchsorted(ends, t * tm, side='right'), n_experts - 1).astype(jnp.int32)
    live = (t * tm < starts[te] + counts[te]).astype(jnp.int32)
    src = jnp.maximum(lax.cummax(jnp.where(live > 0, t, -1)), 0)
    down = (tm, te[src], src, live)
    live_big = live[0::2] + live[1::2]
    src_big = src[0::2] // 2
    up = (big, te[0::2][src_big], src_big, live_big)
    slot2 = slot.reshape(n, TOP_K).astype(jnp.int32)
    pad = jnp.zeros((row0,), jnp.int32)
    slot_a = jnp.concatenate([pad, slot2[:, 0]])
    slot_b = jnp.concatenate([pad, slot2[:, 1]])
    return tok_of_slot, up, down, slot_a, slot_b


def kernel(x, c, ctx, c_ctx, w_mod, b_mod, w_in, w_out, ln1_g, ln1_b, ln2_g, ln2_b, pool_w, pool_scale,
           diff_lambda, diff_subln_g, conv_dw, conv_db, conv_ln_g, conv_ln_b, conv_pw, s5_a_re, s5_a_im,
           s5_log_dt, s5_b_re, s5_b_im, s5_c_re, s5_c_im, s5_d, s5_glu_w, s5_glu_b, ffn_w1, ffn_w3, ffn_w2,
           router_w, moe_w1, moe_w3, moe_w2):
    B, T, D = x.shape
    Tc = ctx.shape[1]
    depth = w_mod.shape[0]
    assert B == 1 and Tc % ROW_BLOCK == 0 and T % ROW_BLOCK == 0
    GW = D // N_GROUPS
    n_ctx_blocks = Tc // ROW_BLOCK
    alpha = (2.0 * depth) ** 0.25

    cc = jnp.zeros((SUBLANES, D), F32).at[0].set(c[0]).at[1].set(c_ctx)
    mod = _modulation(cc, w_mod, b_mod)
    cos, sin = _rope_tables(T, Tc, LANES)
    xs = jnp.concatenate([ctx[0], x[0]], axis=0)

    POOL_B, Q_B, K_B, V_B, CONV_B, S5_B = 0, 1, 2, 3, 2, 6

    for l in range(depth):
        last = l == depth - 1
        lam_init = 0.8 - 0.6 * math.exp(-0.3 * l)
        z = _in_projection(xs, mod, w_in, l, n_ctx_blocks)

        pa = _pool_mixer(z, pool_w, pool_scale, l, n_ctx_blocks)

        qt, kk, vt, kn2 = _qkv_prep(z, cos, sin, GW, Q_B, K_B, V_B)
        kn = jnp.sqrt(jnp.max(kn2[n_ctx_blocks:, :, 0], axis=0)) * (1.0 + 2.0 ** -6)
        lv = diff_lambda[l].astype(F32)
        lam = jnp.exp(jnp.sum(lv[0] * lv[1])) - jnp.exp(jnp.sum(lv[2] * lv[3])) + lam_init
        lam_row = jnp.full((1, ATTN_TQ), lam, F32)
        g_col = (diff_subln_g[l].astype(F32) * (1.0 - lam_init)).reshape(GW, 1)
        pb = _diff_attention(qt, kk, vt, lam_row, g_col, kn, Tc)

        pcv = _conv_mixer(z, CONV_B, conv_dw, conv_db, conv_ln_g, conv_ln_b, conv_pw, l, n_ctx_blocks)

        seg_len = xs.shape[0] // S5_SEGS
        pf = _s5_params(s5_a_re[l, 0], s5_a_im[l, 0], s5_log_dt[l, 0], s5_b_re[l, 0], s5_b_im[l, 0],
                        s5_c_re[l, 0], s5_c_im[l, 0], seg_len)
        pr = _s5_params(s5_a_re[l, 1], s5_a_im[l, 1], s5_log_dt[l, 1], s5_b_re[l, 1], s5_b_im[l, 1],
                        s5_c_re[l, 1], s5_c_im[l, 1], seg_len)
        yf, yr = _s5_scan(z, S5_B, pf, pr, Tc)
        s5 = (yf, yr, z, S5_B, s5_d, s5_glu_w, s5_glu_b)

        w_out_bf = _cast_bf16(w_out, l)
        row_off = n_ctx_blocks if last else 0
        if l % 2 == 0:
            x1, h = _mix_out((pa, pb, pcv), s5, w_out_bf, xs, mod, ln1_g, ln1_b, l, alpha, n_ctx_blocks, None)
            S = xs.shape[0]
            def dense_tiles(sizes, live):
                tm = next(t for t in sizes if S % t == 0)
                nt = S // tm
                return (tm, jnp.zeros((nt,), jnp.int32), jnp.arange(nt, dtype=jnp.int32),
                        jnp.full((nt,), live, jnp.int32))
            f = _swiglu_tiles(h, ffn_w1[l // 2][None], ffn_w3[l // 2][None], ffn_w2[l // 2][None],
                              dense_tiles((1408, 768, 512, ROW_BLOCK), 2), dense_tiles((768, 512, ROW_BLOCK), 1))
            xs_new = _ln2(x1, f, mod, ln2_g, ln2_b, l, alpha, n_ctx_blocks, row_off)
        else:
            x1, h, idx, gates = _mix_out((pa, pb, pcv), s5, w_out_bf, xs, mod, ln1_g, ln1_b, l, alpha,
                                         n_ctx_blocks, router_w[l // 2])
            n_exp = router_w.shape[-1]
            row0 = row_off * ROW_BLOCK
            tok_of_slot, up, down, slot_a, slot_b = _route_slots(idx, row0, n_exp, FFN_TM)
            hs = _gather_rows(h, tok_of_slot, down[3], FFN_TM)
            y = _swiglu_tiles(hs, moe_w1[l // 2], moe_w3[l // 2], moe_w2[l // 2], up, down)
            xs_new = _ln2(x1, y, mod, ln2_g, ln2_b, l, alpha, n_ctx_blocks, row_off, moe=(slot_a, slot_b, gates))
        xs = xs_new
    return xs[None]
```

```python
import functools
import math

import numpy as np
import jax
import jax.numpy as jnp
from jax import lax
from jax.experimental import pallas as pl
from jax.experimental.pallas import tpu as pltpu

F32 = jnp.float32
BF16 = jnp.bfloat16

GRID_W = 64
N_GROUPS = 4
POOL_WINDOWS = (2, 4, 8, 16)
DIFF_HEADS = 8
DIFF_QK = 32
CONV_WIDTH = 31
S5_P = 16
S5_N = 64
TOP_K = 2
ROPE_BASE = 10000.0
LN_EPS = 1e-5

LANES = 128
SUBLANES = 8
ROW_BLOCK = 256
VMEM_LIMIT = 56 * 1024 * 1024


def _cparams(n_axes, vmem=VMEM_LIMIT):
    return pltpu.CompilerParams(dimension_semantics=("arbitrary",) * n_axes, vmem_limit_bytes=vmem)


def _layer_norm(y, g, b):
    mu = jnp.mean(y, -1, keepdims=True)
    yc = y - mu
    var = jnp.mean(yc * yc, -1, keepdims=True)
    return yc * lax.rsqrt(var + LN_EPS) * g + b


def _silu(x):
    return x * jax.nn.sigmoid(x)


def _mod_kernel(cc_ref, w_ref, b_ref, o_ref):
    a = _silu(cc_ref[...])
    o_ref[...] = jnp.dot(a.astype(BF16), w_ref[...].astype(BF16), preferred_element_type=F32) + b_ref[...]


def _modulation(cc, w_mod, b_mod):
    L, D, N = w_mod.shape
    tn = 1536
    assert N % tn == 0
    return pl.pallas_call(
        _mod_kernel,
        grid=(L, N // tn),
        in_specs=[pl.BlockSpec((SUBLANES, D), lambda l, j: (0, 0)),
                  pl.BlockSpec((None, D, tn), lambda l, j: (l, 0, j)),
                  pl.BlockSpec((None, 1, tn), lambda l, j: (l, 0, j))],
        out_specs=pl.BlockSpec((None, SUBLANES, tn), lambda l, j: (l, 0, j)),
        out_shape=jax.ShapeDtypeStruct((L, SUBLANES, N), F32),
        compiler_params=_cparams(2),
    )(cc, w_mod, b_mod.reshape(L, 1, N))


def _mod_spec(layer, chunk, D, n_grid_axes):
    if n_grid_axes == 1:
        return pl.BlockSpec((None, SUBLANES, D), lambda i: (layer, 0, chunk))
    return pl.BlockSpec((None, SUBLANES, D), lambda j, i: (layer, 0, chunk))


def _pick(m, is_ctx):
    return jnp.where(is_ctx, m[1:2, :], m[0:1, :])


class _Stream:
    def __init__(self, latent, context, n_ctx_blocks):
        self.latent = latent
        self.context = context
        self.n_ctx_blocks = n_ctx_blocks
        self.off = n_ctx_blocks if latent is context else 0
        self.rows = (latent.shape[0] // ROW_BLOCK - self.off + n_ctx_blocks) * ROW_BLOCK
        self.width = latent.shape[1]

    def specs(self, n_grid_axes):
        ncb, off, D = self.n_ctx_blocks, self.off, self.width
        lat = lambda i: (jnp.maximum(i - ncb, 0) + off, 0)
        ctx = lambda i: (jnp.minimum(i, ncb - 1), 0)
        if n_grid_axes == 1:
            return [pl.BlockSpec((ROW_BLOCK, D), lat), pl.BlockSpec((ROW_BLOCK, D), ctx)]
        return [pl.BlockSpec((ROW_BLOCK, D), lambda j, i: lat(i)), pl.BlockSpec((ROW_BLOCK, D), lambda j, i: ctx(i))]

    def args(self):
        return [self.latent, self.context]


def _inproj_kernel(x_ref, c_ref, sh_ref, sc_ref, w_ref, o_ref, wb_ref, *, n_ctx_blocks):
    i = pl.program_id(1)

    @pl.when(i == 0)
    def _():
        wb_ref[...] = w_ref[...].astype(BF16)

    is_ctx = i < n_ctx_blocks
    x = jnp.where(is_ctx, c_ref[...], x_ref[...])
    h = x * (1.0 + _pick(sc_ref[...], is_ctx)) + _pick(sh_ref[...], is_ctx)
    o_ref[...] = jnp.dot(h.astype(BF16), wb_ref[...], preferred_element_type=F32).astype(o_ref.dtype)


def _in_projection(stream, mod, w_in, layer, n_ctx_blocks):
    S, D = stream.rows, stream.width
    N = w_in.shape[-1]
    tn = 1792
    assert N % tn == 0 and S % ROW_BLOCK == 0
    return pl.pallas_call(
        functools.partial(_inproj_kernel, n_ctx_blocks=n_ctx_blocks),
        grid=(N // tn, S // ROW_BLOCK),
        in_specs=stream.specs(2) + [
                  _mod_spec(layer, 0, D, 2),
                  _mod_spec(layer, 1, D, 2),
                  pl.BlockSpec((None, D, tn), lambda j, i: (layer, 0, j))],
        out_specs=pl.BlockSpec((ROW_BLOCK, tn), lambda j, i: (i, j)),
        out_shape=jax.ShapeDtypeStruct((S, N), BF16),
        scratch_shapes=[pltpu.VMEM((D, tn), BF16)],
        compiler_params=_cparams(2),
    )(*stream.args(), mod, mod, w_in)


def _seq_edges(i, n_blocks, n_ctx_blocks):
    prev_ok = jnp.logical_and(i != 0, i != n_ctx_blocks)
    next_ok = jnp.logical_and(i != n_ctx_blocks - 1, i != n_blocks - 1)
    return prev_ok, next_ok


def _pool_kernel(p_ref, c_ref, n_ref, w_ref, scale_ref, o_ref, ext_ref, *, n_blocks, n_ctx_blocks):
    i = pl.program_id(0)
    R = ROW_BLOCK
    halo = SUBLANES
    prev_ok, next_ok = _seq_edges(i, n_blocks, n_ctx_blocks)
    cur = c_ref[...].astype(F32)
    pack = 2 * SUBLANES
    ext_ref[0:halo, :] = jnp.where(prev_ok, p_ref[R - pack:R, :].astype(F32)[pack - halo:], 0.0)
    ext_ref[halo:halo + R, :] = cur
    ext_ref[halo + R:halo + R + halo, :] = jnp.where(next_ok, n_ref[0:pack, :].astype(F32)[:halo], 0.0)
    rloc = lax.broadcasted_iota(jnp.int32, (R, 1), 0)
    gw = cur.shape[1] // len(POOL_WINDOWS)
    ext = ext_ref[...]
    n_ext = ext.shape[0]
    accs = [jnp.zeros((R, gw), F32) for _ in POOL_WINDOWS]
    for r in range(SUBLANES):
        rolled = ext if r == 0 else pltpu.roll(ext, n_ext - r, 0)
        for g, w in enumerate(POOL_WINDOWS):
            for d in range(-(w // 2), w - w // 2):
                m, rr = divmod(halo + d, SUBLANES)
                if rr == r:
                    accs[g] = accs[g] + rolled[m * SUBLANES:m * SUBLANES + R, g * gw:(g + 1) * gw]
    outs = []
    for g, w in enumerate(POOL_WINDOWS):
        below = jnp.where(prev_ok, 0, jnp.maximum(w // 2 - rloc, 0))
        above = jnp.where(next_ok, 0, jnp.maximum(rloc + (w - w // 2) - R, 0))
        cnt = (w - below - above).astype(F32)
        diff = accs[g] / cnt - cur[:, g * gw:(g + 1) * gw]
        outs.append(jnp.dot(diff.astype(BF16), w_ref[g].astype(BF16), preferred_element_type=F32))
    o_ref[...] = (jnp.concatenate(outs, axis=-1) * scale_ref[...]).astype(o_ref.dtype)


def _pool_mixer(z, pool_w, pool_scale, layer, n_ctx_blocks):
    S = z.shape[0]
    nb = S // ROW_BLOCK
    GW = pool_scale.shape[-1]
    G, gw = pool_w.shape[1], pool_w.shape[2]
    return pl.pallas_call(
        functools.partial(_pool_kernel, n_blocks=nb, n_ctx_blocks=n_ctx_blocks),
        grid=(nb,),
        in_specs=[pl.BlockSpec((ROW_BLOCK, GW), lambda i: (jnp.maximum(i - 1, 0), 0)),
                  pl.BlockSpec((ROW_BLOCK, GW), lambda i: (i, 0)),
                  pl.BlockSpec((ROW_BLOCK, GW), lambda i: (jnp.minimum(i + 1, nb - 1), 0)),
                  pl.BlockSpec((None, G, gw, gw), lambda i: (layer, 0, 0, 0)),
                  pl.BlockSpec((None, 1, GW), lambda i: (layer, 0, 0))],
        out_specs=pl.BlockSpec((ROW_BLOCK, GW), lambda i: (i, 0)),
        out_shape=jax.ShapeDtypeStruct((S, GW), BF16),
        scratch_shapes=[pltpu.VMEM((ROW_BLOCK + 2 * SUBLANES, GW), F32)],
        compiler_params=_cparams(1),
    )(z, z, z, pool_w, pool_scale.reshape(pool_scale.shape[0], 1, GW))


CONV_HALO = 16


def _conv_kernel(p_ref, c_ref, n_ref, dw_ref, db_ref, g_ref, b_ref, pw_ref, o_ref, ext_ref,
                 *, n_blocks, n_ctx_blocks):
    i = pl.program_id(0)
    R = ROW_BLOCK
    H = CONV_HALO
    GW = o_ref.shape[1]
    prev_ok, next_ok = _seq_edges(i, n_blocks, n_ctx_blocks)

    def glu(u):
        u = u.astype(F32)
        return u[:, :GW] * jax.nn.sigmoid(u[:, GW:])

    ext_ref[0:H, :] = jnp.where(prev_ok, glu(p_ref[R - H:R, :]), 0.0)
    ext_ref[H:H + R, :] = glu(c_ref[...])
    ext_ref[H + R:H + R + H, :] = jnp.where(next_ok, glu(n_ref[0:H, :]), 0.0)
    off = H - CONV_WIDTH // 2
    ext = ext_ref[...]
    n_ext = ext.shape[0]
    acc = jnp.zeros((R, GW), F32)
    for r in range(SUBLANES):
        rolled = ext if r == 0 else pltpu.roll(ext, n_ext - r, 0)
        for j in range(CONV_WIDTH):
            m, rr = divmod(off + j, SUBLANES)
            if rr == r:
                acc = acc + rolled[m * SUBLANES:m * SUBLANES + R, :] * dw_ref[j:j + 1, :]
    y = _layer_norm(acc + db_ref[...], g_ref[...], b_ref[...])
    o_ref[...] = jnp.dot(_silu(y).astype(BF16), pw_ref[...].astype(BF16),
                         preferred_element_type=F32).astype(o_ref.dtype)


def _conv_mixer(z, col_block, conv_dw, conv_db, conv_ln_g, conv_ln_b, conv_pw, layer, n_ctx_blocks):
    S = z.shape[0]
    nb = S // ROW_BLOCK
    GW = conv_db.shape[-1]
    L = conv_db.shape[0]
    vec = lambda a: a.reshape(L, 1, GW)
    vspec = pl.BlockSpec((None, 1, GW), lambda i: (layer, 0, 0))
    return pl.pallas_call(
        functools.partial(_conv_kernel, n_blocks=nb, n_ctx_blocks=n_ctx_blocks),
        grid=(nb,),
        in_specs=[pl.BlockSpec((ROW_BLOCK, 2 * GW), lambda i: (jnp.maximum(i - 1, 0), col_block)),
                  pl.BlockSpec((ROW_BLOCK, 2 * GW), lambda i: (i, col_block)),
                  pl.BlockSpec((ROW_BLOCK, 2 * GW), lambda i: (jnp.minimum(i + 1, nb - 1), col_block)),
                  pl.BlockSpec((None, CONV_WIDTH, GW), lambda i: (layer, 0, 0)),
                  vspec, vspec, vspec,
                  pl.BlockSpec((None, GW, GW), lambda i: (layer, 0, 0))],
        out_specs=pl.BlockSpec((ROW_BLOCK, GW), lambda i: (i, 0)),
        out_shape=jax.ShapeDtypeStruct((S, GW), BF16),
        scratch_shapes=[pltpu.VMEM((ROW_BLOCK + 2 * CONV_HALO, GW), F32)],
        compiler_params=_cparams(1),
    )(z, z, z, conv_dw, vec(conv_db), vec(conv_ln_g), vec(conv_ln_b), conv_pw)


def _rope_tables(T, Tc, width):
    ax = DIFF_QK // 2
    inv = ROPE_BASE ** (-jnp.arange(0, ax, 2, dtype=F32) / ax)
    t = jnp.arange(T)
    row = (t // GRID_W).astype(F32)
    col = (t % GRID_W).astype(F32)
    ang = jnp.stack([row[:, None] * inv, col[:, None] * inv], axis=1)
    cos = jnp.cos(ang)[:, :, None, :]
    sin = jnp.sin(ang)[:, :, None, :]
    cos = jnp.broadcast_to(cos, (T, 2, 2, ax // 2)).reshape(T, DIFF_QK)
    sin = jnp.concatenate([-sin, sin], axis=2).reshape(T, DIFF_QK)
    reps = width // DIFF_QK
    cos = jnp.concatenate([jnp.ones((Tc, DIFF_QK), F32), cos], axis=0)
    sin = jnp.concatenate([jnp.zeros((Tc, DIFF_QK), F32), sin], axis=0)
    return jnp.tile(cos, (1, reps)), jnp.tile(sin, (1, reps))


def _qkv_prep_kernel(q_ref, k_ref, v_ref, cos_ref, sin_ref, qo_ref, ko_ref, vo_ref, kn_ref):
    W = q_ref.shape[1]
    half = DIFF_QK // 4
    lane = lax.broadcasted_iota(jnp.int32, (1, W), 1)
    first = (lane % (2 * half)) < half
    cos = jnp.concatenate([cos_ref[...]] * (W // LANES), axis=1)
    sin = jnp.concatenate([sin_ref[...]] * (W // LANES), axis=1)

    def rope(x):
        partner = jnp.where(first, pltpu.roll(x, W - half, 1), pltpu.roll(x, half, 1))
        return x * cos + partner * sin

    qo_ref[...] = (rope(q_ref[...].astype(F32)) * (DIFF_QK ** -0.5 * math.log2(math.e))).T.astype(BF16)
    kb = rope(k_ref[...].astype(F32)).astype(BF16)
    ko_ref[...] = kb
    ksq = kb.astype(F32).T
    ksq = ksq * ksq
    for grp in range(W // DIFF_QK):
        n2 = jnp.sum(ksq[grp * DIFF_QK:(grp + 1) * DIFF_QK, :], axis=0, keepdims=True)
        kn_ref[grp:grp + 1, :] = jnp.broadcast_to(jnp.max(n2, axis=1, keepdims=True), (1, LANES))
    vt = v_ref[...].astype(F32).T.astype(BF16)
    dv = LANES // 2
    ones = jnp.ones((ATTN_VROWS - dv, vt.shape[1]), BF16)
    for h in range(W // dv):
        vo_ref[h * ATTN_VROWS:h * ATTN_VROWS + dv, :] = vt[h * dv:(h + 1) * dv, :]
        vo_ref[h * ATTN_VROWS + dv:(h + 1) * ATTN_VROWS, :] = ones


def _qkv_prep(z, cos, sin, W, q_blk, k_blk, v_blk):
    S = z.shape[0]
    nb = S // ROW_BLOCK
    row = lambda c: pl.BlockSpec((ROW_BLOCK, W), lambda i: (i, c))
    tab = pl.BlockSpec((ROW_BLOCK, LANES), lambda i: (i, 0))
    return pl.pallas_call(
        _qkv_prep_kernel,
        grid=(nb,),
        in_specs=[row(q_blk), row(k_blk), row(v_blk), tab, tab],
        out_specs=[pl.BlockSpec((W, ROW_BLOCK), lambda i: (0, i)), row(0),
                   pl.BlockSpec((DIFF_HEADS * ATTN_VROWS, ROW_BLOCK), lambda i: (0, i)),
                   pl.BlockSpec((None, W // DIFF_QK, LANES), lambda i: (i, 0, 0))],
        out_shape=[jax.ShapeDtypeStruct((W, S), BF16), jax.ShapeDtypeStruct((S, W), BF16),
                   jax.ShapeDtypeStruct((DIFF_HEADS * ATTN_VROWS, S), BF16),
                   jax.ShapeDtypeStruct((nb, W // DIFF_QK, LANES), F32)],
        compiler_params=_cparams(1),
    )(z, z, z, cos, sin)


ATTN_TQ = 256
ATTN_VROWS = LANES // 2 + 2 * SUBLANES
ATTN_MARGIN = 64.0
ATTN_TK = (4096, 2048, 1024)


def _attn_kernel(qt_ref, k_ref, vt_ref, lam_ref, g_ref, kn_ref, o_ref, qq_ref, m_ref, acc_ref,
                 *, n_ctx, n_ctx_blocks, n_lat_chunks, tk):
    i = pl.program_id(1)
    tq = ATTN_TQ
    dv = LANES // 2
    qt = qt_ref[...]
    feat = lax.broadcasted_iota(jnp.int32, (LANES, 1), 0)
    zero = jnp.zeros_like(qt)
    for hh in range(2):
        for comp in range(2):
            lo = hh * dv + comp * DIFF_QK
            keep = jnp.logical_and(feat >= lo, feat < lo + DIFF_QK)
            qq_ref[hh, :, comp * tq:(comp + 1) * tq] = jnp.where(keep, qt, zero)

    def attend(start, size, mode):
        kk = k_ref[pl.ds(start, size), :]
        for hh in range(2):
            s = jnp.dot(kk, qq_ref[hh], preferred_element_type=F32)
            vv = vt_ref[hh * ATTN_VROWS:(hh + 1) * ATTN_VROWS, pl.ds(start, size)]
            mx = jnp.max(s, axis=0, keepdims=True)
            if mode == "first":
                m_ref[hh] = mx
                p = jnp.exp2(s - mx)
                acc_ref[hh] = jnp.dot(vv, p.astype(BF16), preferred_element_type=F32)
            elif mode == "exact":
                m_old = m_ref[hh]
                m_new = jnp.maximum(m_old, mx)
                m_ref[hh] = m_new
                p = jnp.exp2(s - m_new)
                acc_ref[hh] = (jnp.exp2(m_old - m_new) * acc_ref[hh]
                               + jnp.dot(vv, p.astype(BF16), preferred_element_type=F32))
            else:
                m_old = m_ref[hh]
                p = jnp.exp2(s - m_old)
                m_new = jnp.maximum(m_old, mx)
                m_ref[hh] = m_new
                acc_ref[hh] = (jnp.exp2(m_old - m_new)
                               * (acc_ref[hh] + jnp.dot(vv, p.astype(BF16), preferred_element_type=F32)))

    attend(0, n_ctx, "first")
    n_steps = jnp.where(i < n_ctx_blocks, 0, n_lat_chunks)

    pair = pl.program_id(0)
    col = lax.broadcasted_iota(jnp.int32, (1, 2 * tq), 1)
    excess = jnp.full((1, 2 * tq), -jnp.inf, F32)
    for hh in range(2):
        qf = qq_ref[hh].astype(F32)
        qn = jnp.sqrt(jnp.sum(qf * qf, axis=0, keepdims=True))
        grp = (2 * pair + hh) * 2
        kn = jnp.where(col < tq, kn_ref[grp], kn_ref[grp + 1])
        excess = jnp.maximum(excess, qn * kn - m_ref[hh])
    safe = jnp.max(excess) < ATTN_MARGIN

    def loop(mode):
        def body(c, carry):
            attend(pl.multiple_of(n_ctx + c * tk, LANES), tk, mode)
            return carry
        lax.fori_loop(0, n_steps, body, 0)

    @pl.when(safe)
    def _():
        loop("deferred")

    @pl.when(jnp.logical_not(safe))
    def _():
        loop("exact")

    lam = lam_ref[...]
    outs = []
    for hh in range(2):
        acc = acc_ref[hh]
        ratio = acc[:dv] / acc[dv:dv + 1]
        o = ratio[:, :tq] - lam * ratio[:, tq:]
        r = lax.rsqrt(jnp.sum(o * o, axis=0, keepdims=True) / dv + LN_EPS)
        outs.append(o * r)
    o_ref[...] = (jnp.concatenate(outs, axis=0) * g_ref[...]).T.astype(o_ref.dtype)


def _diff_attention(qt, k, vt, lam, g, kn, n_ctx):
    W, S = qt.shape
    assert S % ATTN_TQ == 0 and n_ctx % ATTN_TQ == 0
    tk = next(t for t in ATTN_TK if (S - n_ctx) % t == 0)
    nq = S // ATTN_TQ
    return pl.pallas_call(
        functools.partial(_attn_kernel, n_ctx=n_ctx, n_ctx_blocks=n_ctx // ATTN_TQ,
                          n_lat_chunks=(S - n_ctx) // tk, tk=tk),
        grid=(W // LANES, nq),
        in_specs=[pl.BlockSpec((LANES, ATTN_TQ), lambda p, i: (p, i)),
                  pl.BlockSpec((S, LANES), lambda p, i: (0, p)),
                  pl.BlockSpec((2 * ATTN_VROWS, S), lambda p, i: (p, 0)),
                  pl.BlockSpec((1, ATTN_TQ), lambda p, i: (0, 0)),
                  pl.BlockSpec((LANES, 1), lambda p, i: (p, 0)),
                  pl.BlockSpec(memory_space=pltpu.SMEM)],
        out_specs=pl.BlockSpec((ATTN_TQ, LANES), lambda p, i: (i, p)),
        out_shape=jax.ShapeDtypeStruct((S, W), BF16),
        scratch_shapes=[pltpu.VMEM((2, LANES, 2 * ATTN_TQ), BF16),
                        pltpu.VMEM((2, 1, 2 * ATTN_TQ), F32),
                        pltpu.VMEM((2, ATTN_VROWS, 2 * ATTN_TQ), F32)],
        compiler_params=_cparams(2),
    )(qt, k, vt, lam, g, kn)


S5_SEGS = SUBLANES
S5_KB = 32
S5_GB = 8


def _s5_params(a_re, a_im, log_dt, b_re, b_im, c_re, c_im, seg_len):
    G, N = a_re.shape
    P = b_re.shape[-1]
    nblk = G // S5_GB
    a_re, a_im = a_re.astype(F32), a_im.astype(F32)
    dt = jnp.exp(log_dt.astype(F32))[:, None]
    lr, li = dt * a_re, dt * a_im
    mag = jnp.exp(lr)
    ar, ai = mag * jnp.cos(li), mag * jnp.sin(li)
    den = a_re * a_re + a_im * a_im
    qr = ((ar - 1.0) * a_re + ai * a_im) / den
    qi = (ai * a_re - (ar - 1.0) * a_im) / den
    b_re, b_im = b_re.astype(F32), b_im.astype(F32)
    br = qr[..., None] * b_re - qi[..., None] * b_im
    bi = qr[..., None] * b_im + qi[..., None] * b_re
    mag_l = jnp.exp(seg_len * lr)
    alr, ali = mag_l * jnp.cos(seg_len * li), mag_l * jnp.sin(seg_len * li)
    eye = jnp.eye(S5_GB, dtype=F32)
    wb = lambda m: jnp.einsum('gh,bgnp->bgphn', eye, m.reshape(nblk, S5_GB, N, P)).reshape(
        nblk, S5_GB * P, S5_GB * N)
    w_in = jnp.concatenate([wb(br), wb(bi)], axis=2)
    cm = lambda m: jnp.einsum('gh,bgpn->bhngp', eye, m.astype(F32).reshape(nblk, S5_GB, P, N)).reshape(
        nblk, S5_GB * N, S5_GB * P)
    w_out = jnp.concatenate([cm(c_re), -cm(c_im)], axis=1)
    row = lambda r, i: jnp.concatenate([r.reshape(1, G * N), i.reshape(1, G * N)], axis=1)
    coef = jnp.broadcast_to(row(ar, ai), (S5_SEGS, 2 * G * N))
    return w_in.astype(BF16), coef, row(alr, ali), w_out.astype(BF16)


def _s5_kernel(*refs, emit_out):
    n_seg = S5_SEGS
    uf_ref = refs[0]
    ur_refs = refs[1:1 + n_seg]
    rest = refs[1 + n_seg:]
    if emit_out:
        (wbf_ref, wbr_ref, af_ref, ar_ref, ef_ref, er_ref, alf_ref, alr_ref, cf_ref, cr_ref,
         yf_ref, yr_ref, stage_ref, bf_ref, br_ref, hf_ref, hr_ref) = rest
    else:
        (wbf_ref, wbr_ref, af_ref, ar_ref, ef_out_ref, er_out_ref,
         stage_ref, bf_ref, br_ref, hf_ref, hr_ref) = rest
    g = pl.program_id(0)
    KB = S5_KB
    R = n_seg * KB
    NS = af_ref.shape[1] // 2
    nblk = wbf_ref.shape[0]
    wi = wbf_ref.shape[1]
    ws = wbf_ref.shape[2] // 2

    def cmul_add(a_row, h, add):
        are, aim = a_row[:, :NS], a_row[:, NS:]
        hre, him = h[:, :NS], h[:, NS:]
        return jnp.concatenate([are * hre - aim * him + add[:, :NS], are * him + aim * hre + add[:, NS:]], axis=1)

    @pl.when(g == 0)
    def _():
        if emit_out:
            def chain(e_ref, al_ref, order):
                al = al_ref[...]
                c = jnp.zeros((1, 2 * NS), F32)
                rows = [None] * n_seg
                for s in order:
                    rows[s] = c
                    c = cmul_add(al, c, e_ref[s:s + 1, :])
                return jnp.concatenate(rows, axis=0)
            hf_ref[...] = chain(ef_ref, alf_ref, range(n_seg))
            hr_ref[...] = chain(er_ref, alr_ref, range(n_seg - 1, -1, -1))
        else:
            hf_ref[...] = jnp.zeros_like(hf_ref)
            hr_ref[...] = jnp.zeros_like(hr_ref)

    def interleaved(load_seg):
        for s in range(n_seg):
            blk = load_seg(s).astype(F32)
            for c in range(nblk):
                stage_ref[c, s * KB:(s + 1) * KB, :] = blk[:, c * wi:(c + 1) * wi]
        rows = [jnp.concatenate([stage_ref[c, pl.ds(kk, n_seg, stride=KB), :] for c in range(nblk)], axis=1)
                for kk in range(KB)]
        return jnp.concatenate(rows, axis=0).astype(BF16)

    def project_in(u, w_ref, buf_ref):
        for b in range(nblk):
            res = jnp.dot(u[:, b * wi:(b + 1) * wi], w_ref[b], preferred_element_type=F32)
            buf_ref[:, b * ws:(b + 1) * ws] = res[:, :ws]
            buf_ref[:, NS + b * ws:NS + (b + 1) * ws] = res[:, ws:]

    def scan(buf_ref, a_ref, h_ref, reverse):
        half = NS // 2
        for c in range(2):
            cre = slice(c * half, (c + 1) * half)
            cim = slice(NS + c * half, NS + (c + 1) * half)
            are, aim = a_ref[:, cre], a_ref[:, cim]

            def step(t, carry, cre=cre, cim=cim, are=are, aim=aim):
                hre, him = carry
                kk = (KB - 1 - t) if reverse else t
                r0 = pl.multiple_of(kk * n_seg, n_seg)
                nre = are * hre - aim * him + buf_ref[pl.ds(r0, n_seg), cre]
                nim = are * him + aim * hre + buf_ref[pl.ds(r0, n_seg), cim]
                if emit_out:
                    buf_ref[pl.ds(r0, n_seg), cre] = nre
                    buf_ref[pl.ds(r0, n_seg), cim] = nim
                return nre, nim

            hre, him = lax.fori_loop(0, KB, step, (h_ref[:, cre], h_ref[:, cim]))
            h_ref[:, cre] = hre
            h_ref[:, cim] = him

    def project_out(buf_ref, c_ref, y_ref):
        for b in range(nblk):
            hcat = jnp.concatenate([buf_ref[:, b * ws:(b + 1) * ws], buf_ref[:, NS + b * ws:NS + (b + 1) * ws]],
                                   axis=1).astype(BF16)
            stage_ref[b] = jnp.dot(hcat, c_ref[b], preferred_element_type=F32)
        for s in range(n_seg):
            y_ref[s] = jnp.concatenate([stage_ref[c, pl.ds(s, KB, stride=n_seg), :] for c in range(nblk)], axis=1)

    project_in(interleaved(lambda s: uf_ref[s]), wbf_ref, bf_ref)
    scan(bf_ref, af_ref, hf_ref, False)
    if emit_out:
        project_out(bf_ref, cf_ref, yf_ref)
    project_in(interleaved(lambda s: ur_refs[s][...]), wbr_ref, br_ref)
    scan(br_ref, ar_ref, hr_ref, True)
    if emit_out:
        project_out(br_ref, cr_ref, yr_ref)
    else:
        ef_out_ref[...] = hf_ref[...]
        er_out_ref[...] = hr_ref[...]


def _s5_pass(z, col_block, pf, pr, ends, n_ctx):
    S, NZ = z.shape
    GW = pf[0].shape[1] * pf[0].shape[0]
    NS2 = pf[1].shape[1]
    seg_len = S // S5_SEGS
    steps = seg_len // S5_KB
    nblocks = S // S5_KB
    assert S % (S5_SEGS * S5_KB) == 0 and n_ctx % S5_KB == 0
    ctx_blocks = n_ctx // S5_KB
    emit_out = ends is not None
    z4 = z.reshape(S5_SEGS, steps, S5_KB, NZ)
    z3 = z.reshape(nblocks, S5_KB, NZ)

    def rev_spec(s):
        return pl.BlockSpec((None, S5_KB, GW),
                            lambda g: ((s * steps + steps - 1 - g + ctx_blocks) % nblocks, 0, col_block))

    const = lambda a: pl.BlockSpec(a.shape, lambda g: (0,) * a.ndim)
    in_specs = [pl.BlockSpec((S5_SEGS, None, S5_KB, GW), lambda g: (0, g, 0, col_block))]
    in_specs += [rev_spec(s) for s in range(S5_SEGS)]
    args = [z4] + [z3] * S5_SEGS
    weights = [pf[0], pr[0], pf[1], pr[1]]
    if emit_out:
        weights += [ends[0], ends[1], pf[2], pr[2], pf[3], pr[3]]
    in_specs += [const(a) for a in weights]
    args += weights
    scratch = [pltpu.VMEM((pf[0].shape[0], S5_SEGS * S5_KB, pf[0].shape[1]), F32),
               pltpu.VMEM((S5_SEGS * S5_KB, NS2), F32), pltpu.VMEM((S5_SEGS * S5_KB, NS2), F32),
               pltpu.VMEM((S5_SEGS, NS2), F32), pltpu.VMEM((S5_SEGS, NS2), F32)]
    if emit_out:
        yshape = jax.ShapeDtypeStruct((S5_SEGS, steps, S5_KB, GW), F32)
        out_shape = [yshape, yshape]
        out_specs = [pl.BlockSpec((S5_SEGS, None, S5_KB, GW), lambda g: (0, g, 0, 0)),
                     pl.BlockSpec((S5_SEGS, None, S5_KB, GW), lambda g: (0, steps - 1 - g, 0, 0))]
    else:
        eshape = jax.ShapeDtypeStruct((S5_SEGS, NS2), F32)
        out_shape = [eshape, eshape]
        out_specs = [pl.BlockSpec((S5_SEGS, NS2), lambda g: (0, 0))] * 2
    return pl.pallas_call(
        functools.partial(_s5_kernel, emit_out=emit_out),
        grid=(steps,),
        in_specs=in_specs,
        out_specs=out_specs,
        out_shape=out_shape,
        scratch_shapes=scratch,
        compiler_params=_cparams(1),
    )(*args)


def _s5_scan(z, col_block, pf, pr, n_ctx):
    S = z.shape[0]
    GW = pf[0].shape[1] * pf[0].shape[0]
    ends = _s5_pass(z, col_block, pf, pr, None, n_ctx)
    yf, yr = _s5_pass(z, col_block, pf, pr, ends, n_ctx)
    return yf.reshape(S, GW), yr.reshape(S, GW)


def _cast_kernel(x_ref, o_ref):
    o_ref[...] = x_ref[...].astype(o_ref.dtype)


def _cast_bf16(w, layer):
    _, K, N = w.shape
    tk = 512
    return pl.pallas_call(
        _cast_kernel,
        grid=(K // tk,),
        in_specs=[pl.BlockSpec((None, tk, N), lambda i: (layer, i, 0))],
        out_specs=pl.BlockSpec((tk, N), lambda i: (i, 0)),
        out_shape=jax.ShapeDtypeStruct((K, N), BF16),
        compiler_params=_cparams(1),
    )(w)


def _mixout_kernel(*refs, alpha, n_ctx_blocks, route, h_dtype):
    (pa_ref, pb_ref, pc_ref, yf_ref, yr_ref, u_ref, d_ref, gw_ref, gb_ref,
     w_ref, x_ref, c_ref, g1_ref, lg_ref, lb_ref, sh_ref, sc_ref) = refs[:17]
    if route:
        rw_ref, x1_ref, h_ref, idx_ref, gate_ref = refs[17:]
    else:
        x1_ref, h_ref = refs[17:]
    i = pl.program_id(0)
    is_ctx = i < n_ctx_blocks
    sy = yf_ref[...] + yr_ref[...] + d_ref[...] * u_ref[...].astype(F32)
    zz = jax.nn.gelu(sy)
    sgate = jnp.dot(zz.astype(BF16), gw_ref[...].astype(BF16), preferred_element_type=F32) + gb_ref[...]
    pd = (zz * jax.nn.sigmoid(sgate)).astype(BF16)
    GW = pa_ref.shape[1]
    mix = jnp.zeros(x_ref.shape, F32)
    for k, part in enumerate((pa_ref[...], pb_ref[...], pc_ref[...], pd)):
        mix = mix + jnp.dot(part.astype(BF16), w_ref[k * GW:(k + 1) * GW, :], preferred_element_type=F32)
    y = alpha * jnp.where(is_ctx, c_ref[...], x_ref[...]) + _pick(g1_ref[...], is_ctx) * mix
    x1 = _layer_norm(y, lg_ref[...], lb_ref[...])
    x1_ref[...] = x1
    h = x1 * (1.0 + _pick(sc_ref[...], is_ctx)) + _pick(sh_ref[...], is_ctx)
    h_ref[...] = h.astype(h_dtype)
    if route:
        rw = rw_ref[...]
        h_hi = h.astype(BF16)
        h_lo = (h - h_hi.astype(F32)).astype(BF16)
        w_hi = rw.astype(BF16)
        w_lo = (rw - w_hi.astype(F32)).astype(BF16)
        logits = (jnp.dot(h_hi, w_hi, preferred_element_type=F32)
                  + (jnp.dot(h_lo, w_hi, preferred_element_type=F32)
                     + jnp.dot(h_hi, w_lo, preferred_element_type=F32)))
        n_exp = rw_ref.shape[1]
        lane = lax.broadcasted_iota(jnp.int32, logits.shape, 1)
        m1 = jnp.max(logits, axis=-1, keepdims=True)
        i1 = jnp.min(jnp.where(logits == m1, lane, n_exp), axis=-1, keepdims=True)
        rest = jnp.where(lane == i1, -jnp.inf, logits)
        m2 = jnp.max(rest, axis=-1, keepdims=True)
        i2 = jnp.min(jnp.where(rest == m2, lane, n_exp), axis=-1, keepdims=True)
        e2 = jnp.exp(m2 - m1)
        idx_ref[...] = jnp.concatenate([i1, i2], axis=1)
        gate_ref[...] = jnp.concatenate([1.0 / (1.0 + e2), e2 / (1.0 + e2)], axis=1)


def _mix_out(parts, s5, w_out_bf, stream, mod, ln_g, ln_b, layer, alpha, n_ctx_blocks, router_w):
    S, D = stream.rows, stream.width
    GW = parts[0].shape[1]
    L = ln_g.shape[0]
    nb = S // ROW_BLOCK
    route = router_w is not None
    h_dtype = F32 if route else BF16
    yf, yr, z, col_block, s5_d, glu_w, glu_b = s5
    part = pl.BlockSpec((ROW_BLOCK, GW), lambda i: (i, 0))
    rot = pl.BlockSpec((ROW_BLOCK, GW), lambda i: ((i + nb - n_ctx_blocks) % nb, 0))
    gvec = pl.BlockSpec((None, 1, GW), lambda i: (layer, 0, 0))
    rows = pl.BlockSpec((ROW_BLOCK, D), lambda i: (i, 0))
    vspec = pl.BlockSpec((None, 1, D), lambda i: (layer, 0, 0))
    in_specs = [part, part, part,
                part, rot, pl.BlockSpec((ROW_BLOCK, GW), lambda i: (i, col_block)), gvec,
                pl.BlockSpec((None, GW, GW), lambda i: (layer, 0, 0)), gvec,
                pl.BlockSpec((D, D), lambda i: (0, 0))] + stream.specs(1) + [
                _mod_spec(layer, 2, D, 1), vspec, vspec, _mod_spec(layer, 3, D, 1), _mod_spec(layer, 4, D, 1)]
    args = list(parts) + [yf, yr, z, s5_d.reshape(L, 1, GW), glu_w, glu_b.reshape(L, 1, GW),
                          w_out_bf] + stream.args() + [mod, ln_g.reshape(L, 1, D), ln_b.reshape(L, 1, D), mod, mod]
    out_specs = [rows, rows]
    out_shape = [jax.ShapeDtypeStruct((S, D), F32), jax.ShapeDtypeStruct((S, D), h_dtype)]
    if route:
        E = router_w.shape[-1]
        in_specs.append(pl.BlockSpec((D, E), lambda i: (0, 0)))
        args.append(router_w)
        out_specs += [pl.BlockSpec((ROW_BLOCK, TOP_K), lambda i: (i, 0))] * 2
        out_shape += [jax.ShapeDtypeStruct((S, TOP_K), jnp.int32), jax.ShapeDtypeStruct((S, TOP_K), F32)]
    return pl.pallas_call(
        functools.partial(_mixout_kernel, alpha=alpha, n_ctx_blocks=n_ctx_blocks, route=route, h_dtype=h_dtype),
        grid=(S // ROW_BLOCK,),
        in_specs=in_specs,
        out_specs=out_specs,
        out_shape=out_shape,
        compiler_params=_cparams(1),
    )(*args)


FFN_TM = 512
FFN_TF = 512
FFN_TN = 512


def _expert_changed(te_ref, i):
    prev = te_ref[jnp.maximum(i - 1, 0)]
    return jnp.logical_or(i == 0, te_ref[i] != prev)


def _ffn_up_kernel(te_ref, src_ref, live_ref, h_ref, w1_ref, w3_ref, o_ref, w1b_ref, w3b_ref):
    i = pl.program_id(1)
    half = h_ref.shape[0] // 2

    @pl.when(_expert_changed(te_ref, i))
    def _():
        w1b_ref[...] = w1_ref[...].astype(BF16)
        w3b_ref[...] = w3_ref[...].astype(BF16)

    def act(h):
        a = jnp.dot(h, w1b_ref[...], preferred_element_type=F32)
        b = jnp.dot(h, w3b_ref[...], preferred_element_type=F32)
        return (_silu(a) * b).astype(o_ref.dtype)

    @pl.when(live_ref[i] == 2)
    def _():
        o_ref[...] = act(h_ref[...])

    @pl.when(live_ref[i] == 1)
    def _():
        o_ref[0:half, :] = act(h_ref[0:half, :])
        o_ref[half:, :] = jnp.zeros((half, o_ref.shape[1]), o_ref.dtype)

    @pl.when(live_ref[i] == 0)
    def _():
        o_ref[...] = jnp.zeros_like(o_ref)


def _ffn_down_kernel(te_ref, src_ref, live_ref, g_ref, w2_ref, o_ref, w2b_ref):
    i = pl.program_id(1)

    @pl.when(_expert_changed(te_ref, i))
    def _():
        w2b_ref[...] = w2_ref[...].astype(BF16)

    @pl.when(live_ref[i] > 0)
    def _():
        o_ref[...] = jnp.dot(g_ref[...], w2b_ref[...], preferred_element_type=F32)

    @pl.when(live_ref[i] == 0)
    def _():
        o_ref[...] = jnp.zeros_like(o_ref)


def _swiglu_tiles(hs, w1, w3, w2, up, down):
    R, D = hs.shape
    _, _, F = w1.shape
    tf = FFN_TF if F % FFN_TF == 0 else F
    tn = FFN_TN
    tm_u, te_u, src_u, live_u = up
    tm_d, te_d, src_d, live_d = down
    assert R % tm_u == 0 and R % tm_d == 0 and F % tf == 0 and D % tn == 0
    g = pl.pallas_call(
        _ffn_up_kernel,
        grid_spec=pltpu.PrefetchScalarGridSpec(
            num_scalar_prefetch=3,
            grid=(F // tf, R // tm_u),
            in_specs=[pl.BlockSpec((tm_u, D), lambda j, i, te, src, lv: (src[i], 0)),
                      pl.BlockSpec((None, D, tf), lambda j, i, te, src, lv: (te[i], 0, j)),
                      pl.BlockSpec((None, D, tf), lambda j, i, te, src, lv: (te[i], 0, j))],
            out_specs=pl.BlockSpec((tm_u, tf), lambda j, i, te, src, lv: (i, j)),
            scratch_shapes=[pltpu.VMEM((D, tf), BF16), pltpu.VMEM((D, tf), BF16)]),
        out_shape=jax.ShapeDtypeStruct((R, F), BF16),
        compiler_params=_cparams(2),
    )(te_u, src_u, live_u, hs, w1, w3)
    return pl.pallas_call(
        _ffn_down_kernel,
        grid_spec=pltpu.PrefetchScalarGridSpec(
            num_scalar_prefetch=3,
            grid=(D // tn, R // tm_d),
            in_specs=[pl.BlockSpec((tm_d, F), lambda j, i, te, src, lv: (src[i], 0)),
                      pl.BlockSpec((None, F, tn), lambda j, i, te, src, lv: (te[i], 0, j))],
            out_specs=pl.BlockSpec((tm_d, tn), lambda j, i, te, src, lv: (i, j)),
            scratch_shapes=[pltpu.VMEM((F, tn), BF16)]),
        out_shape=jax.ShapeDtypeStruct((R, D), F32),
        compiler_params=_cparams(2),
    )(te_d, src_d, live_d, g, w2)


def _row_copy(src_ref, dst_ref, src_row, dst_row, sem):
    return pltpu.make_async_copy(src_ref.at[pl.ds(src_row, 1)], dst_ref.at[pl.ds(dst_row, 1)], sem)


DMA_UNROLL = 8


def _gather_kernel(tok_ref, live_ref, src_ref, o_ref, buf_ref, sem):
    i = pl.program_id(0)
    tm = buf_ref.shape[0]
    live = live_ref[i] > 0

    def issue(r, c):
        _row_copy(src_ref, buf_ref, tok_ref[i * tm + r], r, sem).start()
        return c

    def drain(r, c):
        _row_copy(src_ref, buf_ref, 0, r, sem).wait()
        return c

    @pl.when(live)
    def _():
        lax.fori_loop(0, tm, issue, 0, unroll=DMA_UNROLL)
        lax.fori_loop(0, tm, drain, 0, unroll=DMA_UNROLL)
        o_ref[...] = buf_ref[...].astype(o_ref.dtype)

    @pl.when(jnp.logical_not(live))
    def _():
        o_ref[...] = jnp.zeros_like(o_ref)


def _gather_rows(src, tok_of_slot, live, tm):
    R = tok_of_slot.shape[0]
    D = src.shape[1]
    return pl.pallas_call(
        _gather_kernel,
        grid_spec=pltpu.PrefetchScalarGridSpec(
            num_scalar_prefetch=2,
            grid=(R // tm,),
            in_specs=[pl.BlockSpec(memory_space=pl.ANY)],
            out_specs=pl.BlockSpec((tm, D), lambda i, tok, nu: (i, 0)),
            scratch_shapes=[pltpu.VMEM((tm, D), src.dtype), pltpu.SemaphoreType.DMA(())]),
        out_shape=jax.ShapeDtypeStruct((R, D), BF16),
        compiler_params=_cparams(1),
    )(tok_of_slot, live, src)


def _ln2_dense_kernel(x_ref, f_ref, g2_ref, lg_ref, lb_ref, o_ref, *, alpha, n_ctx_blocks, row_off):
    is_ctx = (pl.program_id(0) + row_off) < n_ctx_blocks
    y = alpha * x_ref[...] + _pick(g2_ref[...], is_ctx) * f_ref[...]
    o_ref[...] = _layer_norm(y, lg_ref[...], lb_ref[...])


def _ln2_moe_kernel(sa_ref, sb_ref, x_ref, y_ref, gate_ref, g2_ref, lg_ref, lb_ref, o_ref, bufa_ref, bufb_ref,
                    sem, *, alpha, n_ctx_blocks, row_off):
    i = pl.program_id(0)
    R = ROW_BLOCK
    base = (i + row_off) * R

    def issue(r, c):
        _row_copy(y_ref, bufa_ref, sa_ref[base + r], r, sem).start()
        _row_copy(y_ref, bufb_ref, sb_ref[base + r], r, sem).start()
        return c

    def drain(r, c):
        _row_copy(y_ref, bufa_ref, 0, r, sem).wait()
        _row_copy(y_ref, bufb_ref, 0, r, sem).wait()
        return c

    lax.fori_loop(0, R, issue, 0, unroll=DMA_UNROLL // 2)
    lax.fori_loop(0, R, drain, 0, unroll=DMA_UNROLL // 2)
    gate = gate_ref[...]
    f = gate[:, 0:1] * bufa_ref[...] + gate[:, 1:2] * bufb_ref[...]
    is_ctx = (i + row_off) < n_ctx_blocks
    y = alpha * x_ref[...] + _pick(g2_ref[...], is_ctx) * f
    o_ref[...] = _layer_norm(y, lg_ref[...], lb_ref[...])


def _ln2(x1, f, mod, ln_g, ln_b, layer, alpha, n_ctx_blocks, row_off, moe=None):
    S, D = x1.shape
    L = ln_g.shape[0]
    nb = S // ROW_BLOCK - row_off
    n_pre = 0 if moe is None else 2
    wrap = (lambda f_: (lambda i, *_: f_(i)))
    rows_in = pl.BlockSpec((ROW_BLOCK, D), wrap(lambda i: (i + row_off, 0)))
    rows_out = pl.BlockSpec((ROW_BLOCK, D), wrap(lambda i: (i, 0)))
    vspec = pl.BlockSpec((None, 1, D), wrap(lambda i: (layer, 0, 0)))
    mspec = pl.BlockSpec((None, SUBLANES, D), wrap(lambda i: (layer, 0, 5)))
    common = dict(alpha=alpha, n_ctx_blocks=n_ctx_blocks, row_off=row_off)
    lg, lb = ln_g.reshape(L, 1, D), ln_b.reshape(L, 1, D)
    out_shape = jax.ShapeDtypeStruct((nb * ROW_BLOCK, D), F32)
    if moe is None:
        return pl.pallas_call(
            functools.partial(_ln2_dense_kernel, **common),
            grid=(nb,),
            in_specs=[rows_in, rows_in, mspec, vspec, vspec],
            out_specs=rows_out,
            out_shape=out_shape,
            compiler_params=_cparams(1),
        )(x1, f, mod, lg, lb)
    slot_a, slot_b, gates = moe
    return pl.pallas_call(
        functools.partial(_ln2_moe_kernel, **common),
        grid_spec=pltpu.PrefetchScalarGridSpec(
            num_scalar_prefetch=n_pre,
            grid=(nb,),
            in_specs=[rows_in, pl.BlockSpec(memory_space=pl.ANY),
                      pl.BlockSpec((ROW_BLOCK, TOP_K), wrap(lambda i: (i + row_off, 0))),
                      mspec, vspec, vspec],
            out_specs=rows_out,
            scratch_shapes=[pltpu.VMEM((ROW_BLOCK, D), F32), pltpu.VMEM((ROW_BLOCK, D), F32),
                            pltpu.SemaphoreType.DMA(())]),
        out_shape=out_shape,
        compiler_params=_cparams(1),
    )(slot_a, slot_b, x1, f, gates, mod, lg, lb)


def _route_slots(idx, row0, n_experts, tm):
    S = idx.shape[0]
    n = S - row0
    e_flat = idx[row0:].reshape(-1)
    onehot = (e_flat[:, None] == jnp.arange(n_experts, dtype=jnp.int32)[None, :]).astype(jnp.int32)
    pos = jnp.sum((jnp.cumsum(onehot, axis=0) - 1) * onehot, axis=1)
    counts = jnp.sum(onehot, axis=0)
    big = 2 * tm
    padded = ((counts + big - 1) // big) * big
    ends = jnp.cumsum(padded)
    starts = ends - padded
    slot = starts[e_flat] + pos
    n_big = (TOP_K * n) // big + n_experts
    tok = jnp.repeat(jnp.arange(n, dtype=jnp.int32) + row0, TOP_K)
    tok_of_slot = jnp.full((n_big * big,), row0, jnp.int32).at[slot].set(tok)
    t = jnp.arange(2 * n_big, dtype=jnp.int32)
    te = jnp.minimum(jnp.searchsorted(ends, t * tm, side='right'), n_experts - 1).astype(jnp.int32)
    live = (t * tm < starts[te] + counts[te]).astype(jnp.int32)
    src = jnp.maximum(lax.cummax(jnp.where(live > 0, t, -1)), 0)
    down = (tm, te[src], src, live)
    live_big = live[0::2] + live[1::2]
    src_big = src[0::2] // 2
    up = (big, te[0::2][src_big], src_big, live_big)
    slot2 = slot.reshape(n, TOP_K).astype(jnp.int32)
    pad = jnp.zeros((row0,), jnp.int32)
    slot_a = jnp.concatenate([pad, slot2[:, 0]])
    slot_b = jnp.concatenate([pad, slot2[:, 1]])
    return tok_of_slot, up, down, slot_a, slot_b


def kernel(x, c, ctx, c_ctx, w_mod, b_mod, w_in, w_out, ln1_g, ln1_b, ln2_g, ln2_b, pool_w, pool_scale,
           diff_lambda, diff_subln_g, conv_dw, conv_db, conv_ln_g, conv_ln_b, conv_pw, s5_a_re, s5_a_im,
           s5_log_dt, s5_b_re, s5_b_im, s5_c_re, s5_c_im, s5_d, s5_glu_w, s5_glu_b, ffn_w1, ffn_w3, ffn_w2,
           router_w, moe_w1, moe_w3, moe_w2):
    B, T, D = x.shape
    Tc = ctx.shape[1]
    depth = w_mod.shape[0]
    assert B == 1 and Tc % ROW_BLOCK == 0 and T % ROW_BLOCK == 0
    GW = D // N_GROUPS
    n_ctx_blocks = Tc // ROW_BLOCK
    alpha = (2.0 * depth) ** 0.25

    cc = jnp.zeros((SUBLANES, D), F32).at[0].set(c[0]).at[1].set(c_ctx)
    mod = _modulation(cc, w_mod, b_mod)
    cos, sin = _rope_tables(T, Tc, LANES)
    stream = _Stream(x[0], ctx[0], n_ctx_blocks)

    POOL_B, Q_B, K_B, V_B, CONV_B, S5_B = 0, 1, 2, 3, 2, 6

    for l in range(depth):
        last = l == depth - 1
        lam_init = 0.8 - 0.6 * math.exp(-0.3 * l)
        z = _in_projection(stream, mod, w_in, l, n_ctx_blocks)

        pa = _pool_mixer(z, pool_w, pool_scale, l, n_ctx_blocks)

        qt, kk, vt, kn2 = _qkv_prep(z, cos, sin, GW, Q_B, K_B, V_B)
        kn = jnp.sqrt(jnp.max(kn2[n_ctx_blocks:, :, 0], axis=0)) * (1.0 + 2.0 ** -6)
        lv = diff_lambda[l].astype(F32)
        lam = jnp.exp(jnp.sum(lv[0] * lv[1])) - jnp.exp(jnp.sum(lv[2] * lv[3])) + lam_init
        lam_row = jnp.full((1, ATTN_TQ), lam, F32)
        g_col = (diff_subln_g[l].astype(F32) * (1.0 - lam_init)).reshape(GW, 1)
        pb = _diff_attention(qt, kk, vt, lam_row, g_col, kn, Tc)

        pcv = _conv_mixer(z, CONV_B, conv_dw, conv_db, conv_ln_g, conv_ln_b, conv_pw, l, n_ctx_blocks)

        seg_len = stream.rows // S5_SEGS
        pf = _s5_params(s5_a_re[l, 0], s5_a_im[l, 0], s5_log_dt[l, 0], s5_b_re[l, 0], s5_b_im[l, 0],
                        s5_c_re[l, 0], s5_c_im[l, 0], seg_len)
        pr = _s5_params(s5_a_re[l, 1], s5_a_im[l, 1], s5_log_dt[l, 1], s5_b_re[l, 1], s5_b_im[l, 1],
                        s5_c_re[l, 1], s5_c_im[l, 1], seg_len)
        yf, yr = _s5_scan(z, S5_B, pf, pr, Tc)
        s5 = (yf, yr, z, S5_B, s5_d, s5_glu_w, s5_glu_b)

        w_out_bf = _cast_bf16(w_out, l)
        row_off = n_ctx_blocks if last else 0
        if l % 2 == 0:
            x1, h = _mix_out((pa, pb, pcv), s5, w_out_bf, stream, mod, ln1_g, ln1_b, l, alpha, n_ctx_blocks, None)
            S = stream.rows
            def dense_tiles(sizes, live):
                tm = next(t for t in sizes if S % t == 0)
                nt = S // tm
                return (tm, jnp.zeros((nt,), jnp.int32), jnp.arange(nt, dtype=jnp.int32),
                        jnp.full((nt,), live, jnp.int32))
            f = _swiglu_tiles(h, ffn_w1[l // 2][None], ffn_w3[l // 2][None], ffn_w2[l // 2][None],
                              dense_tiles((1408, 768, 512, ROW_BLOCK), 2), dense_tiles((768, 512, ROW_BLOCK), 1))
            xs_new = _ln2(x1, f, mod, ln2_g, ln2_b, l, alpha, n_ctx_blocks, row_off)
        else:
            x1, h, idx, gates = _mix_out((pa, pb, pcv), s5, w_out_bf, stream, mod, ln1_g, ln1_b, l, alpha,
                                         n_ctx_blocks, router_w[l // 2])
            n_exp = router_w.shape[-1]
            row0 = row_off * ROW_BLOCK
            tok_of_slot, up, down, slot_a, slot_b = _route_slots(idx, row0, n_exp, FFN_TM)
            hs = _gather_rows(h, tok_of_slot, down[3], FFN_TM)
            y = _swiglu_tiles(hs, moe_w1[l // 2], moe_w3[l // 2], moe_w2[l // 2], up, down)
            xs_new = _ln2(x1, y, mod, ln2_g, ln2_b, l, alpha, n_ctx_blocks, row_off, moe=(slot_a, slot_b, gates))
        stream = _Stream(xs_new, xs_new, n_ctx_blocks)
    return xs_new[None]
```

```python
import functools
import math

import numpy as np
import jax
import jax.numpy as jnp
from jax import lax
from jax.experimental import pallas as pl
from jax.experimental.pallas import tpu as pltpu

F32 = jnp.float32
BF16 = jnp.bfloat16

GRID_W = 64
N_GROUPS = 4
POOL_WINDOWS = (2, 4, 8, 16)
DIFF_HEADS = 8
DIFF_QK = 32
CONV_WIDTH = 31
S5_P = 16
S5_N = 64
TOP_K = 2
ROPE_BASE = 10000.0
LN_EPS = 1e-5

LANES = 128
SUBLANES = 8
ROW_BLOCK = 256
VMEM_LIMIT = 56 * 1024 * 1024


def _cparams(n_axes, vmem=VMEM_LIMIT):
    return pltpu.CompilerParams(dimension_semantics=("arbitrary",) * n_axes, vmem_limit_bytes=vmem)


def _layer_norm(y, g, b):
    mu = jnp.mean(y, -1, keepdims=True)
    yc = y - mu
    var = jnp.mean(yc * yc, -1, keepdims=True)
    return yc * lax.rsqrt(var + LN_EPS) * g + b


def _silu(x):
    return x * jax.nn.sigmoid(x)


def _mod_kernel(cc_ref, w_ref, b_ref, o_ref):
    a = _silu(cc_ref[...])
    o_ref[...] = jnp.dot(a.astype(BF16), w_ref[...].astype(BF16), preferred_element_type=F32) + b_ref[...]


def _modulation(cc, w_mod, b_mod):
    L, D, N = w_mod.shape
    tn = 1536
    assert N % tn == 0
    return pl.pallas_call(
        _mod_kernel,
        grid=(L, N // tn),
        in_specs=[pl.BlockSpec((SUBLANES, D), lambda l, j: (0, 0)),
                  pl.BlockSpec((None, D, tn), lambda l, j: (l, 0, j)),
                  pl.BlockSpec((None, 1, tn), lambda l, j: (l, 0, j))],
        out_specs=pl.BlockSpec((None, SUBLANES, tn), lambda l, j: (l, 0, j)),
        out_shape=jax.ShapeDtypeStruct((L, SUBLANES, N), F32),
        compiler_params=_cparams(2),
    )(cc, w_mod, b_mod.reshape(L, 1, N))


def _mod_spec(layer, chunk, D, n_grid_axes):
    if n_grid_axes == 1:
        return pl.BlockSpec((None, SUBLANES, D), lambda i: (layer, 0, chunk))
    return pl.BlockSpec((None, SUBLANES, D), lambda j, i: (layer, 0, chunk))


def _pick(m, is_ctx):
    return jnp.where(is_ctx, m[1:2, :], m[0:1, :])


class _Stream:
    def __init__(self, latent, context, n_ctx_blocks):
        self.latent = latent
        self.context = context
        self.n_ctx_blocks = n_ctx_blocks
        self.split = latent is not context
        self.off = 0 if self.split else n_ctx_blocks
        self.rows = (latent.shape[0] // ROW_BLOCK - self.off + n_ctx_blocks) * ROW_BLOCK
        self.width = latent.shape[1]

    def specs(self, n_grid_axes):
        ncb, D = self.n_ctx_blocks, self.width
        lat = (lambda i: (jnp.maximum(i - ncb, 0), 0)) if self.split else (lambda i: (i, 0))
        ctx = lambda i: (jnp.minimum(i, ncb - 1), 0)
        if n_grid_axes == 1:
            return [pl.BlockSpec((ROW_BLOCK, D), lat), pl.BlockSpec((ROW_BLOCK, D), ctx)]
        return [pl.BlockSpec((ROW_BLOCK, D), lambda j, i: lat(i)), pl.BlockSpec((ROW_BLOCK, D), lambda j, i: ctx(i))]

    def args(self):
        return [self.latent, self.context]


def _inproj_kernel(x_ref, c_ref, sh_ref, sc_ref, w_ref, o_ref, wb_ref, *, n_ctx_blocks, split):
    i = pl.program_id(1)

    @pl.when(i == 0)
    def _():
        wb_ref[...] = w_ref[...].astype(BF16)

    is_ctx = i < n_ctx_blocks
    x = jnp.where(is_ctx, c_ref[...], x_ref[...]) if split else x_ref[...]
    h = x * (1.0 + _pick(sc_ref[...], is_ctx)) + _pick(sh_ref[...], is_ctx)
    o_ref[...] = jnp.dot(h.astype(BF16), wb_ref[...], preferred_element_type=F32).astype(o_ref.dtype)


def _in_projection(stream, mod, w_in, layer, n_ctx_blocks):
    S, D = stream.rows, stream.width
    N = w_in.shape[-1]
    tn = 1792
    assert N % tn == 0 and S % ROW_BLOCK == 0
    return pl.pallas_call(
        functools.partial(_inproj_kernel, n_ctx_blocks=n_ctx_blocks, split=stream.split),
        grid=(N // tn, S // ROW_BLOCK),
        in_specs=stream.specs(2) + [
                  _mod_spec(layer, 0, D, 2),
                  _mod_spec(layer, 1, D, 2),
                  pl.BlockSpec((None, D, tn), lambda j, i: (layer, 0, j))],
        out_specs=pl.BlockSpec((ROW_BLOCK, tn), lambda j, i: (i, j)),
        out_shape=jax.ShapeDtypeStruct((S, N), BF16),
        scratch_shapes=[pltpu.VMEM((D, tn), BF16)],
        compiler_params=_cparams(2),
    )(*stream.args(), mod, mod, w_in)


def _seq_edges(i, n_blocks, n_ctx_blocks):
    prev_ok = jnp.logical_and(i != 0, i != n_ctx_blocks)
    next_ok = jnp.logical_and(i != n_ctx_blocks - 1, i != n_blocks - 1)
    return prev_ok, next_ok


def _pool_kernel(p_ref, c_ref, n_ref, w_ref, scale_ref, o_ref, ext_ref, *, n_blocks, n_ctx_blocks):
    i = pl.program_id(0)
    R = ROW_BLOCK
    halo = SUBLANES
    prev_ok, next_ok = _seq_edges(i, n_blocks, n_ctx_blocks)
    cur = c_ref[...].astype(F32)
    pack = 2 * SUBLANES
    ext_ref[0:halo, :] = jnp.where(prev_ok, p_ref[R - pack:R, :].astype(F32)[pack - halo:], 0.0)
    ext_ref[halo:halo + R, :] = cur
    ext_ref[halo + R:halo + R + halo, :] = jnp.where(next_ok, n_ref[0:pack, :].astype(F32)[:halo], 0.0)
    rloc = lax.broadcasted_iota(jnp.int32, (R, 1), 0)
    gw = cur.shape[1] // len(POOL_WINDOWS)
    ext = ext_ref[...]
    n_ext = ext.shape[0]
    accs = [jnp.zeros((R, gw), F32) for _ in POOL_WINDOWS]
    for r in range(SUBLANES):
        rolled = ext if r == 0 else pltpu.roll(ext, n_ext - r, 0)
        for g, w in enumerate(POOL_WINDOWS):
            for d in range(-(w // 2), w - w // 2):
                m, rr = divmod(halo + d, SUBLANES)
                if rr == r:
                    accs[g] = accs[g] + rolled[m * SUBLANES:m * SUBLANES + R, g * gw:(g + 1) * gw]
    outs = []
    for g, w in enumerate(POOL_WINDOWS):
        below = jnp.where(prev_ok, 0, jnp.maximum(w // 2 - rloc, 0))
        above = jnp.where(next_ok, 0, jnp.maximum(rloc + (w - w // 2) - R, 0))
        cnt = (w - below - above).astype(F32)
        diff = accs[g] / cnt - cur[:, g * gw:(g + 1) * gw]
        outs.append(jnp.dot(diff.astype(BF16), w_ref[g].astype(BF16), preferred_element_type=F32))
    o_ref[...] = (jnp.concatenate(outs, axis=-1) * scale_ref[...]).astype(o_ref.dtype)


def _pool_mixer(z, pool_w, pool_scale, layer, n_ctx_blocks):
    S = z.shape[0]
    nb = S // ROW_BLOCK
    GW = pool_scale.shape[-1]
    G, gw = pool_w.shape[1], pool_w.shape[2]
    return pl.pallas_call(
        functools.partial(_pool_kernel, n_blocks=nb, n_ctx_blocks=n_ctx_blocks),
        grid=(nb,),
        in_specs=[pl.BlockSpec((ROW_BLOCK, GW), lambda i: (jnp.maximum(i - 1, 0), 0)),
                  pl.BlockSpec((ROW_BLOCK, GW), lambda i: (i, 0)),
                  pl.BlockSpec((ROW_BLOCK, GW), lambda i: (jnp.minimum(i + 1, nb - 1), 0)),
                  pl.BlockSpec((None, G, gw, gw), lambda i: (layer, 0, 0, 0)),
                  pl.BlockSpec((None, 1, GW), lambda i: (layer, 0, 0))],
        out_specs=pl.BlockSpec((ROW_BLOCK, GW), lambda i: (i, 0)),
        out_shape=jax.ShapeDtypeStruct((S, GW), BF16),
        scratch_shapes=[pltpu.VMEM((ROW_BLOCK + 2 * SUBLANES, GW), F32)],
        compiler_params=_cparams(1),
    )(z, z, z, pool_w, pool_scale.reshape(pool_scale.shape[0], 1, GW))


CONV_HALO = 16


def _conv_kernel(p_ref, c_ref, n_ref, dw_ref, db_ref, g_ref, b_ref, pw_ref, o_ref, ext_ref,
                 *, n_blocks, n_ctx_blocks):
    i = pl.program_id(0)
    R = ROW_BLOCK
    H = CONV_HALO
    GW = o_ref.shape[1]
    prev_ok, next_ok = _seq_edges(i, n_blocks, n_ctx_blocks)

    def glu(u):
        u = u.astype(F32)
        return u[:, :GW] * jax.nn.sigmoid(u[:, GW:])

    ext_ref[0:H, :] = jnp.where(prev_ok, glu(p_ref[R - H:R, :]), 0.0)
    ext_ref[H:H + R, :] = glu(c_ref[...])
    ext_ref[H + R:H + R + H, :] = jnp.where(next_ok, glu(n_ref[0:H, :]), 0.0)
    off = H - CONV_WIDTH // 2
    ext = ext_ref[...]
    n_ext = ext.shape[0]
    acc = jnp.zeros((R, GW), F32)
    for r in range(SUBLANES):
        rolled = ext if r == 0 else pltpu.roll(ext, n_ext - r, 0)
        for j in range(CONV_WIDTH):
            m, rr = divmod(off + j, SUBLANES)
            if rr == r:
                acc = acc + rolled[m * SUBLANES:m * SUBLANES + R, :] * dw_ref[j:j + 1, :]
    y = _layer_norm(acc + db_ref[...], g_ref[...], b_ref[...])
    o_ref[...] = jnp.dot(_silu(y).astype(BF16), pw_ref[...].astype(BF16),
                         preferred_element_type=F32).astype(o_ref.dtype)


def _conv_mixer(z, col_block, conv_dw, conv_db, conv_ln_g, conv_ln_b, conv_pw, layer, n_ctx_blocks):
    S = z.shape[0]
    nb = S // ROW_BLOCK
    GW = conv_db.shape[-1]
    L = conv_db.shape[0]
    vec = lambda a: a.reshape(L, 1, GW)
    vspec = pl.BlockSpec((None, 1, GW), lambda i: (layer, 0, 0))
    return pl.pallas_call(
        functools.partial(_conv_kernel, n_blocks=nb, n_ctx_blocks=n_ctx_blocks),
        grid=(nb,),
        in_specs=[pl.BlockSpec((ROW_BLOCK, 2 * GW), lambda i: (jnp.maximum(i - 1, 0), col_block)),
                  pl.BlockSpec((ROW_BLOCK, 2 * GW), lambda i: (i, col_block)),
                  pl.BlockSpec((ROW_BLOCK, 2 * GW), lambda i: (jnp.minimum(i + 1, nb - 1), col_block)),
                  pl.BlockSpec((None, CONV_WIDTH, GW), lambda i: (layer, 0, 0)),
                  vspec, vspec, vspec,
                  pl.BlockSpec((None, GW, GW), lambda i: (layer, 0, 0))],
        out_specs=pl.BlockSpec((ROW_BLOCK, GW), lambda i: (i, 0)),
        out_shape=jax.ShapeDtypeStruct((S, GW), BF16),
        scratch_shapes=[pltpu.VMEM((ROW_BLOCK + 2 * CONV_HALO, GW), F32)],
        compiler_params=_cparams(1),
    )(z, z, z, conv_dw, vec(conv_db), vec(conv_ln_g), vec(conv_ln_b), conv_pw)


def _rope_tables(T, Tc, width):
    ax = DIFF_QK // 2
    inv = ROPE_BASE ** (-jnp.arange(0, ax, 2, dtype=F32) / ax)
    t = jnp.arange(T)
    row = (t // GRID_W).astype(F32)
    col = (t % GRID_W).astype(F32)
    ang = jnp.stack([row[:, None] * inv, col[:, None] * inv], axis=1)
    cos = jnp.cos(ang)[:, :, None, :]
    sin = jnp.sin(ang)[:, :, None, :]
    cos = jnp.broadcast_to(cos, (T, 2, 2, ax // 2)).reshape(T, DIFF_QK)
    sin = jnp.concatenate([-sin, sin], axis=2).reshape(T, DIFF_QK)
    reps = width // DIFF_QK
    cos = jnp.concatenate([jnp.ones((Tc, DIFF_QK), F32), cos], axis=0)
    sin = jnp.concatenate([jnp.zeros((Tc, DIFF_QK), F32), sin], axis=0)
    return jnp.tile(cos, (1, reps)), jnp.tile(sin, (1, reps))


def _qkv_prep_kernel(q_ref, k_ref, v_ref, cos_ref, sin_ref, qo_ref, ko_ref, vo_ref, kn_ref):
    W = q_ref.shape[1]
    half = DIFF_QK // 4
    lane = lax.broadcasted_iota(jnp.int32, (1, W), 1)
    first = (lane % (2 * half)) < half
    cos = jnp.concatenate([cos_ref[...]] * (W // LANES), axis=1)
    sin = jnp.concatenate([sin_ref[...]] * (W // LANES), axis=1)

    def rope(x):
        partner = jnp.where(first, pltpu.roll(x, W - half, 1), pltpu.roll(x, half, 1))
        return x * cos + partner * sin

    qo_ref[...] = (rope(q_ref[...].astype(F32)) * (DIFF_QK ** -0.5 * math.log2(math.e))).T.astype(BF16)
    kb = rope(k_ref[...].astype(F32)).astype(BF16)
    ko_ref[...] = kb
    ksq = kb.astype(F32).T
    ksq = ksq * ksq
    for grp in range(W // DIFF_QK):
        n2 = jnp.sum(ksq[grp * DIFF_QK:(grp + 1) * DIFF_QK, :], axis=0, keepdims=True)
        kn_ref[grp:grp + 1, :] = jnp.broadcast_to(jnp.max(n2, axis=1, keepdims=True), (1, LANES))
    vt = v_ref[...].astype(F32).T.astype(BF16)
    dv = LANES // 2
    ones = jnp.ones((ATTN_VROWS - dv, vt.shape[1]), BF16)
    for h in range(W // dv):
        vo_ref[h * ATTN_VROWS:h * ATTN_VROWS + dv, :] = vt[h * dv:(h + 1) * dv, :]
        vo_ref[h * ATTN_VROWS + dv:(h + 1) * ATTN_VROWS, :] = ones


def _qkv_prep(z, cos, sin, W, q_blk, k_blk, v_blk):
    S = z.shape[0]
    nb = S // ROW_BLOCK
    row = lambda c: pl.BlockSpec((ROW_BLOCK, W), lambda i: (i, c))
    tab = pl.BlockSpec((ROW_BLOCK, LANES), lambda i: (i, 0))
    return pl.pallas_call(
        _qkv_prep_kernel,
        grid=(nb,),
        in_specs=[row(q_blk), row(k_blk), row(v_blk), tab, tab],
        out_specs=[pl.BlockSpec((W, ROW_BLOCK), lambda i: (0, i)), row(0),
                   pl.BlockSpec((DIFF_HEADS * ATTN_VROWS, ROW_BLOCK), lambda i: (0, i)),
                   pl.BlockSpec((None, W // DIFF_QK, LANES), lambda i: (i, 0, 0))],
        out_shape=[jax.ShapeDtypeStruct((W, S), BF16), jax.ShapeDtypeStruct((S, W), BF16),
                   jax.ShapeDtypeStruct((DIFF_HEADS * ATTN_VROWS, S), BF16),
                   jax.ShapeDtypeStruct((nb, W // DIFF_QK, LANES), F32)],
        compiler_params=_cparams(1),
    )(z, z, z, cos, sin)


ATTN_TQ = 256
ATTN_VROWS = LANES // 2 + 2 * SUBLANES
ATTN_MARGIN = 64.0
ATTN_TK = (4096, 2048, 1024)


def _attn_kernel(qt_ref, k_ref, vt_ref, lam_ref, g_ref, kn_ref, o_ref, qq_ref, m_ref, acc_ref,
                 *, n_ctx, n_ctx_blocks, n_lat_chunks, tk):
    i = pl.program_id(1)
    tq = ATTN_TQ
    dv = LANES // 2
    qt = qt_ref[...]
    feat = lax.broadcasted_iota(jnp.int32, (LANES, 1), 0)
    zero = jnp.zeros_like(qt)
    for hh in range(2):
        for comp in range(2):
            lo = hh * dv + comp * DIFF_QK
            keep = jnp.logical_and(feat >= lo, feat < lo + DIFF_QK)
            qq_ref[hh, :, comp * tq:(comp + 1) * tq] = jnp.where(keep, qt, zero)

    def attend(start, size, mode):
        kk = k_ref[pl.ds(start, size), :]
        for hh in range(2):
            s = jnp.dot(kk, qq_ref[hh], preferred_element_type=F32)
            vv = vt_ref[hh * ATTN_VROWS:(hh + 1) * ATTN_VROWS, pl.ds(start, size)]
            mx = jnp.max(s, axis=0, keepdims=True)
            if mode == "first":
                m_ref[hh] = mx
                p = jnp.exp2(s - mx)
                acc_ref[hh] = jnp.dot(vv, p.astype(BF16), preferred_element_type=F32)
            elif mode == "exact":
                m_old = m_ref[hh]
                m_new = jnp.maximum(m_old, mx)
                m_ref[hh] = m_new
                p = jnp.exp2(s - m_new)
                acc_ref[hh] = (jnp.exp2(m_old - m_new) * acc_ref[hh]
                               + jnp.dot(vv, p.astype(BF16), preferred_element_type=F32))
            else:
                m_old = m_ref[hh]
                p = jnp.exp2(s - m_old)
                m_new = jnp.maximum(m_old, mx)
                m_ref[hh] = m_new
                acc_ref[hh] = (jnp.exp2(m_old - m_new)
                               * (acc_ref[hh] + jnp.dot(vv, p.astype(BF16), preferred_element_type=F32)))

    attend(0, n_ctx, "first")
    n_steps = jnp.where(i < n_ctx_blocks, 0, n_lat_chunks)

    pair = pl.program_id(0)
    col = lax.broadcasted_iota(jnp.int32, (1, 2 * tq), 1)
    excess = jnp.full((1, 2 * tq), -jnp.inf, F32)
    for hh in range(2):
        qf = qq_ref[hh].astype(F32)
        qn = jnp.sqrt(jnp.sum(qf * qf, axis=0, keepdims=True))
        grp = (2 * pair + hh) * 2
        kn = jnp.where(col < tq, kn_ref[grp], kn_ref[grp + 1])
        excess = jnp.maximum(excess, qn * kn - m_ref[hh])
    safe = jnp.max(excess) < ATTN_MARGIN

    def loop(mode):
        def body(c, carry):
            attend(pl.multiple_of(n_ctx + c * tk, LANES), tk, mode)
            return carry
        lax.fori_loop(0, n_steps, body, 0)

    @pl.when(safe)
    def _():
        loop("deferred")

    @pl.when(jnp.logical_not(safe))
    def _():
        loop("exact")

    lam = lam_ref[...]
    outs = []
    for hh in range(2):
        acc = acc_ref[hh]
        ratio = acc[:dv] / acc[dv:dv + 1]
        o = ratio[:, :tq] - lam * ratio[:, tq:]
        r = lax.rsqrt(jnp.sum(o * o, axis=0, keepdims=True) / dv + LN_EPS)
        outs.append(o * r)
    o_ref[...] = (jnp.concatenate(outs, axis=0) * g_ref[...]).T.astype(o_ref.dtype)


def _diff_attention(qt, k, vt, lam, g, kn, n_ctx):
    W, S = qt.shape
    assert S % ATTN_TQ == 0 and n_ctx % ATTN_TQ == 0
    tk = next(t for t in ATTN_TK if (S - n_ctx) % t == 0)
    nq = S // ATTN_TQ
    return pl.pallas_call(
        functools.partial(_attn_kernel, n_ctx=n_ctx, n_ctx_blocks=n_ctx // ATTN_TQ,
                          n_lat_chunks=(S - n_ctx) // tk, tk=tk),
        grid=(W // LANES, nq),
        in_specs=[pl.BlockSpec((LANES, ATTN_TQ), lambda p, i: (p, i)),
                  pl.BlockSpec((S, LANES), lambda p, i: (0, p)),
                  pl.BlockSpec((2 * ATTN_VROWS, S), lambda p, i: (p, 0)),
                  pl.BlockSpec((1, ATTN_TQ), lambda p, i: (0, 0)),
                  pl.BlockSpec((LANES, 1), lambda p, i: (p, 0)),
                  pl.BlockSpec(memory_space=pltpu.SMEM)],
        out_specs=pl.BlockSpec((ATTN_TQ, LANES), lambda p, i: (i, p)),
        out_shape=jax.ShapeDtypeStruct((S, W), BF16),
        scratch_shapes=[pltpu.VMEM((2, LANES, 2 * ATTN_TQ), BF16),
                        pltpu.VMEM((2, 1, 2 * ATTN_TQ), F32),
                        pltpu.VMEM((2, ATTN_VROWS, 2 * ATTN_TQ), F32)],
        compiler_params=_cparams(2),
    )(qt, k, vt, lam, g, kn)


S5_SEGS = SUBLANES
S5_KB = 32
S5_GB = 8


def _s5_params(a_re, a_im, log_dt, b_re, b_im, c_re, c_im, seg_len):
    G, N = a_re.shape
    P = b_re.shape[-1]
    nblk = G // S5_GB
    a_re, a_im = a_re.astype(F32), a_im.astype(F32)
    dt = jnp.exp(log_dt.astype(F32))[:, None]
    lr, li = dt * a_re, dt * a_im
    mag = jnp.exp(lr)
    ar, ai = mag * jnp.cos(li), mag * jnp.sin(li)
    den = a_re * a_re + a_im * a_im
    qr = ((ar - 1.0) * a_re + ai * a_im) / den
    qi = (ai * a_re - (ar - 1.0) * a_im) / den
    b_re, b_im = b_re.astype(F32), b_im.astype(F32)
    br = qr[..., None] * b_re - qi[..., None] * b_im
    bi = qr[..., None] * b_im + qi[..., None] * b_re
    mag_l = jnp.exp(seg_len * lr)
    alr, ali = mag_l * jnp.cos(seg_len * li), mag_l * jnp.sin(seg_len * li)
    eye = jnp.eye(S5_GB, dtype=F32)
    wb = lambda m: jnp.einsum('gh,bgnp->bgphn', eye, m.reshape(nblk, S5_GB, N, P)).reshape(
        nblk, S5_GB * P, S5_GB * N)
    w_in = jnp.concatenate([wb(br), wb(bi)], axis=2)
    cm = lambda m: jnp.einsum('gh,bgpn->bhngp', eye, m.astype(F32).reshape(nblk, S5_GB, P, N)).reshape(
        nblk, S5_GB * N, S5_GB * P)
    w_out = jnp.concatenate([cm(c_re), -cm(c_im)], axis=1)
    row = lambda r, i: jnp.concatenate([r.reshape(1, G * N), i.reshape(1, G * N)], axis=1)
    coef = jnp.broadcast_to(row(ar, ai), (S5_SEGS, 2 * G * N))
    return w_in.astype(BF16), coef, row(alr, ali), w_out.astype(BF16)


def _s5_kernel(*refs, emit_out):
    n_seg = S5_SEGS
    uf_ref = refs[0]
    ur_refs = refs[1:1 + n_seg]
    rest = refs[1 + n_seg:]
    if emit_out:
        (wbf_ref, wbr_ref, af_ref, ar_ref, ef_ref, er_ref, alf_ref, alr_ref, cf_ref, cr_ref,
         yf_ref, yr_ref, stage_ref, bf_ref, br_ref, hf_ref, hr_ref) = rest
    else:
        (wbf_ref, wbr_ref, af_ref, ar_ref, ef_out_ref, er_out_ref,
         stage_ref, bf_ref, br_ref, hf_ref, hr_ref) = rest
    g = pl.program_id(0)
    KB = S5_KB
    R = n_seg * KB
    NS = af_ref.shape[1] // 2
    nblk = wbf_ref.shape[0]
    wi = wbf_ref.shape[1]
    ws = wbf_ref.shape[2] // 2

    def cmul_add(a_row, h, add):
        are, aim = a_row[:, :NS], a_row[:, NS:]
        hre, him = h[:, :NS], h[:, NS:]
        return jnp.concatenate([are * hre - aim * him + add[:, :NS], are * him + aim * hre + add[:, NS:]], axis=1)

    @pl.when(g == 0)
    def _():
        if emit_out:
            def chain(e_ref, al_ref, order):
                al = al_ref[...]
                c = jnp.zeros((1, 2 * NS), F32)
                rows = [None] * n_seg
                for s in order:
                    rows[s] = c
                    c = cmul_add(al, c, e_ref[s:s + 1, :])
                return jnp.concatenate(rows, axis=0)
            hf_ref[...] = chain(ef_ref, alf_ref, range(n_seg))
            hr_ref[...] = chain(er_ref, alr_ref, range(n_seg - 1, -1, -1))
        else:
            hf_ref[...] = jnp.zeros_like(hf_ref)
            hr_ref[...] = jnp.zeros_like(hr_ref)

    def interleaved(load_seg):
        for s in range(n_seg):
            blk = load_seg(s).astype(F32)
            for c in range(nblk):
                stage_ref[c, s * KB:(s + 1) * KB, :] = blk[:, c * wi:(c + 1) * wi]
        rows = [jnp.concatenate([stage_ref[c, pl.ds(kk, n_seg, stride=KB), :] for c in range(nblk)], axis=1)
                for kk in range(KB)]
        return jnp.concatenate(rows, axis=0).astype(BF16)

    def project_in(u, w_ref, buf_ref):
        for b in range(nblk):
            res = jnp.dot(u[:, b * wi:(b + 1) * wi], w_ref[b], preferred_element_type=F32)
            buf_ref[:, b * ws:(b + 1) * ws] = res[:, :ws]
            buf_ref[:, NS + b * ws:NS + (b + 1) * ws] = res[:, ws:]

    def scan(buf_ref, a_ref, h_ref, reverse):
        half = NS // 2
        for c in range(2):
            cre = slice(c * half, (c + 1) * half)
            cim = slice(NS + c * half, NS + (c + 1) * half)
            are, aim = a_ref[:, cre], a_ref[:, cim]

            def step(t, carry, cre=cre, cim=cim, are=are, aim=aim):
                hre, him = carry
                kk = (KB - 1 - t) if reverse else t
                r0 = pl.multiple_of(kk * n_seg, n_seg)
                nre = are * hre - aim * him + buf_ref[pl.ds(r0, n_seg), cre]
                nim = are * him + aim * hre + buf_ref[pl.ds(r0, n_seg), cim]
                if emit_out:
                    buf_ref[pl.ds(r0, n_seg), cre] = nre
                    buf_ref[pl.ds(r0, n_seg), cim] = nim
                return nre, nim

            hre, him = lax.fori_loop(0, KB, step, (h_ref[:, cre], h_ref[:, cim]))
            h_ref[:, cre] = hre
            h_ref[:, cim] = him

    def project_out(buf_ref, c_ref, y_ref):
        for b in range(nblk):
            hcat = jnp.concatenate([buf_ref[:, b * ws:(b + 1) * ws], buf_ref[:, NS + b * ws:NS + (b + 1) * ws]],
                                   axis=1).astype(BF16)
            stage_ref[b] = jnp.dot(hcat, c_ref[b], preferred_element_type=F32)
        for s in range(n_seg):
            y_ref[s] = jnp.concatenate([stage_ref[c, pl.ds(s, KB, stride=n_seg), :] for c in range(nblk)], axis=1)

    project_in(interleaved(lambda s: uf_ref[s]), wbf_ref, bf_ref)
    scan(bf_ref, af_ref, hf_ref, False)
    if emit_out:
        project_out(bf_ref, cf_ref, yf_ref)
    project_in(interleaved(lambda s: ur_refs[s][...]), wbr_ref, br_ref)
    scan(br_ref, ar_ref, hr_ref, True)
    if emit_out:
        project_out(br_ref, cr_ref, yr_ref)
    else:
        ef_out_ref[...] = hf_ref[...]
        er_out_ref[...] = hr_ref[...]


def _s5_pass(z, col_block, pf, pr, ends, n_ctx):
    S, NZ = z.shape
    GW = pf[0].shape[1] * pf[0].shape[0]
    NS2 = pf[1].shape[1]
    seg_len = S // S5_SEGS
    steps = seg_len // S5_KB
    nblocks = S // S5_KB
    assert S % (S5_SEGS * S5_KB) == 0 and n_ctx % S5_KB == 0
    ctx_blocks = n_ctx // S5_KB
    emit_out = ends is not None
    z4 = z.reshape(S5_SEGS, steps, S5_KB, NZ)
    z3 = z.reshape(nblocks, S5_KB, NZ)

    def rev_spec(s):
        return pl.BlockSpec((None, S5_KB, GW),
                            lambda g: ((s * steps + steps - 1 - g + ctx_blocks) % nblocks, 0, col_block))

    const = lambda a: pl.BlockSpec(a.shape, lambda g: (0,) * a.ndim)
    in_specs = [pl.BlockSpec((S5_SEGS, None, S5_KB, GW), lambda g: (0, g, 0, col_block))]
    in_specs += [rev_spec(s) for s in range(S5_SEGS)]
    args = [z4] + [z3] * S5_SEGS
    weights = [pf[0], pr[0], pf[1], pr[1]]
    if emit_out:
        weights += [ends[0], ends[1], pf[2], pr[2], pf[3], pr[3]]
    in_specs += [const(a) for a in weights]
    args += weights
    scratch = [pltpu.VMEM((pf[0].shape[0], S5_SEGS * S5_KB, pf[0].shape[1]), F32),
               pltpu.VMEM((S5_SEGS * S5_KB, NS2), F32), pltpu.VMEM((S5_SEGS * S5_KB, NS2), F32),
               pltpu.VMEM((S5_SEGS, NS2), F32), pltpu.VMEM((S5_SEGS, NS2), F32)]
    if emit_out:
        yshape = jax.ShapeDtypeStruct((S5_SEGS, steps, S5_KB, GW), F32)
        out_shape = [yshape, yshape]
        out_specs = [pl.BlockSpec((S5_SEGS, None, S5_KB, GW), lambda g: (0, g, 0, 0)),
                     pl.BlockSpec((S5_SEGS, None, S5_KB, GW), lambda g: (0, steps - 1 - g, 0, 0))]
    else:
        eshape = jax.ShapeDtypeStruct((S5_SEGS, NS2), F32)
        out_shape = [eshape, eshape]
        out_specs = [pl.BlockSpec((S5_SEGS, NS2), lambda g: (0, 0))] * 2
    return pl.pallas_call(
        functools.partial(_s5_kernel, emit_out=emit_out),
        grid=(steps,),
        in_specs=in_specs,
        out_specs=out_specs,
        out_shape=out_shape,
        scratch_shapes=scratch,
        compiler_params=_cparams(1),
    )(*args)


def _s5_scan(z, col_block, pf, pr, n_ctx):
    S = z.shape[0]
    GW = pf[0].shape[1] * pf[0].shape[0]
    ends = _s5_pass(z, col_block, pf, pr, None, n_ctx)
    yf, yr = _s5_pass(z, col_block, pf, pr, ends, n_ctx)
    return yf.reshape(S, GW), yr.reshape(S, GW)


def _cast_kernel(x_ref, o_ref):
    o_ref[...] = x_ref[...].astype(o_ref.dtype)


def _cast_bf16(w, layer):
    _, K, N = w.shape
    tk = 512
    return pl.pallas_call(
        _cast_kernel,
        grid=(K // tk,),
        in_specs=[pl.BlockSpec((None, tk, N), lambda i: (layer, i, 0))],
        out_specs=pl.BlockSpec((tk, N), lambda i: (i, 0)),
        out_shape=jax.ShapeDtypeStruct((K, N), BF16),
        compiler_params=_cparams(1),
    )(w)


def _mixout_kernel(*refs, alpha, n_ctx_blocks, route, h_dtype, split):
    (pa_ref, pb_ref, pc_ref, yf_ref, yr_ref, u_ref, d_ref, gw_ref, gb_ref,
     w_ref, x_ref, c_ref, g1_ref, lg_ref, lb_ref, sh_ref, sc_ref) = refs[:17]
    if route:
        rw_ref, x1_ref, h_ref, idx_ref, gate_ref = refs[17:]
    else:
        x1_ref, h_ref = refs[17:]
    i = pl.program_id(0)
    is_ctx = i < n_ctx_blocks
    sy = yf_ref[...] + yr_ref[...] + d_ref[...] * u_ref[...].astype(F32)
    zz = jax.nn.gelu(sy)
    sgate = jnp.dot(zz.astype(BF16), gw_ref[...].astype(BF16), preferred_element_type=F32) + gb_ref[...]
    pd = (zz * jax.nn.sigmoid(sgate)).astype(BF16)
    GW = pa_ref.shape[1]
    mix = jnp.zeros(x_ref.shape, F32)
    for k, part in enumerate((pa_ref[...], pb_ref[...], pc_ref[...], pd)):
        mix = mix + jnp.dot(part.astype(BF16), w_ref[k * GW:(k + 1) * GW, :], preferred_element_type=F32)
    x = jnp.where(is_ctx, c_ref[...], x_ref[...]) if split else x_ref[...]
    y = alpha * x + _pick(g1_ref[...], is_ctx) * mix
    x1 = _layer_norm(y, lg_ref[...], lb_ref[...])
    x1_ref[...] = x1
    h = x1 * (1.0 + _pick(sc_ref[...], is_ctx)) + _pick(sh_ref[...], is_ctx)
    h_ref[...] = h.astype(h_dtype)
    if route:
        rw = rw_ref[...]
        h_hi = h.astype(BF16)
        h_lo = (h - h_hi.astype(F32)).astype(BF16)
        w_hi = rw.astype(BF16)
        w_lo = (rw - w_hi.astype(F32)).astype(BF16)
        logits = (jnp.dot(h_hi, w_hi, preferred_element_type=F32)
                  + (jnp.dot(h_lo, w_hi, preferred_element_type=F32)
                     + jnp.dot(h_hi, w_lo, preferred_element_type=F32)))
        n_exp = rw_ref.shape[1]
        lane = lax.broadcasted_iota(jnp.int32, logits.shape, 1)
        m1 = jnp.max(logits, axis=-1, keepdims=True)
        i1 = jnp.min(jnp.where(logits == m1, lane, n_exp), axis=-1, keepdims=True)
        rest = jnp.where(lane == i1, -jnp.inf, logits)
        m2 = jnp.max(rest, axis=-1, keepdims=True)
        i2 = jnp.min(jnp.where(rest == m2, lane, n_exp), axis=-1, keepdims=True)
        e2 = jnp.exp(m2 - m1)
        idx_ref[...] = jnp.concatenate([i1, i2], axis=1)
        gate_ref[...] = jnp.concatenate([1.0 / (1.0 + e2), e2 / (1.0 + e2)], axis=1)


def _mix_out(parts, s5, w_out_bf, stream, mod, ln_g, ln_b, layer, alpha, n_ctx_blocks, router_w):
    S, D = stream.rows, stream.width
    GW = parts[0].shape[1]
    L = ln_g.shape[0]
    nb = S // ROW_BLOCK
    route = router_w is not None
    h_dtype = F32 if route else BF16
    yf, yr, z, col_block, s5_d, glu_w, glu_b = s5
    part = pl.BlockSpec((ROW_BLOCK, GW), lambda i: (i, 0))
    rot = pl.BlockSpec((ROW_BLOCK, GW), lambda i: ((i + nb - n_ctx_blocks) % nb, 0))
    gvec = pl.BlockSpec((None, 1, GW), lambda i: (layer, 0, 0))
    rows = pl.BlockSpec((ROW_BLOCK, D), lambda i: (i, 0))
    vspec = pl.BlockSpec((None, 1, D), lambda i: (layer, 0, 0))
    in_specs = [part, part, part,
                part, rot, pl.BlockSpec((ROW_BLOCK, GW), lambda i: (i, col_block)), gvec,
                pl.BlockSpec((None, GW, GW), lambda i: (layer, 0, 0)), gvec,
                pl.BlockSpec((D, D), lambda i: (0, 0))] + stream.specs(1) + [
                _mod_spec(layer, 2, D, 1), vspec, vspec, _mod_spec(layer, 3, D, 1), _mod_spec(layer, 4, D, 1)]
    args = list(parts) + [yf, yr, z, s5_d.reshape(L, 1, GW), glu_w, glu_b.reshape(L, 1, GW),
                          w_out_bf] + stream.args() + [mod, ln_g.reshape(L, 1, D), ln_b.reshape(L, 1, D), mod, mod]
    out_specs = [rows, rows]
    out_shape = [jax.ShapeDtypeStruct((S, D), F32), jax.ShapeDtypeStruct((S, D), h_dtype)]
    if route:
        E = router_w.shape[-1]
        in_specs.append(pl.BlockSpec((D, E), lambda i: (0, 0)))
        args.append(router_w)
        out_specs += [pl.BlockSpec((ROW_BLOCK, TOP_K), lambda i: (i, 0))] * 2
        out_shape += [jax.ShapeDtypeStruct((S, TOP_K), jnp.int32), jax.ShapeDtypeStruct((S, TOP_K), F32)]
    return pl.pallas_call(
        functools.partial(_mixout_kernel, alpha=alpha, n_ctx_blocks=n_ctx_blocks, route=route, h_dtype=h_dtype,
                          split=stream.split),
        grid=(S // ROW_BLOCK,),
        in_specs=in_specs,
        out_specs=out_specs,
        out_shape=out_shape,
        compiler_params=_cparams(1),
    )(*args)


FFN_TM = 512
FFN_TF = 512
FFN_TN = 512


def _expert_changed(te_ref, i):
    prev = te_ref[jnp.maximum(i - 1, 0)]
    return jnp.logical_or(i == 0, te_ref[i] != prev)


def _ffn_up_kernel(te_ref, src_ref, live_ref, h_ref, w1_ref, w3_ref, o_ref, w1b_ref, w3b_ref):
    i = pl.program_id(1)
    half = h_ref.shape[0] // 2

    @pl.when(_expert_changed(te_ref, i))
    def _():
        w1b_ref[...] = w1_ref[...].astype(BF16)
        w3b_ref[...] = w3_ref[...].astype(BF16)

    def act(h):
        a = jnp.dot(h, w1b_ref[...], preferred_element_type=F32)
        b = jnp.dot(h, w3b_ref[...], preferred_element_type=F32)
        return (_silu(a) * b).astype(o_ref.dtype)

    @pl.when(live_ref[i] == 2)
    def _():
        o_ref[...] = act(h_ref[...])

    @pl.when(live_ref[i] == 1)
    def _():
        o_ref[0:half, :] = act(h_ref[0:half, :])
        o_ref[half:, :] = jnp.zeros((half, o_ref.shape[1]), o_ref.dtype)

    @pl.when(live_ref[i] == 0)
    def _():
        o_ref[...] = jnp.zeros_like(o_ref)


def _ffn_down_kernel(te_ref, src_ref, live_ref, g_ref, w2_ref, o_ref, w2b_ref):
    i = pl.program_id(1)

    @pl.when(_expert_changed(te_ref, i))
    def _():
        w2b_ref[...] = w2_ref[...].astype(BF16)

    @pl.when(live_ref[i] > 0)
    def _():
        o_ref[...] = jnp.dot(g_ref[...], w2b_ref[...], preferred_element_type=F32)

    @pl.when(live_ref[i] == 0)
    def _():
        o_ref[...] = jnp.zeros_like(o_ref)


def _swiglu_tiles(hs, w1, w3, w2, up, down):
    R, D = hs.shape
    _, _, F = w1.shape
    tf = FFN_TF if F % FFN_TF == 0 else F
    tn = FFN_TN
    tm_u, te_u, src_u, live_u = up
    tm_d, te_d, src_d, live_d = down
    assert R % tm_u == 0 and R % tm_d == 0 and F % tf == 0 and D % tn == 0
    g = pl.pallas_call(
        _ffn_up_kernel,
        grid_spec=pltpu.PrefetchScalarGridSpec(
            num_scalar_prefetch=3,
            grid=(F // tf, R // tm_u),
            in_specs=[pl.BlockSpec((tm_u, D), lambda j, i, te, src, lv: (src[i], 0)),
                      pl.BlockSpec((None, D, tf), lambda j, i, te, src, lv: (te[i], 0, j)),
                      pl.BlockSpec((None, D, tf), lambda j, i, te, src, lv: (te[i], 0, j))],
            out_specs=pl.BlockSpec((tm_u, tf), lambda j, i, te, src, lv: (i, j)),
            scratch_shapes=[pltpu.VMEM((D, tf), BF16), pltpu.VMEM((D, tf), BF16)]),
        out_shape=jax.ShapeDtypeStruct((R, F), BF16),
        compiler_params=_cparams(2),
    )(te_u, src_u, live_u, hs, w1, w3)
    return pl.pallas_call(
        _ffn_down_kernel,
        grid_spec=pltpu.PrefetchScalarGridSpec(
            num_scalar_prefetch=3,
            grid=(D // tn, R // tm_d),
            in_specs=[pl.BlockSpec((tm_d, F), lambda j, i, te, src, lv: (src[i], 0)),
                      pl.BlockSpec((None, F, tn), lambda j, i, te, src, lv: (te[i], 0, j))],
            out_specs=pl.BlockSpec((tm_d, tn), lambda j, i, te, src, lv: (i, j)),
            scratch_shapes=[pltpu.VMEM((F, tn), BF16)]),
        out_shape=jax.ShapeDtypeStruct((R, D), F32),
        compiler_params=_cparams(2),
    )(te_d, src_d, live_d, g, w2)


def _row_copy(src_ref, dst_ref, src_row, dst_row, sem):
    return pltpu.make_async_copy(src_ref.at[pl.ds(src_row, 1)], dst_ref.at[pl.ds(dst_row, 1)], sem)


DMA_UNROLL = 16


def _gather_kernel(tok_ref, live_ref, src_ref, o_ref, buf_ref, sem):
    i = pl.program_id(0)
    tm = buf_ref.shape[0]
    live = live_ref[i] > 0

    def issue(r, c):
        _row_copy(src_ref, buf_ref, tok_ref[i * tm + r], r, sem).start()
        return c

    def drain(r, c):
        _row_copy(src_ref, buf_ref, 0, r, sem).wait()
        return c

    @pl.when(live)
    def _():
        lax.fori_loop(0, tm, issue, 0, unroll=DMA_UNROLL)
        lax.fori_loop(0, tm, drain, 0, unroll=DMA_UNROLL)
        o_ref[...] = buf_ref[...].astype(o_ref.dtype)

    @pl.when(jnp.logical_not(live))
    def _():
        o_ref[...] = jnp.zeros_like(o_ref)


def _gather_rows(src, tok_of_slot, live, tm):
    R = tok_of_slot.shape[0]
    D = src.shape[1]
    return pl.pallas_call(
        _gather_kernel,
        grid_spec=pltpu.PrefetchScalarGridSpec(
            num_scalar_prefetch=2,
            grid=(R // tm,),
            in_specs=[pl.BlockSpec(memory_space=pl.ANY)],
            out_specs=pl.BlockSpec((tm, D), lambda i, tok, nu: (i, 0)),
            scratch_shapes=[pltpu.VMEM((tm, D), src.dtype), pltpu.SemaphoreType.DMA(())]),
        out_shape=jax.ShapeDtypeStruct((R, D), BF16),
        compiler_params=_cparams(1),
    )(tok_of_slot, live, src)


def _ln2_dense_kernel(x_ref, f_ref, g2_ref, lg_ref, lb_ref, o_ref, *, alpha, n_ctx_blocks, row_off):
    is_ctx = (pl.program_id(0) + row_off) < n_ctx_blocks
    y = alpha * x_ref[...] + _pick(g2_ref[...], is_ctx) * f_ref[...]
    o_ref[...] = _layer_norm(y, lg_ref[...], lb_ref[...])


def _ln2_moe_kernel(sa_ref, sb_ref, x_ref, y_ref, gate_ref, g2_ref, lg_ref, lb_ref, o_ref, bufa_ref, bufb_ref,
                    sem, *, alpha, n_ctx_blocks, row_off):
    i = pl.program_id(0)
    R = ROW_BLOCK
    base = (i + row_off) * R

    def issue(r, c):
        _row_copy(y_ref, bufa_ref, sa_ref[base + r], r, sem).start()
        _row_copy(y_ref, bufb_ref, sb_ref[base + r], r, sem).start()
        return c

    def drain(r, c):
        _row_copy(y_ref, bufa_ref, 0, r, sem).wait()
        _row_copy(y_ref, bufb_ref, 0, r, sem).wait()
        return c

    lax.fori_loop(0, R, issue, 0, unroll=DMA_UNROLL // 2)
    lax.fori_loop(0, R, drain, 0, unroll=DMA_UNROLL // 2)
    gate = gate_ref[...]
    f = gate[:, 0:1] * bufa_ref[...] + gate[:, 1:2] * bufb_ref[...]
    is_ctx = (i + row_off) < n_ctx_blocks
    y = alpha * x_ref[...] + _pick(g2_ref[...], is_ctx) * f
    o_ref[...] = _layer_norm(y, lg_ref[...], lb_ref[...])


def _ln2(x1, f, mod, ln_g, ln_b, layer, alpha, n_ctx_blocks, row_off, moe=None):
    S, D = x1.shape
    L = ln_g.shape[0]
    nb = S // ROW_BLOCK - row_off
    n_pre = 0 if moe is None else 2
    wrap = (lambda f_: (lambda i, *_: f_(i)))
    rows_in = pl.BlockSpec((ROW_BLOCK, D), wrap(lambda i: (i + row_off, 0)))
    rows_out = pl.BlockSpec((ROW_BLOCK, D), wrap(lambda i: (i, 0)))
    vspec = pl.BlockSpec((None, 1, D), wrap(lambda i: (layer, 0, 0)))
    mspec = pl.BlockSpec((None, SUBLANES, D), wrap(lambda i: (layer, 0, 5)))
    common = dict(alpha=alpha, n_ctx_blocks=n_ctx_blocks, row_off=row_off)
    lg, lb = ln_g.reshape(L, 1, D), ln_b.reshape(L, 1, D)
    out_shape = jax.ShapeDtypeStruct((nb * ROW_BLOCK, D), F32)
    if moe is None:
        return pl.pallas_call(
            functools.partial(_ln2_dense_kernel, **common),
            grid=(nb,),
            in_specs=[rows_in, rows_in, mspec, vspec, vspec],
            out_specs=rows_out,
            out_shape=out_shape,
            compiler_params=_cparams(1),
        )(x1, f, mod, lg, lb)
    slot_a, slot_b, gates = moe
    return pl.pallas_call(
        functools.partial(_ln2_moe_kernel, **common),
        grid_spec=pltpu.PrefetchScalarGridSpec(
            num_scalar_prefetch=n_pre,
            grid=(nb,),
            in_specs=[rows_in, pl.BlockSpec(memory_space=pl.ANY),
                      pl.BlockSpec((ROW_BLOCK, TOP_K), wrap(lambda i: (i + row_off, 0))),
                      mspec, vspec, vspec],
            out_specs=rows_out,
            scratch_shapes=[pltpu.VMEM((ROW_BLOCK, D), F32), pltpu.VMEM((ROW_BLOCK, D), F32),
                            pltpu.SemaphoreType.DMA(())]),
        out_shape=out_shape,
        compiler_params=_cparams(1),
    )(slot_a, slot_b, x1, f, gates, mod, lg, lb)


def _route_slots(idx, row0, n_experts, tm):
    S = idx.shape[0]
    n = S - row0
    e_flat = idx[row0:].reshape(-1)
    onehot = (e_flat[:, None] == jnp.arange(n_experts, dtype=jnp.int32)[None, :]).astype(jnp.int32)
    pos = jnp.sum((jnp.cumsum(onehot, axis=0) - 1) * onehot, axis=1)
    counts = jnp.sum(onehot, axis=0)
    big = 2 * tm
    padded = ((counts + big - 1) // big) * big
    ends = jnp.cumsum(padded)
    starts = ends - padded
    slot = starts[e_flat] + pos
    n_big = (TOP_K * n) // big + n_experts
    tok = jnp.repeat(jnp.arange(n, dtype=jnp.int32) + row0, TOP_K)
    tok_of_slot = jnp.full((n_big * big,), row0, jnp.int32).at[slot].set(tok)
    t = jnp.arange(2 * n_big, dtype=jnp.int32)
    te = jnp.minimum(jnp.searchsorted(ends, t * tm, side='right'), n_experts - 1).astype(jnp.int32)
    live = (t * tm < starts[te] + counts[te]).astype(jnp.int32)
    src = jnp.maximum(lax.cummax(jnp.where(live > 0, t, -1)), 0)
    down = (tm, te[src], src, live)
    live_big = live[0::2] + live[1::2]
    src_big = src[0::2] // 2
    up = (big, te[0::2][src_big], src_big, live_big)
    slot2 = slot.reshape(n, TOP_K).astype(jnp.int32)
    pad = jnp.zeros((row0,), jnp.int32)
    slot_a = jnp.concatenate([pad, slot2[:, 0]])
    slot_b = jnp.concatenate([pad, slot2[:, 1]])
    return tok_of_slot, up, down, slot_a, slot_b


def kernel(x, c, ctx, c_ctx, w_mod, b_mod, w_in, w_out, ln1_g, ln1_b, ln2_g, ln2_b, pool_w, pool_scale,
           diff_lambda, diff_subln_g, conv_dw, conv_db, conv_ln_g, conv_ln_b, conv_pw, s5_a_re, s5_a_im,
           s5_log_dt, s5_b_re, s5_b_im, s5_c_re, s5_c_im, s5_d, s5_glu_w, s5_glu_b, ffn_w1, ffn_w3, ffn_w2,
           router_w, moe_w1, moe_w3, moe_w2):
    B, T, D = x.shape
    Tc = ctx.shape[1]
    depth = w_mod.shape[0]
    assert B == 1 and Tc % ROW_BLOCK == 0 and T % ROW_BLOCK == 0
    GW = D // N_GROUPS
    n_ctx_blocks = Tc // ROW_BLOCK
    alpha = (2.0 * depth) ** 0.25

    cc = jnp.zeros((SUBLANES, D), F32).at[0].set(c[0]).at[1].set(c_ctx)
    mod = _modulation(cc, w_mod, b_mod)
    cos, sin = _rope_tables(T, Tc, LANES)
    stream = _Stream(x[0], ctx[0], n_ctx_blocks)

    POOL_B, Q_B, K_B, V_B, CONV_B, S5_B = 0, 1, 2, 3, 2, 6

    for l in range(depth):
        last = l == depth - 1
        lam_init = 0.8 - 0.6 * math.exp(-0.3 * l)
        z = _in_projection(stream, mod, w_in, l, n_ctx_blocks)

        pa = _pool_mixer(z, pool_w, pool_scale, l, n_ctx_blocks)

        qt, kk, vt, kn2 = _qkv_prep(z, cos, sin, GW, Q_B, K_B, V_B)
        kn = jnp.sqrt(jnp.max(kn2[n_ctx_blocks:, :, 0], axis=0)) * (1.0 + 2.0 ** -6)
        lv = diff_lambda[l].astype(F32)
        lam = jnp.exp(jnp.sum(lv[0] * lv[1])) - jnp.exp(jnp.sum(lv[2] * lv[3])) + lam_init
        lam_row = jnp.full((1, ATTN_TQ), lam, F32)
        g_col = (diff_subln_g[l].astype(F32) * (1.0 - lam_init)).reshape(GW, 1)
        pb = _diff_attention(qt, kk, vt, lam_row, g_col, kn, Tc)

        pcv = _conv_mixer(z, CONV_B, conv_dw, conv_db, conv_ln_g, conv_ln_b, conv_pw, l, n_ctx_blocks)

        seg_len = stream.rows // S5_SEGS
        pf = _s5_params(s5_a_re[l, 0], s5_a_im[l, 0], s5_log_dt[l, 0], s5_b_re[l, 0], s5_b_im[l, 0],
                        s5_c_re[l, 0], s5_c_im[l, 0], seg_len)
        pr = _s5_params(s5_a_re[l, 1], s5_a_im[l, 1], s5_log_dt[l, 1], s5_b_re[l, 1], s5_b_im[l, 1],
                        s5_c_re[l, 1], s5_c_im[l, 1], seg_len)
        yf, yr = _s5_scan(z, S5_B, pf, pr, Tc)
        s5 = (yf, yr, z, S5_B, s5_d, s5_glu_w, s5_glu_b)

        w_out_bf = _cast_bf16(w_out, l)
        row_off = n_ctx_blocks if last else 0
        if l % 2 == 0:
            x1, h = _mix_out((pa, pb, pcv), s5, w_out_bf, stream, mod, ln1_g, ln1_b, l, alpha, n_ctx_blocks, None)
            S = stream.rows
            def dense_tiles(sizes, live):
                tm = next(t for t in sizes if S % t == 0)
                nt = S // tm
                return (tm, jnp.zeros((nt,), jnp.int32), jnp.arange(nt, dtype=jnp.int32),
                        jnp.full((nt,), live, jnp.int32))
            f = _swiglu_tiles(h, ffn_w1[l // 2][None], ffn_w3[l // 2][None], ffn_w2[l // 2][None],
                              dense_tiles((1408, 768, 512, ROW_BLOCK), 2), dense_tiles((768, 512, ROW_BLOCK), 1))
            xs_new = _ln2(x1, f, mod, ln2_g, ln2_b, l, alpha, n_ctx_blocks, row_off)
        else:
            x1, h, idx, gates = _mix_out((pa, pb, pcv), s5, w_out_bf, stream, mod, ln1_g, ln1_b, l, alpha,
                                         n_ctx_blocks, router_w[l // 2])
            n_exp = router_w.shape[-1]
            row0 = row_off * ROW_BLOCK
            tok_of_slot, up, down, slot_a, slot_b = _route_slots(idx, row0, n_exp, FFN_TM)
            hs = _gather_rows(h, tok_of_slot, down[3], FFN_TM)
            y = _swiglu_tiles(hs, moe_w1[l // 2], moe_w3[l // 2], moe_w2[l // 2], up, down)
            xs_new = _ln2(x1, y, mod, ln2_g, ln2_b, l, alpha, n_ctx_blocks, row_off, moe=(slot_a, slot_b, gates))
        stream = _Stream(xs_new, xs_new, n_ctx_blocks)
    return xs_new[None]
```

```python
import functools
import math

import numpy as np
import jax
import jax.numpy as jnp
from jax import lax
from jax.experimental import pallas as pl
from jax.experimental.pallas import tpu as pltpu

F32 = jnp.float32
BF16 = jnp.bfloat16

GRID_W = 64
N_GROUPS = 4
POOL_WINDOWS = (2, 4, 8, 16)
DIFF_HEADS = 8
DIFF_QK = 32
CONV_WIDTH = 31
S5_P = 16
S5_N = 64
TOP_K = 2
ROPE_BASE = 10000.0
LN_EPS = 1e-5

LANES = 128
SUBLANES = 8
ROW_BLOCK = 256
VMEM_LIMIT = 56 * 1024 * 1024


def _cparams(n_axes, vmem=VMEM_LIMIT):
    return pltpu.CompilerParams(dimension_semantics=("arbitrary",) * n_axes, vmem_limit_bytes=vmem)


def _layer_norm(y, g, b):
    mu = jnp.mean(y, -1, keepdims=True)
    yc = y - mu
    var = jnp.mean(yc * yc, -1, keepdims=True)
    return yc * lax.rsqrt(var + LN_EPS) * g + b


def _silu(x):
    return x * jax.nn.sigmoid(x)


def _mod_kernel(cc_ref, w_ref, b_ref, o_ref):
    a = _silu(cc_ref[...])
    o_ref[...] = jnp.dot(a.astype(BF16), w_ref[...].astype(BF16), preferred_element_type=F32) + b_ref[...]


def _modulation(cc, w_mod, b_mod):
    L, D, N = w_mod.shape
    tn = 1536
    assert N % tn == 0
    return pl.pallas_call(
        _mod_kernel,
        grid=(L, N // tn),
        in_specs=[pl.BlockSpec((SUBLANES, D), lambda l, j: (0, 0)),
                  pl.BlockSpec((None, D, tn), lambda l, j: (l, 0, j)),
                  pl.BlockSpec((None, 1, tn), lambda l, j: (l, 0, j))],
        out_specs=pl.BlockSpec((None, SUBLANES, tn), lambda l, j: (l, 0, j)),
        out_shape=jax.ShapeDtypeStruct((L, SUBLANES, N), F32),
        compiler_params=_cparams(2),
    )(cc, w_mod, b_mod.reshape(L, 1, N))


def _mod_spec(layer, chunk, D, n_grid_axes):
    if n_grid_axes == 1:
        return pl.BlockSpec((None, SUBLANES, D), lambda i: (layer, 0, chunk))
    return pl.BlockSpec((None, SUBLANES, D), lambda j, i: (layer, 0, chunk))


def _pick(m, is_ctx):
    return jnp.where(is_ctx, m[1:2, :], m[0:1, :])


class _Stream:
    def __init__(self, latent, context, n_ctx_blocks):
        self.latent = latent
        self.context = context
        self.n_ctx_blocks = n_ctx_blocks
        self.off = n_ctx_blocks if latent is context else 0
        self.rows = (latent.shape[0] // ROW_BLOCK - self.off + n_ctx_blocks) * ROW_BLOCK
        self.width = latent.shape[1]

    def specs(self, n_grid_axes):
        ncb, off, D = self.n_ctx_blocks, self.off, self.width
        lat = lambda i: (jnp.maximum(i - ncb, 0) + off, 0)
        ctx = lambda i: (jnp.minimum(i, ncb - 1), 0)
        if n_grid_axes == 1:
            return [pl.BlockSpec((ROW_BLOCK, D), lat), pl.BlockSpec((ROW_BLOCK, D), ctx)]
        return [pl.BlockSpec((ROW_BLOCK, D), lambda j, i: lat(i)), pl.BlockSpec((ROW_BLOCK, D), lambda j, i: ctx(i))]

    def args(self):
        return [self.latent, self.context]


def _inproj_kernel(x_ref, c_ref, sh_ref, sc_ref, w_ref, o_ref, wb_ref, *, n_ctx_blocks):
    i = pl.program_id(1)

    @pl.when(i == 0)
    def _():
        wb_ref[...] = w_ref[...].astype(BF16)

    is_ctx = i < n_ctx_blocks
    x = jnp.where(is_ctx, c_ref[...], x_ref[...])
    h = x * (1.0 + _pick(sc_ref[...], is_ctx)) + _pick(sh_ref[...], is_ctx)
    o_ref[...] = jnp.dot(h.astype(BF16), wb_ref[...], preferred_element_type=F32).astype(o_ref.dtype)


def _in_projection(stream, mod, w_in, layer, n_ctx_blocks):
    S, D = stream.rows, stream.width
    N = w_in.shape[-1]
    tn = 1792
    assert N % tn == 0 and S % ROW_BLOCK == 0
    return pl.pallas_call(
        functools.partial(_inproj_kernel, n_ctx_blocks=n_ctx_blocks),
        grid=(N // tn, S // ROW_BLOCK),
        in_specs=stream.specs(2) + [
                  _mod_spec(layer, 0, D, 2),
                  _mod_spec(layer, 1, D, 2),
                  pl.BlockSpec((None, D, tn), lambda j, i: (layer, 0, j))],
        out_specs=pl.BlockSpec((ROW_BLOCK, tn), lambda j, i: (i, j)),
        out_shape=jax.ShapeDtypeStruct((S, N), BF16),
        scratch_shapes=[pltpu.VMEM((D, tn), BF16)],
        compiler_params=_cparams(2),
    )(*stream.args(), mod, mod, w_in)


def _seq_edges(i, n_blocks, n_ctx_blocks):
    prev_ok = jnp.logical_and(i != 0, i != n_ctx_blocks)
    next_ok = jnp.logical_and(i != n_ctx_blocks - 1, i != n_blocks - 1)
    return prev_ok, next_ok


def _pool_kernel(p_ref, c_ref, n_ref, w_ref, scale_ref, o_ref, ext_ref, *, n_blocks, n_ctx_blocks):
    i = pl.program_id(0)
    R = ROW_BLOCK
    halo = SUBLANES
    prev_ok, next_ok = _seq_edges(i, n_blocks, n_ctx_blocks)
    cur = c_ref[...].astype(F32)
    pack = 2 * SUBLANES
    ext_ref[0:halo, :] = jnp.where(prev_ok, p_ref[R - pack:R, :].astype(F32)[pack - halo:], 0.0)
    ext_ref[halo:halo + R, :] = cur
    ext_ref[halo + R:halo + R + halo, :] = jnp.where(next_ok, n_ref[0:pack, :].astype(F32)[:halo], 0.0)
    rloc = lax.broadcasted_iota(jnp.int32, (R, 1), 0)
    gw = cur.shape[1] // len(POOL_WINDOWS)
    ext = ext_ref[...]
    n_ext = ext.shape[0]
    accs = [jnp.zeros((R, gw), F32) for _ in POOL_WINDOWS]
    for r in range(SUBLANES):
        rolled = ext if r == 0 else pltpu.roll(ext, n_ext - r, 0)
        for g, w in enumerate(POOL_WINDOWS):
            for d in range(-(w // 2), w - w // 2):
                m, rr = divmod(halo + d, SUBLANES)
                if rr == r:
                    accs[g] = accs[g] + rolled[m * SUBLANES:m * SUBLANES + R, g * gw:(g + 1) * gw]
    outs = []
    for g, w in enumerate(POOL_WINDOWS):
        below = jnp.where(prev_ok, 0, jnp.maximum(w // 2 - rloc, 0))
        above = jnp.where(next_ok, 0, jnp.maximum(rloc + (w - w // 2) - R, 0))
        cnt = (w - below - above).astype(F32)
        diff = accs[g] / cnt - cur[:, g * gw:(g + 1) * gw]
        outs.append(jnp.dot(diff.astype(BF16), w_ref[g].astype(BF16), preferred_element_type=F32))
    o_ref[...] = (jnp.concatenate(outs, axis=-1) * scale_ref[...]).astype(o_ref.dtype)


def _pool_mixer(z, pool_w, pool_scale, layer, n_ctx_blocks):
    S = z.shape[0]
    nb = S // ROW_BLOCK
    GW = pool_scale.shape[-1]
    G, gw = pool_w.shape[1], pool_w.shape[2]
    return pl.pallas_call(
        functools.partial(_pool_kernel, n_blocks=nb, n_ctx_blocks=n_ctx_blocks),
        grid=(nb,),
        in_specs=[pl.BlockSpec((ROW_BLOCK, GW), lambda i: (jnp.maximum(i - 1, 0), 0)),
                  pl.BlockSpec((ROW_BLOCK, GW), lambda i: (i, 0)),
                  pl.BlockSpec((ROW_BLOCK, GW), lambda i: (jnp.minimum(i + 1, nb - 1), 0)),
                  pl.BlockSpec((None, G, gw, gw), lambda i: (layer, 0, 0, 0)),
                  pl.BlockSpec((None, 1, GW), lambda i: (layer, 0, 0))],
        out_specs=pl.BlockSpec((ROW_BLOCK, GW), lambda i: (i, 0)),
        out_shape=jax.ShapeDtypeStruct((S, GW), BF16),
        scratch_shapes=[pltpu.VMEM((ROW_BLOCK + 2 * SUBLANES, GW), F32)],
        compiler_params=_cparams(1),
    )(z, z, z, pool_w, pool_scale.reshape(pool_scale.shape[0], 1, GW))


CONV_HALO = 16


def _conv_kernel(p_ref, c_ref, n_ref, dw_ref, db_ref, g_ref, b_ref, pw_ref, o_ref, ext_ref,
                 *, n_blocks, n_ctx_blocks):
    i = pl.program_id(0)
    R = ROW_BLOCK
    H = CONV_HALO
    GW = o_ref.shape[1]
    prev_ok, next_ok = _seq_edges(i, n_blocks, n_ctx_blocks)

    def glu(u):
        u = u.astype(F32)
        return u[:, :GW] * jax.nn.sigmoid(u[:, GW:])

    ext_ref[0:H, :] = jnp.where(prev_ok, glu(p_ref[R - H:R, :]), 0.0)
    ext_ref[H:H + R, :] = glu(c_ref[...])
    ext_ref[H + R:H + R + H, :] = jnp.where(next_ok, glu(n_ref[0:H, :]), 0.0)
    off = H - CONV_WIDTH // 2
    ext = ext_ref[...]
    n_ext = ext.shape[0]
    acc = jnp.zeros((R, GW), F32)
    for r in range(SUBLANES):
        rolled = ext if r == 0 else pltpu.roll(ext, n_ext - r, 0)
        for j in range(CONV_WIDTH):
            m, rr = divmod(off + j, SUBLANES)
            if rr == r:
                acc = acc + rolled[m * SUBLANES:m * SUBLANES + R, :] * dw_ref[j:j + 1, :]
    y = _layer_norm(acc + db_ref[...], g_ref[...], b_ref[...])
    o_ref[...] = jnp.dot(_silu(y).astype(BF16), pw_ref[...].astype(BF16),
                         preferred_element_type=F32).astype(o_ref.dtype)


def _conv_mixer(z, col_block, conv_dw, conv_db, conv_ln_g, conv_ln_b, conv_pw, layer, n_ctx_blocks):
    S = z.shape[0]
    nb = S // ROW_BLOCK
    GW = conv_db.shape[-1]
    L = conv_db.shape[0]
    vec = lambda a: a.reshape(L, 1, GW)
    vspec = pl.BlockSpec((None, 1, GW), lambda i: (layer, 0, 0))
    return pl.pallas_call(
        functools.partial(_conv_kernel, n_blocks=nb, n_ctx_blocks=n_ctx_blocks),
        grid=(nb,),
        in_specs=[pl.BlockSpec((ROW_BLOCK, 2 * GW), lambda i: (jnp.maximum(i - 1, 0), col_block)),
                  pl.BlockSpec((ROW_BLOCK, 2 * GW), lambda i: (i, col_block)),
                  pl.BlockSpec((ROW_BLOCK, 2 * GW), lambda i: (jnp.minimum(i + 1, nb - 1), col_block)),
                  pl.BlockSpec((None, CONV_WIDTH, GW), lambda i: (layer, 0, 0)),
                  vspec, vspec, vspec,
                  pl.BlockSpec((None, GW, GW), lambda i: (layer, 0, 0))],
        out_specs=pl.BlockSpec((ROW_BLOCK, GW), lambda i: (i, 0)),
        out_shape=jax.ShapeDtypeStruct((S, GW), BF16),
        scratch_shapes=[pltpu.VMEM((ROW_BLOCK + 2 * CONV_HALO, GW), F32)],
        compiler_params=_cparams(1),
    )(z, z, z, conv_dw, vec(conv_db), vec(conv_ln_g), vec(conv_ln_b), conv_pw)


def _rope_tables(T, Tc, width):
    ax = DIFF_QK // 2
    inv = ROPE_BASE ** (-jnp.arange(0, ax, 2, dtype=F32) / ax)
    t = jnp.arange(T)
    row = (t // GRID_W).astype(F32)
    col = (t % GRID_W).astype(F32)
    ang = jnp.stack([row[:, None] * inv, col[:, None] * inv], axis=1)
    cos = jnp.cos(ang)[:, :, None, :]
    sin = jnp.sin(ang)[:, :, None, :]
    cos = jnp.broadcast_to(cos, (T, 2, 2, ax // 2)).reshape(T, DIFF_QK)
    sin = jnp.concatenate([-sin, sin], axis=2).reshape(T, DIFF_QK)
    reps = width // DIFF_QK
    cos = jnp.concatenate([jnp.ones((Tc, DIFF_QK), F32), cos], axis=0)
    sin = jnp.concatenate([jnp.zeros((Tc, DIFF_QK), F32), sin], axis=0)
    return jnp.tile(cos, (1, reps)), jnp.tile(sin, (1, reps))


def _qkv_prep_kernel(q_ref, k_ref, v_ref, cos_ref, sin_ref, qo_ref, ko_ref, vo_ref, kn_ref):
    W = q_ref.shape[1]
    half = DIFF_QK // 4
    lane = lax.broadcasted_iota(jnp.int32, (1, W), 1)
    first = (lane % (2 * half)) < half
    cos = jnp.concatenate([cos_ref[...]] * (W // LANES), axis=1)
    sin = jnp.concatenate([sin_ref[...]] * (W // LANES), axis=1)

    def rope(x):
        partner = jnp.where(first, pltpu.roll(x, W - half, 1), pltpu.roll(x, half, 1))
        return x * cos + partner * sin

    qo_ref[...] = (rope(q_ref[...].astype(F32)) * (DIFF_QK ** -0.5 * math.log2(math.e))).T.astype(BF16)
    kb = rope(k_ref[...].astype(F32)).astype(BF16)
    ko_ref[...] = kb
    ksq = kb.astype(F32).T
    ksq = ksq * ksq
    for grp in range(W // DIFF_QK):
        n2 = jnp.sum(ksq[grp * DIFF_QK:(grp + 1) * DIFF_QK, :], axis=0, keepdims=True)
        kn_ref[grp:grp + 1, :] = jnp.broadcast_to(jnp.max(n2, axis=1, keepdims=True), (1, LANES))
    vt = v_ref[...].astype(F32).T.astype(BF16)
    dv = LANES // 2
    ones = jnp.ones((ATTN_VROWS - dv, vt.shape[1]), BF16)
    for h in range(W // dv):
        vo_ref[h * ATTN_VROWS:h * ATTN_VROWS + dv, :] = vt[h * dv:(h + 1) * dv, :]
        vo_ref[h * ATTN_VROWS + dv:(h + 1) * ATTN_VROWS, :] = ones


def _qkv_prep(z, cos, sin, W, q_blk, k_blk, v_blk):
    S = z.shape[0]
    nb = S // ROW_BLOCK
    row = lambda c: pl.BlockSpec((ROW_BLOCK, W), lambda i: (i, c))
    tab = pl.BlockSpec((ROW_BLOCK, LANES), lambda i: (i, 0))
    return pl.pallas_call(
        _qkv_prep_kernel,
        grid=(nb,),
        in_specs=[row(q_blk), row(k_blk), row(v_blk), tab, tab],
        out_specs=[pl.BlockSpec((W, ROW_BLOCK), lambda i: (0, i)), row(0),
                   pl.BlockSpec((DIFF_HEADS * ATTN_VROWS, ROW_BLOCK), lambda i: (0, i)),
                   pl.BlockSpec((None, W // DIFF_QK, LANES), lambda i: (i, 0, 0))],
        out_shape=[jax.ShapeDtypeStruct((W, S), BF16), jax.ShapeDtypeStruct((S, W), BF16),
                   jax.ShapeDtypeStruct((DIFF_HEADS * ATTN_VROWS, S), BF16),
                   jax.ShapeDtypeStruct((nb, W // DIFF_QK, LANES), F32)],
        compiler_params=_cparams(1),
    )(z, z, z, cos, sin)


ATTN_TQ = 256
ATTN_VROWS = LANES // 2 + 2 * SUBLANES
ATTN_MARGIN = 64.0
ATTN_TK = (4096, 2048, 1024)


def _attn_kernel(qt_ref, k_ref, vt_ref, lam_ref, g_ref, kn_ref, o_ref, qq_ref, m_ref, acc_ref,
                 *, n_ctx, n_ctx_blocks, n_lat_chunks, tk):
    i = pl.program_id(1)
    tq = ATTN_TQ
    dv = LANES // 2
    qt = qt_ref[...]
    feat = lax.broadcasted_iota(jnp.int32, (LANES, 1), 0)
    zero = jnp.zeros_like(qt)
    for hh in range(2):
        for comp in range(2):
            lo = hh * dv + comp * DIFF_QK
            keep = jnp.logical_and(feat >= lo, feat < lo + DIFF_QK)
            qq_ref[hh, :, comp * tq:(comp + 1) * tq] = jnp.where(keep, qt, zero)

    def attend(start, size, mode):
        kk = k_ref[pl.ds(start, size), :]
        for hh in range(2):
            s = jnp.dot(kk, qq_ref[hh], preferred_element_type=F32)
            vv = vt_ref[hh * ATTN_VROWS:(hh + 1) * ATTN_VROWS, pl.ds(start, size)]
            mx = jnp.max(s, axis=0, keepdims=True)
            if mode == "first":
                m_ref[hh] = mx
                p = jnp.exp2(s - mx)
                acc_ref[hh] = jnp.dot(vv, p.astype(BF16), preferred_element_type=F32)
            elif mode == "exact":
                m_old = m_ref[hh]
                m_new = jnp.maximum(m_old, mx)
                m_ref[hh] = m_new
                p = jnp.exp2(s - m_new)
                acc_ref[hh] = (jnp.exp2(m_old - m_new) * acc_ref[hh]
                               + jnp.dot(vv, p.astype(BF16), preferred_element_type=F32))
            else:
                m_old = m_ref[hh]
                p = jnp.exp2(s - m_old)
                m_new = jnp.maximum(m_old, mx)
                m_ref[hh] = m_new
                acc_ref[hh] = (jnp.exp2(m_old - m_new)
                               * (acc_ref[hh] + jnp.dot(vv, p.astype(BF16), preferred_element_type=F32)))

    attend(0, n_ctx, "first")
    n_steps = jnp.where(i < n_ctx_blocks, 0, n_lat_chunks)

    pair = pl.program_id(0)
    col = lax.broadcasted_iota(jnp.int32, (1, 2 * tq), 1)
    excess = jnp.full((1, 2 * tq), -jnp.inf, F32)
    for hh in range(2):
        qf = qq_ref[hh].astype(F32)
        qn = jnp.sqrt(jnp.sum(qf * qf, axis=0, keepdims=True))
        grp = (2 * pair + hh) * 2
        kn = jnp.where(col < tq, kn_ref[grp], kn_ref[grp + 1])
        excess = jnp.maximum(excess, qn * kn - m_ref[hh])
    safe = jnp.max(excess) < ATTN_MARGIN

    def loop(mode):
        def body(c, carry):
            attend(pl.multiple_of(n_ctx + c * tk, LANES), tk, mode)
            return carry
        lax.fori_loop(0, n_steps, body, 0)

    @pl.when(safe)
    def _():
        loop("deferred")

    @pl.when(jnp.logical_not(safe))
    def _():
        loop("exact")

    lam = lam_ref[...]
    outs = []
    for hh in range(2):
        acc = acc_ref[hh]
        ratio = acc[:dv] / acc[dv:dv + 1]
        o = ratio[:, :tq] - lam * ratio[:, tq:]
        r = lax.rsqrt(jnp.sum(o * o, axis=0, keepdims=True) / dv + LN_EPS)
        outs.append(o * r)
    o_ref[...] = (jnp.concatenate(outs, axis=0) * g_ref[...]).T.astype(o_ref.dtype)


def _diff_attention(qt, k, vt, lam, g, kn, n_ctx):
    W, S = qt.shape
    assert S % ATTN_TQ == 0 and n_ctx % ATTN_TQ == 0
    tk = next(t for t in ATTN_TK if (S - n_ctx) % t == 0)
    nq = S // ATTN_TQ
    return pl.pallas_call(
        functools.partial(_attn_kernel, n_ctx=n_ctx, n_ctx_blocks=n_ctx // ATTN_TQ,
                          n_lat_chunks=(S - n_ctx) // tk, tk=tk),
        grid=(W // LANES, nq),
        in_specs=[pl.BlockSpec((LANES, ATTN_TQ), lambda p, i: (p, i)),
                  pl.BlockSpec((S, LANES), lambda p, i: (0, p)),
                  pl.BlockSpec((2 * ATTN_VROWS, S), lambda p, i: (p, 0)),
                  pl.BlockSpec((1, ATTN_TQ), lambda p, i: (0, 0)),
                  pl.BlockSpec((LANES, 1), lambda p, i: (p, 0)),
                  pl.BlockSpec(memory_space=pltpu.SMEM)],
        out_specs=pl.BlockSpec((ATTN_TQ, LANES), lambda p, i: (i, p)),
        out_shape=jax.ShapeDtypeStruct((S, W), BF16),
        scratch_shapes=[pltpu.VMEM((2, LANES, 2 * ATTN_TQ), BF16),
                        pltpu.VMEM((2, 1, 2 * ATTN_TQ), F32),
                        pltpu.VMEM((2, ATTN_VROWS, 2 * ATTN_TQ), F32)],
        compiler_params=_cparams(2),
    )(qt, k, vt, lam, g, kn)


S5_SEGS = SUBLANES
S5_KB = 32
S5_GB = 8


def _s5_params(a_re, a_im, log_dt, b_re, b_im, c_re, c_im, seg_len):
    G, N = a_re.shape
    P = b_re.shape[-1]
    nblk = G // S5_GB
    a_re, a_im = a_re.astype(F32), a_im.astype(F32)
    dt = jnp.exp(log_dt.astype(F32))[:, None]
    lr, li = dt * a_re, dt * a_im
    mag = jnp.exp(lr)
    ar, ai = mag * jnp.cos(li), mag * jnp.sin(li)
    den = a_re * a_re + a_im * a_im
    qr = ((ar - 1.0) * a_re + ai * a_im) / den
    qi = (ai * a_re - (ar - 1.0) * a_im) / den
    b_re, b_im = b_re.astype(F32), b_im.astype(F32)
    br = qr[..., None] * b_re - qi[..., None] * b_im
    bi = qr[..., None] * b_im + qi[..., None] * b_re
    mag_l = jnp.exp(seg_len * lr)
    alr, ali = mag_l * jnp.cos(seg_len * li), mag_l * jnp.sin(seg_len * li)
    eye = jnp.eye(S5_GB, dtype=F32)
    wb = lambda m: jnp.einsum('gh,bgnp->bgphn', eye, m.reshape(nblk, S5_GB, N, P)).reshape(
        nblk, S5_GB * P, S5_GB * N)
    w_in = jnp.concatenate([wb(br), wb(bi)], axis=2)
    cm = lambda m: jnp.einsum('gh,bgpn->bhngp', eye, m.astype(F32).reshape(nblk, S5_GB, P, N)).reshape(
        nblk, S5_GB * N, S5_GB * P)
    w_out = jnp.concatenate([cm(c_re), -cm(c_im)], axis=1)
    row = lambda r, i: jnp.concatenate([r.reshape(1, G * N), i.reshape(1, G * N)], axis=1)
    coef = jnp.broadcast_to(row(ar, ai), (S5_SEGS, 2 * G * N))
    return w_in.astype(BF16), coef, row(alr, ali), w_out.astype(BF16)


def _s5_kernel(*refs, emit_out):
    n_seg = S5_SEGS
    uf_ref = refs[0]
    ur_refs = refs[1:1 + n_seg]
    rest = refs[1 + n_seg:]
    if emit_out:
        (wbf_ref, wbr_ref, af_ref, ar_ref, ef_ref, er_ref, alf_ref, alr_ref, cf_ref, cr_ref,
         yf_ref, yr_ref, stage_ref, bf_ref, br_ref, hf_ref, hr_ref) = rest
    else:
        (wbf_ref, wbr_ref, af_ref, ar_ref, ef_out_ref, er_out_ref,
         stage_ref, bf_ref, br_ref, hf_ref, hr_ref) = rest
    g = pl.program_id(0)
    KB = S5_KB
    R = n_seg * KB
    NS = af_ref.shape[1] // 2
    nblk = wbf_ref.shape[0]
    wi = wbf_ref.shape[1]
    ws = wbf_ref.shape[2] // 2

    def cmul_add(a_row, h, add):
        are, aim = a_row[:, :NS], a_row[:, NS:]
        hre, him = h[:, :NS], h[:, NS:]
        return jnp.concatenate([are * hre - aim * him + add[:, :NS], are * him + aim * hre + add[:, NS:]], axis=1)

    @pl.when(g == 0)
    def _():
        if emit_out:
            def chain(e_ref, al_ref, order):
                al = al_ref[...]
                c = jnp.zeros((1, 2 * NS), F32)
                rows = [None] * n_seg
                for s in order:
                    rows[s] = c
                    c = cmul_add(al, c, e_ref[s:s + 1, :])
                return jnp.concatenate(rows, axis=0)
            hf_ref[...] = chain(ef_ref, alf_ref, range(n_seg))
            hr_ref[...] = chain(er_ref, alr_ref, range(n_seg - 1, -1, -1))
        else:
            hf_ref[...] = jnp.zeros_like(hf_ref)
            hr_ref[...] = jnp.zeros_like(hr_ref)

    def interleaved(load_seg):
        for s in range(n_seg):
            blk = load_seg(s).astype(F32)
            for c in range(nblk):
                stage_ref[c, s * KB:(s + 1) * KB, :] = blk[:, c * wi:(c + 1) * wi]
        rows = [jnp.concatenate([stage_ref[c, pl.ds(kk, n_seg, stride=KB), :] for c in range(nblk)], axis=1)
                for kk in range(KB)]
        return jnp.concatenate(rows, axis=0).astype(BF16)

    def project_in(u, w_ref, buf_ref):
        for b in range(nblk):
            res = jnp.dot(u[:, b * wi:(b + 1) * wi], w_ref[b], preferred_element_type=F32)
            buf_ref[:, b * ws:(b + 1) * ws] = res[:, :ws]
            buf_ref[:, NS + b * ws:NS + (b + 1) * ws] = res[:, ws:]

    def scan(buf_ref, a_ref, h_ref, reverse):
        half = NS // 2
        for c in range(2):
            cre = slice(c * half, (c + 1) * half)
            cim = slice(NS + c * half, NS + (c + 1) * half)
            are, aim = a_ref[:, cre], a_ref[:, cim]

            def step(t, carry, cre=cre, cim=cim, are=are, aim=aim):
                hre, him = carry
                kk = (KB - 1 - t) if reverse else t
                r0 = pl.multiple_of(kk * n_seg, n_seg)
                nre = are * hre - aim * him + buf_ref[pl.ds(r0, n_seg), cre]
                nim = are * him + aim * hre + buf_ref[pl.ds(r0, n_seg), cim]
                if emit_out:
                    buf_ref[pl.ds(r0, n_seg), cre] = nre
                    buf_ref[pl.ds(r0, n_seg), cim] = nim
                return nre, nim

            hre, him = lax.fori_loop(0, KB, step, (h_ref[:, cre], h_ref[:, cim]))
            h_ref[:, cre] = hre
            h_ref[:, cim] = him

    def project_out(buf_ref, c_ref, y_ref):
        for b in range(nblk):
            hcat = jnp.concatenate([buf_ref[:, b * ws:(b + 1) * ws], buf_ref[:, NS + b * ws:NS + (b + 1) * ws]],
                                   axis=1).astype(BF16)
            stage_ref[b] = jnp.dot(hcat, c_ref[b], preferred_element_type=F32)
        for s in range(n_seg):
            y_ref[s] = jnp.concatenate([stage_ref[c, pl.ds(s, KB, stride=n_seg), :] for c in range(nblk)], axis=1)

    project_in(interleaved(lambda s: uf_ref[s]), wbf_ref, bf_ref)
    scan(bf_ref, af_ref, hf_ref, False)
    if emit_out:
        project_out(bf_ref, cf_ref, yf_ref)
    project_in(interleaved(lambda s: ur_refs[s][...]), wbr_ref, br_ref)
    scan(br_ref, ar_ref, hr_ref, True)
    if emit_out:
        project_out(br_ref, cr_ref, yr_ref)
    else:
        ef_out_ref[...] = hf_ref[...]
        er_out_ref[...] = hr_ref[...]


def _s5_pass(z, col_block, pf, pr, ends, n_ctx):
    S, NZ = z.shape
    GW = pf[0].shape[1] * pf[0].shape[0]
    NS2 = pf[1].shape[1]
    seg_len = S // S5_SEGS
    steps = seg_len // S5_KB
    nblocks = S // S5_KB
    assert S % (S5_SEGS * S5_KB) == 0 and n_ctx % S5_KB == 0
    ctx_blocks = n_ctx // S5_KB
    emit_out = ends is not None
    z4 = z.reshape(S5_SEGS, steps, S5_KB, NZ)
    z3 = z.reshape(nblocks, S5_KB, NZ)

    def rev_spec(s):
        return pl.BlockSpec((None, S5_KB, GW),
                            lambda g: ((s * steps + steps - 1 - g + ctx_blocks) % nblocks, 0, col_block))

    const = lambda a: pl.BlockSpec(a.shape, lambda g: (0,) * a.ndim)
    in_specs = [pl.BlockSpec((S5_SEGS, None, S5_KB, GW), lambda g: (0, g, 0, col_block))]
    in_specs += [rev_spec(s) for s in range(S5_SEGS)]
    args = [z4] + [z3] * S5_SEGS
    weights = [pf[0], pr[0], pf[1], pr[1]]
    if emit_out:
        weights += [ends[0], ends[1], pf[2], pr[2], pf[3], pr[3]]
    in_specs += [const(a) for a in weights]
    args += weights
    scratch = [pltpu.VMEM((pf[0].shape[0], S5_SEGS * S5_KB, pf[0].shape[1]), F32),
               pltpu.VMEM((S5_SEGS * S5_KB, NS2), F32), pltpu.VMEM((S5_SEGS * S5_KB, NS2), F32),
               pltpu.VMEM((S5_SEGS, NS2), F32), pltpu.VMEM((S5_SEGS, NS2), F32)]
    if emit_out:
        yshape = jax.ShapeDtypeStruct((S5_SEGS, steps, S5_KB, GW), F32)
        out_shape = [yshape, yshape]
        out_specs = [pl.BlockSpec((S5_SEGS, None, S5_KB, GW), lambda g: (0, g, 0, 0)),
                     pl.BlockSpec((S5_SEGS, None, S5_KB, GW), lambda g: (0, steps - 1 - g, 0, 0))]
    else:
        eshape = jax.ShapeDtypeStruct((S5_SEGS, NS2), F32)
        out_shape = [eshape, eshape]
        out_specs = [pl.BlockSpec((S5_SEGS, NS2), lambda g: (0, 0))] * 2
    return pl.pallas_call(
        functools.partial(_s5_kernel, emit_out=emit_out),
        grid=(steps,),
        in_specs=in_specs,
        out_specs=out_specs,
        out_shape=out_shape,
        scratch_shapes=scratch,
        compiler_params=_cparams(1),
    )(*args)


def _s5_scan(z, col_block, pf, pr, n_ctx):
    S = z.shape[0]
    GW = pf[0].shape[1] * pf[0].shape[0]
    ends = _s5_pass(z, col_block, pf, pr, None, n_ctx)
    yf, yr = _s5_pass(z, col_block, pf, pr, ends, n_ctx)
    return yf.reshape(S, GW), yr.reshape(S, GW)


def _cast_kernel(x_ref, o_ref):
    o_ref[...] = x_ref[...].astype(o_ref.dtype)


def _cast_bf16(w, layer):
    _, K, N = w.shape
    tk = 512
    return pl.pallas_call(
        _cast_kernel,
        grid=(K // tk,),
        in_specs=[pl.BlockSpec((None, tk, N), lambda i: (layer, i, 0))],
        out_specs=pl.BlockSpec((tk, N), lambda i: (i, 0)),
        out_shape=jax.ShapeDtypeStruct((K, N), BF16),
        compiler_params=_cparams(1),
    )(w)


def _mixout_kernel(*refs, alpha, n_ctx_blocks, route, h_dtype):
    (pa_ref, pb_ref, pc_ref, yf_ref, yr_ref, u_ref, d_ref, gw_ref, gb_ref,
     w_ref, x_ref, c_ref, g1_ref, lg_ref, lb_ref, sh_ref, sc_ref) = refs[:17]
    if route:
        rw_ref, x1_ref, h_ref, idx_ref, gate_ref = refs[17:]
    else:
        x1_ref, h_ref = refs[17:]
    i = pl.program_id(0)
    is_ctx = i < n_ctx_blocks
    sy = yf_ref[...] + yr_ref[...] + d_ref[...] * u_ref[...].astype(F32)
    zz = jax.nn.gelu(sy)
    sgate = jnp.dot(zz.astype(BF16), gw_ref[...].astype(BF16), preferred_element_type=F32) + gb_ref[...]
    pd = (zz * jax.nn.sigmoid(sgate)).astype(BF16)
    GW = pa_ref.shape[1]
    mix = jnp.zeros(x_ref.shape, F32)
    for k, part in enumerate((pa_ref[...], pb_ref[...], pc_ref[...], pd)):
        mix = mix + jnp.dot(part.astype(BF16), w_ref[k * GW:(k + 1) * GW, :], preferred_element_type=F32)
    y = alpha * jnp.where(is_ctx, c_ref[...], x_ref[...]) + _pick(g1_ref[...], is_ctx) * mix
    x1 = _layer_norm(y, lg_ref[...], lb_ref[...])
    x1_ref[...] = x1
    h = x1 * (1.0 + _pick(sc_ref[...], is_ctx)) + _pick(sh_ref[...], is_ctx)
    h_ref[...] = h.astype(h_dtype)
    if route:
        rw = rw_ref[...]
        h_hi = h.astype(BF16)
        h_lo = (h - h_hi.astype(F32)).astype(BF16)
        w_hi = rw.astype(BF16)
        w_lo = (rw - w_hi.astype(F32)).astype(BF16)
        logits = (jnp.dot(h_hi, w_hi, preferred_element_type=F32)
                  + (jnp.dot(h_lo, w_hi, preferred_element_type=F32)
                     + jnp.dot(h_hi, w_lo, preferred_element_type=F32)))
        n_exp = rw_ref.shape[1]
        lane = lax.broadcasted_iota(jnp.int32, logits.shape, 1)
        m1 = jnp.max(logits, axis=-1, keepdims=True)
        i1 = jnp.min(jnp.where(logits == m1, lane, n_exp), axis=-1, keepdims=True)
        rest = jnp.where(lane == i1, -jnp.inf, logits)
        m2 = jnp.max(rest, axis=-1, keepdims=True)
        i2 = jnp.min(jnp.where(rest == m2, lane, n_exp), axis=-1, keepdims=True)
        e2 = jnp.exp(m2 - m1)
        idx_ref[...] = jnp.concatenate([i1, i2], axis=1)
        gate_ref[...] = jnp.concatenate([1.0 / (1.0 + e2), e2 / (1.0 + e2)], axis=1)


def _mix_out(parts, s5, w_out_bf, stream, mod, ln_g, ln_b, layer, alpha, n_ctx_blocks, router_w):
    S, D = stream.rows, stream.width
    GW = parts[0].shape[1]
    L = ln_g.shape[0]
    nb = S // ROW_BLOCK
    route = router_w is not None
    h_dtype = F32 if route else BF16
    yf, yr, z, col_block, s5_d, glu_w, glu_b = s5
    part = pl.BlockSpec((ROW_BLOCK, GW), lambda i: (i, 0))
    rot = pl.BlockSpec((ROW_BLOCK, GW), lambda i: ((i + nb - n_ctx_blocks) % nb, 0))
    gvec = pl.BlockSpec((None, 1, GW), lambda i: (layer, 0, 0))
    rows = pl.BlockSpec((ROW_BLOCK, D), lambda i: (i, 0))
    vspec = pl.BlockSpec((None, 1, D), lambda i: (layer, 0, 0))
    in_specs = [part, part, part,
                part, rot, pl.BlockSpec((ROW_BLOCK, GW), lambda i: (i, col_block)), gvec,
                pl.BlockSpec((None, GW, GW), lambda i: (layer, 0, 0)), gvec,
                pl.BlockSpec((D, D), lambda i: (0, 0))] + stream.specs(1) + [
                _mod_spec(layer, 2, D, 1), vspec, vspec, _mod_spec(layer, 3, D, 1), _mod_spec(layer, 4, D, 1)]
    args = list(parts) + [yf, yr, z, s5_d.reshape(L, 1, GW), glu_w, glu_b.reshape(L, 1, GW),
                          w_out_bf] + stream.args() + [mod, ln_g.reshape(L, 1, D), ln_b.reshape(L, 1, D), mod, mod]
    out_specs = [rows, rows]
    out_shape = [jax.ShapeDtypeStruct((S, D), F32), jax.ShapeDtypeStruct((S, D), h_dtype)]
    if route:
        E = router_w.shape[-1]
        in_specs.append(pl.BlockSpec((D, E), lambda i: (0, 0)))
        args.append(router_w)
        out_specs += [pl.BlockSpec((ROW_BLOCK, TOP_K), lambda i: (i, 0))] * 2
        out_shape += [jax.ShapeDtypeStruct((S, TOP_K), jnp.int32), jax.ShapeDtypeStruct((S, TOP_K), F32)]
    return pl.pallas_call(
        functools.partial(_mixout_kernel, alpha=alpha, n_ctx_blocks=n_ctx_blocks, route=route, h_dtype=h_dtype),
        grid=(S // ROW_BLOCK,),
        in_specs=in_specs,
        out_specs=out_specs,
        out_shape=out_shape,
        compiler_params=_cparams(1),
    )(*args)


FFN_TM = 512
FFN_TF = 512
FFN_TN = 512


def _expert_changed(te_ref, i):
    prev = te_ref[jnp.maximum(i - 1, 0)]
    return jnp.logical_or(i == 0, te_ref[i] != prev)


def _ffn_up_kernel(te_ref, src_ref, live_ref, h_ref, w1_ref, w3_ref, o_ref, w1b_ref, w3b_ref):
    i = pl.program_id(1)
    half = h_ref.shape[0] // 2

    @pl.when(_expert_changed(te_ref, i))
    def _():
        w1b_ref[...] = w1_ref[...].astype(BF16)
        w3b_ref[...] = w3_ref[...].astype(BF16)

    def act(h):
        a = jnp.dot(h, w1b_ref[...], preferred_element_type=F32)
        b = jnp.dot(h, w3b_ref[...], preferred_element_type=F32)
        return (_silu(a) * b).astype(o_ref.dtype)

    @pl.when(live_ref[i] == 2)
    def _():
        o_ref[...] = act(h_ref[...])

    @pl.when(live_ref[i] == 1)
    def _():
        o_ref[0:half, :] = act(h_ref[0:half, :])
        o_ref[half:, :] = jnp.zeros((half, o_ref.shape[1]), o_ref.dtype)

    @pl.when(live_ref[i] == 0)
    def _():
        o_ref[...] = jnp.zeros_like(o_ref)


def _ffn_down_kernel(te_ref, src_ref, live_ref, g_ref, w2_ref, o_ref, w2b_ref):
    i = pl.program_id(1)

    @pl.when(_expert_changed(te_ref, i))
    def _():
        w2b_ref[...] = w2_ref[...].astype(BF16)

    @pl.when(live_ref[i] > 0)
    def _():
        o_ref[...] = jnp.dot(g_ref[...], w2b_ref[...], preferred_element_type=F32)

    @pl.when(live_ref[i] == 0)
    def _():
        o_ref[...] = jnp.zeros_like(o_ref)


def _swiglu_tiles(hs, w1, w3, w2, up, down):
    R, D = hs.shape
    _, _, F = w1.shape
    tf = FFN_TF if F % FFN_TF == 0 else F
    tn = FFN_TN
    tm_u, te_u, src_u, live_u = up
    tm_d, te_d, src_d, live_d = down
    assert R % tm_u == 0 and R % tm_d == 0 and F % tf == 0 and D % tn == 0
    g = pl.pallas_call(
        _ffn_up_kernel,
        grid_spec=pltpu.PrefetchScalarGridSpec(
            num_scalar_prefetch=3,
            grid=(F // tf, R // tm_u),
            in_specs=[pl.BlockSpec((tm_u, D), lambda j, i, te, src, lv: (src[i], 0)),
                      pl.BlockSpec((None, D, tf), lambda j, i, te, src, lv: (te[i], 0, j)),
                      pl.BlockSpec((None, D, tf), lambda j, i, te, src, lv: (te[i], 0, j))],
            out_specs=pl.BlockSpec((tm_u, tf), lambda j, i, te, src, lv: (i, j)),
            scratch_shapes=[pltpu.VMEM((D, tf), BF16), pltpu.VMEM((D, tf), BF16)]),
        out_shape=jax.ShapeDtypeStruct((R, F), BF16),
        compiler_params=_cparams(2),
    )(te_u, src_u, live_u, hs, w1, w3)
    return pl.pallas_call(
        _ffn_down_kernel,
        grid_spec=pltpu.PrefetchScalarGridSpec(
            num_scalar_prefetch=3,
            grid=(D // tn, R // tm_d),
            in_specs=[pl.BlockSpec((tm_d, F), lambda j, i, te, src, lv: (src[i], 0)),
                      pl.BlockSpec((None, F, tn), lambda j, i, te, src, lv: (te[i], 0, j))],
            out_specs=pl.BlockSpec((tm_d, tn), lambda j, i, te, src, lv: (i, j)),
            scratch_shapes=[pltpu.VMEM((F, tn), BF16)]),
        out_shape=jax.ShapeDtypeStruct((R, D), F32),
        compiler_params=_cparams(2),
    )(te_d, src_d, live_d, g, w2)


def _row_copy(src_ref, dst_ref, src_row, dst_row, sem):
    return pltpu.make_async_copy(src_ref.at[pl.ds(src_row, 1)], dst_ref.at[pl.ds(dst_row, 1)], sem)


DMA_UNROLL = 8


def _gather_kernel(tok_ref, live_ref, src_ref, o_ref, buf_ref, sem):
    i = pl.program_id(0)
    tm = buf_ref.shape[0]
    live = live_ref[i] > 0

    def issue(r2, c):
        for prio in range(2):
            r = 2 * r2 + prio
            _row_copy(src_ref, buf_ref, tok_ref[i * tm + r], r, sem).start(priority=prio)
        return c

    def drain(r, c):
        _row_copy(src_ref, buf_ref, 0, r, sem).wait()
        return c

    @pl.when(live)
    def _():
        lax.fori_loop(0, tm // 2, issue, 0, unroll=DMA_UNROLL // 2)
        lax.fori_loop(0, tm, drain, 0, unroll=DMA_UNROLL)
        o_ref[...] = buf_ref[...].astype(o_ref.dtype)

    @pl.when(jnp.logical_not(live))
    def _():
        o_ref[...] = jnp.zeros_like(o_ref)


def _gather_rows(src, tok_of_slot, live, tm):
    R = tok_of_slot.shape[0]
    D = src.shape[1]
    return pl.pallas_call(
        _gather_kernel,
        grid_spec=pltpu.PrefetchScalarGridSpec(
            num_scalar_prefetch=2,
            grid=(R // tm,),
            in_specs=[pl.BlockSpec(memory_space=pl.ANY)],
            out_specs=pl.BlockSpec((tm, D), lambda i, tok, nu: (i, 0)),
            scratch_shapes=[pltpu.VMEM((tm, D), src.dtype), pltpu.SemaphoreType.DMA(())]),
        out_shape=jax.ShapeDtypeStruct((R, D), BF16),
        compiler_params=_cparams(1),
    )(tok_of_slot, live, src)


def _ln2_dense_kernel(x_ref, f_ref, g2_ref, lg_ref, lb_ref, o_ref, *, alpha, n_ctx_blocks, row_off):
    is_ctx = (pl.program_id(0) + row_off) < n_ctx_blocks
    y = alpha * x_ref[...] + _pick(g2_ref[...], is_ctx) * f_ref[...]
    o_ref[...] = _layer_norm(y, lg_ref[...], lb_ref[...])


def _ln2_moe_kernel(sa_ref, sb_ref, x_ref, y_ref, gate_ref, g2_ref, lg_ref, lb_ref, o_ref, bufa_ref, bufb_ref,
                    sem, *, alpha, n_ctx_blocks, row_off):
    i = pl.program_id(0)
    R = ROW_BLOCK
    base = (i + row_off) * R

    def issue(r, c):
        _row_copy(y_ref, bufa_ref, sa_ref[base + r], r, sem).start(priority=0)
        _row_copy(y_ref, bufb_ref, sb_ref[base + r], r, sem).start(priority=1)
        return c

    def drain(r, c):
        _row_copy(y_ref, bufa_ref, 0, r, sem).wait()
        _row_copy(y_ref, bufb_ref, 0, r, sem).wait()
        return c

    lax.fori_loop(0, R, issue, 0, unroll=DMA_UNROLL // 2)
    lax.fori_loop(0, R, drain, 0, unroll=DMA_UNROLL // 2)
    gate = gate_ref[...]
    f = gate[:, 0:1] * bufa_ref[...] + gate[:, 1:2] * bufb_ref[...]
    is_ctx = (i + row_off) < n_ctx_blocks
    y = alpha * x_ref[...] + _pick(g2_ref[...], is_ctx) * f
    o_ref[...] = _layer_norm(y, lg_ref[...], lb_ref[...])


def _ln2(x1, f, mod, ln_g, ln_b, layer, alpha, n_ctx_blocks, row_off, moe=None):
    S, D = x1.shape
    L = ln_g.shape[0]
    nb = S // ROW_BLOCK - row_off
    n_pre = 0 if moe is None else 2
    wrap = (lambda f_: (lambda i, *_: f_(i)))
    rows_in = pl.BlockSpec((ROW_BLOCK, D), wrap(lambda i: (i + row_off, 0)))
    rows_out = pl.BlockSpec((ROW_BLOCK, D), wrap(lambda i: (i, 0)))
    vspec = pl.BlockSpec((None, 1, D), wrap(lambda i: (layer, 0, 0)))
    mspec = pl.BlockSpec((None, SUBLANES, D), wrap(lambda i: (layer, 0, 5)))
    common = dict(alpha=alpha, n_ctx_blocks=n_ctx_blocks, row_off=row_off)
    lg, lb = ln_g.reshape(L, 1, D), ln_b.reshape(L, 1, D)
    out_shape = jax.ShapeDtypeStruct((nb * ROW_BLOCK, D), F32)
    if moe is None:
        return pl.pallas_call(
            functools.partial(_ln2_dense_kernel, **common),
            grid=(nb,),
            in_specs=[rows_in, rows_in, mspec, vspec, vspec],
            out_specs=rows_out,
            out_shape=out_shape,
            compiler_params=_cparams(1),
        )(x1, f, mod, lg, lb)
    slot_a, slot_b, gates = moe
    return pl.pallas_call(
        functools.partial(_ln2_moe_kernel, **common),
        grid_spec=pltpu.PrefetchScalarGridSpec(
            num_scalar_prefetch=n_pre,
            grid=(nb,),
            in_specs=[rows_in, pl.BlockSpec(memory_space=pl.ANY),
                      pl.BlockSpec((ROW_BLOCK, TOP_K), wrap(lambda i: (i + row_off, 0))),
                      mspec, vspec, vspec],
            out_specs=rows_out,
            scratch_shapes=[pltpu.VMEM((ROW_BLOCK, D), F32), pltpu.VMEM((ROW_BLOCK, D), F32),
                            pltpu.SemaphoreType.DMA(())]),
        out_shape=out_shape,
        compiler_params=_cparams(1),
    )(slot_a, slot_b, x1, f, gates, mod, lg, lb)


def _route_slots(idx, row0, n_experts, tm):
    S = idx.shape[0]
    n = S - row0
    e_flat = idx[row0:].reshape(-1)
    onehot = (e_flat[:, None] == jnp.arange(n_experts, dtype=jnp.int32)[None, :]).astype(jnp.int32)
    pos = jnp.sum((jnp.cumsum(onehot, axis=0) - 1) * onehot, axis=1)
    counts = jnp.sum(onehot, axis=0)
    big = 2 * tm
    padded = ((counts + big - 1) // big) * big
    ends = jnp.cumsum(padded)
    starts = ends - padded
    slot = starts[e_flat] + pos
    n_big = (TOP_K * n) // big + n_experts
    tok = jnp.repeat(jnp.arange(n, dtype=jnp.int32) + row0, TOP_K)
    tok_of_slot = jnp.full((n_big * big,), row0, jnp.int32).at[slot].set(tok)
    t = jnp.arange(2 * n_big, dtype=jnp.int32)
    te = jnp.minimum(jnp.searchsorted(ends, t * tm, side='right'), n_experts - 1).astype(jnp.int32)
    live = (t * tm < starts[te] + counts[te]).astype(jnp.int32)
    src = jnp.maximum(lax.cummax(jnp.where(live > 0, t, -1)), 0)
    down = (tm, te[src], src, live)
    live_big = live[0::2] + live[1::2]
    src_big = src[0::2] // 2
    up = (big, te[0::2][src_big], src_big, live_big)
    slot2 = slot.reshape(n, TOP_K).astype(jnp.int32)
    pad = jnp.zeros((row0,), jnp.int32)
    slot_a = jnp.concatenate([pad, slot2[:, 0]])
    slot_b = jnp.concatenate([pad, slot2[:, 1]])
    return tok_of_slot, up, down, slot_a, slot_b


def kernel(x, c, ctx, c_ctx, w_mod, b_mod, w_in, w_out, ln1_g, ln1_b, ln2_g, ln2_b, pool_w, pool_scale,
           diff_lambda, diff_subln_g, conv_dw, conv_db, conv_ln_g, conv_ln_b, conv_pw, s5_a_re, s5_a_im,
           s5_log_dt, s5_b_re, s5_b_im, s5_c_re, s5_c_im, s5_d, s5_glu_w, s5_glu_b, ffn_w1, ffn_w3, ffn_w2,
           router_w, moe_w1, moe_w3, moe_w2):
    B, T, D = x.shape
    Tc = ctx.shape[1]
    depth = w_mod.shape[0]
    assert B == 1 and Tc % ROW_BLOCK == 0 and T % ROW_BLOCK == 0
    GW = D // N_GROUPS
    n_ctx_blocks = Tc // ROW_BLOCK
    alpha = (2.0 * depth) ** 0.25

    cc = jnp.zeros((SUBLANES, D), F32).at[0].set(c[0]).at[1].set(c_ctx)
    mod = _modulation(cc, w_mod, b_mod)
    cos, sin = _rope_tables(T, Tc, LANES)
    stream = _Stream(x[0], ctx[0], n_ctx_blocks)

    POOL_B, Q_B, K_B, V_B, CONV_B, S5_B = 0, 1, 2, 3, 2, 6

    for l in range(depth):
        last = l == depth - 1
        lam_init = 0.8 - 0.6 * math.exp(-0.3 * l)
        z = _in_projection(stream, mod, w_in, l, n_ctx_blocks)

        pa = _pool_mixer(z, pool_w, pool_scale, l, n_ctx_blocks)

        qt, kk, vt, kn2 = _qkv_prep(z, cos, sin, GW, Q_B, K_B, V_B)
        kn = jnp.sqrt(jnp.max(kn2[n_ctx_blocks:, :, 0], axis=0)) * (1.0 + 2.0 ** -6)
        lv = diff_lambda[l].astype(F32)
        lam = jnp.exp(jnp.sum(lv[0] * lv[1])) - jnp.exp(jnp.sum(lv[2] * lv[3])) + lam_init
        lam_row = jnp.full((1, ATTN_TQ), lam, F32)
        g_col = (diff_subln_g[l].astype(F32) * (1.0 - lam_init)).reshape(GW, 1)
        pb = _diff_attention(qt, kk, vt, lam_row, g_col, kn, Tc)

        pcv = _conv_mixer(z, CONV_B, conv_dw, conv_db, conv_ln_g, conv_ln_b, conv_pw, l, n_ctx_blocks)

        seg_len = stream.rows // S5_SEGS
        pf = _s5_params(s5_a_re[l, 0], s5_a_im[l, 0], s5_log_dt[l, 0], s5_b_re[l, 0], s5_b_im[l, 0],
                        s5_c_re[l, 0], s5_c_im[l, 0], seg_len)
        pr = _s5_params(s5_a_re[l, 1], s5_a_im[l, 1], s5_log_dt[l, 1], s5_b_re[l, 1], s5_b_im[l, 1],
                        s5_c_re[l, 1], s5_c_im[l, 1], seg_len)
        yf, yr = _s5_scan(z, S5_B, pf, pr, Tc)
        s5 = (yf, yr, z, S5_B, s5_d, s5_glu_w, s5_glu_b)

        w_out_bf = _cast_bf16(w_out, l)
        row_off = n_ctx_blocks if last else 0
        if l % 2 == 0:
            x1, h = _mix_out((pa, pb, pcv), s5, w_out_bf, stream, mod, ln1_g, ln1_b, l, alpha, n_ctx_blocks, None)
            S = stream.rows
            def dense_tiles(sizes, live):
                tm = next(t for t in sizes if S % t == 0)
                nt = S // tm
                return (tm, jnp.zeros((nt,), jnp.int32), jnp.arange(nt, dtype=jnp.int32),
                        jnp.full((nt,), live, jnp.int32))
            f = _swiglu_tiles(h, ffn_w1[l // 2][None], ffn_w3[l // 2][None], ffn_w2[l // 2][None],
                              dense_tiles((1408, 768, 512, ROW_BLOCK), 2), dense_tiles((768, 512, ROW_BLOCK), 1))
            xs_new = _ln2(x1, f, mod, ln2_g, ln2_b, l, alpha, n_ctx_blocks, row_off)
        else:
            x1, h, idx, gates = _mix_out((pa, pb, pcv), s5, w_out_bf, stream, mod, ln1_g, ln1_b, l, alpha,
                                         n_ctx_blocks, router_w[l // 2])
            n_exp = router_w.shape[-1]
            row0 = row_off * ROW_BLOCK
            tok_of_slot, up, down, slot_a, slot_b = _route_slots(idx, row0, n_exp, FFN_TM)
            hs = _gather_rows(h, tok_of_slot, down[3], FFN_TM)
            y = _swiglu_tiles(hs, moe_w1[l // 2], moe_w3[l // 2], moe_w2[l // 2], up, down)
            xs_new = _ln2(x1, y, mod, ln2_g, ln2_b, l, alpha, n_ctx_blocks, row_off, moe=(slot_a, slot_b, gates))
        stream = _Stream(xs_new, xs_new, n_ctx_blocks)
    return xs_new[None]
```
